```python
import jax, jax.numpy as jnp
from jax import lax
import numpy as np

D_MODEL = 1024
BATCH = 16
SEQ = 2048
DEPTH = 4

GRID_W = 64
CTX_LEN = 256
N_BRANCH = 3
D_CONV = 512
CONV_WIDTH = 31
D_MLSTM = 1024
MLSTM_HEADS = 4
MLSTM_HEAD_DIM = D_MLSTM // MLSTM_HEADS
QK_BLOCK = 4
SHORT_CONV = 4
MLSTM_CHUNK = 64
D_POOL = 512
POOL_WINDOWS = (2, 4, 8, 16)
POOL_GROUP = D_POOL // len(POOL_WINDOWS)
N_EXPERTS = 32
TOP_K = 4
D_FF = 1024
SWIGLU_LIMIT = 7.0
SWIGLU_ALPHA = 1.702
MOE_BLOCK = 128
RMS_EPS = 1e-6
LN_EPS = 1e-5
COL_GATE = 0
COL_CONV = N_BRANCH * D_MODEL
COL_POOL = COL_CONV + 2 * D_CONV
COL_MZ = COL_POOL + D_POOL
COL_MQK = COL_MZ + D_MLSTM
COL_MV = COL_MQK + D_MLSTM
D_IN = COL_MV + D_MLSTM

kernel_name = 'hybrid_conv_mlstm_pool_moe_dit'


def rmsnorm(x, g):
    xf = x.astype(jnp.float32)
    y = xf * lax.rsqrt(jnp.mean(xf * xf, axis=-1, keepdims=True) + RMS_EPS)
    return y.astype(x.dtype) * g


def layernorm(x, g, b):
    xf = x.astype(jnp.float32)
    mu = jnp.mean(xf, axis=-1, keepdims=True)
    var = jnp.mean(jnp.square(xf - mu), axis=-1, keepdims=True)
    return ((xf - mu) * lax.rsqrt(var + LN_EPS)).astype(x.dtype) * g + b


def depthwise_conv(x, w, b, pad_l, pad_r):
    ch = x.shape[-1]
    y = lax.conv_general_dilated(x, w[:, None, :].astype(x.dtype), (1,), [(pad_l, pad_r)],
                                 dimension_numbers=('NWC', 'WIO', 'NWC'), feature_group_count=ch)
    return y + b


def conv_module(u, dw_w, dw_b, ln_g, ln_b, w_o):
    a = u[..., :D_CONV] * jax.nn.sigmoid(u[..., D_CONV:])
    a = depthwise_conv(a, dw_w, dw_b, CONV_WIDTH // 2, CONV_WIDTH // 2)
    return jax.nn.silu(layernorm(a, ln_g, ln_b)) @ w_o


def pool_module(u, pw, ps, w_o):
    n, t, _ = u.shape
    pref = jnp.concatenate([jnp.zeros((n, 1, D_POOL), jnp.float32),
                            jnp.cumsum(u.astype(jnp.float32), axis=1)], axis=1)
    pos = jnp.arange(t)
    means = []
    for gi, w in enumerate(POOL_WINDOWS):
        lo = jnp.clip(pos - w // 2, 0, t - 1)
        hi = jnp.clip(pos + w // 2 - 1, 0, t - 1)
        pg = pref[..., gi * POOL_GROUP:(gi + 1) * POOL_GROUP]
        cnt = (hi - lo + 1).astype(jnp.float32)
        means.append((pg[:, hi + 1] - pg[:, lo]) / cnt[None, :, None])
    d = (jnp.concatenate(means, axis=-1) - u.astype(jnp.float32)).astype(u.dtype)
    d = jnp.einsum('ntgc,gcd->ntgd', d.reshape(n, t, len(POOL_WINDOWS), POOL_GROUP), pw)
    return (d.reshape(n, t, D_POOL) * ps) @ w_o


def headwise(a, w):
    bsz, t, _ = a.shape
    y = jnp.einsum('btnc,ncd->btnd', a.reshape(bsz, t, D_MLSTM // QK_BLOCK, QK_BLOCK), w)
    return y.reshape(bsz, t, D_MLSTM)


def split_heads(a):
    bsz, t, _ = a.shape
    return a.reshape(bsz, t, MLSTM_HEADS, MLSTM_HEAD_DIM).transpose(0, 2, 1, 3)


def mlstm_inputs(u, v, sc_w, sc_b, wq, wk):
    a = jax.nn.silu(depthwise_conv(u, sc_w, sc_b, SHORT_CONV // 2, SHORT_CONV - 1 - SHORT_CONV // 2))
    q = headwise(a, wq)
    k = headwise(a, wk)
    feat = jnp.concatenate([q, k, v], axis=-1)
    return a, split_heads(q), split_heads(k), split_heads(v), feat


def mlstm_gates(feat, w, b):
    pre = (feat @ w + b).astype(jnp.float32)
    log_i = pre[..., :MLSTM_HEADS]
    log_f = jax.nn.log_sigmoid(pre[..., MLSTM_HEADS:])
    return log_i.transpose(0, 2, 1), log_f.transpose(0, 2, 1)


def zero_state(bsz):
    return (jnp.zeros((bsz, MLSTM_HEADS, MLSTM_HEAD_DIM, MLSTM_HEAD_DIM), jnp.float32),
            jnp.zeros((bsz, MLSTM_HEADS, MLSTM_HEAD_DIM), jnp.float32),
            jnp.zeros((bsz, MLSTM_HEADS), jnp.float32))


def mlstm_scan(q, k, v, log_i, log_f, state):
    bsz, nh, t, dh = q.shape
    nc = t // MLSTM_CHUNK

    def chunks(a):
        return jnp.moveaxis(a.reshape(bsz, nh, nc, MLSTM_CHUNK, *a.shape[3:]), 2, 0)

    xs = (chunks(q.astype(jnp.float32) * dh ** -0.5), chunks(k.astype(jnp.float32)),
          chunks(v.astype(jnp.float32)), chunks(log_i), chunks(log_f))
    tri = jnp.tril(jnp.ones((MLSTM_CHUNK, MLSTM_CHUNK), dtype=bool))

    def step(carry, inp):
        cmat, nvec, mval = carry
        qc, kc, vc, lic, lfc = inp
        b = jnp.cumsum(lfc, axis=-1)
        log_d = jnp.where(tri, b[..., :, None] - b[..., None, :] + lic[..., None, :], -jnp.inf)
        inter = b + mval[..., None]
        m_t = jnp.maximum(inter, jnp.max(log_d, axis=-1))
        s = jnp.einsum('bhtd,bhjd->bhtj', qc, kc) * jnp.exp(log_d - m_t[..., None])
        w_inter = jnp.exp(inter - m_t)
        num = jnp.einsum('bhtj,bhjd->bhtd', s, vc) + w_inter[..., None] * jnp.einsum('bhvk,bhtk->bhtv', cmat, qc)
        den = jnp.sum(s, axis=-1) + w_inter * jnp.einsum('bhk,bhtk->bht', nvec, qc)
        h = num / jnp.maximum(jnp.abs(den), jnp.exp(-m_t))[..., None]
        b_end = b[..., -1]
        g = b_end[..., None] - b + lic
        m_new = jnp.maximum(b_end + mval, jnp.max(g, axis=-1))
        wj = jnp.exp(g - m_new[..., None])
        decay = jnp.exp(b_end + mval - m_new)
        cmat = decay[..., None, None] * cmat + jnp.einsum('bhj,bhjv,bhjk->bhvk', wj, vc, kc)
        nvec = decay[..., None] * nvec + jnp.einsum('bhj,bhjk->bhk', wj, kc)
        return (cmat, nvec, m_new), h

    state, h = lax.scan(step, state, xs)
    h = jnp.moveaxis(h, 0, 2).reshape(bsz, nh, t, dh)
    return h, state


def mlstm_direction(q, k, v, log_i, log_f, state, reverse):
    if reverse:
        q, k, v, log_i, log_f = (jnp.flip(a, axis=2) for a in (q, k, v, log_i, log_f))
    h, state = mlstm_scan(q, k, v, log_i, log_f, state)
    if reverse:
        h = jnp.flip(h, axis=2)
    return h, state


def mlstm_out(h, a, z, norm_g, skip, w_o):
    mu = jnp.mean(h, axis=-1, keepdims=True)
    var = jnp.mean(jnp.square(h - mu), axis=-1, keepdims=True)
    hn = (h - mu) * lax.rsqrt(var + LN_EPS)
    bsz, nh, t, dh = h.shape
    hn = hn.transpose(0, 2, 1, 3).reshape(bsz, t, D_MLSTM).astype(a.dtype) * norm_g
    return ((hn + skip * a) * jax.nn.silu(z)) @ w_o


def merge_branches(gate_pre, y_conv, y_m, y_pool, w_out):
    g = jax.nn.sigmoid(gate_pre)
    merged = (g[..., :D_MODEL] * y_conv + g[..., D_MODEL:2 * D_MODEL] * y_m
              + g[..., 2 * D_MODEL:] * y_pool)
    return merged @ w_out


def mixer_sublayer(hx, hc, rows, need_ctx, w_in, conv_dw_w, conv_dw_b, conv_ln_g, conv_ln_b, w_conv_out,
                   m_sc_w, m_sc_b, m_wq, m_wk, m_w_if, m_b_if, m_norm_g, m_skip, w_m_out,
                   pool_w, pool_scale, w_pool_out, w_out):
    bsz, seq, _ = hx.shape
    px = hx @ w_in
    off = 0 if need_ctx else COL_MQK
    pc = hc @ w_in[:, off:]
    a_x, q_x, k_x, v_x, f_x = mlstm_inputs(px[..., COL_MQK:COL_MV], px[..., COL_MV:], m_sc_w, m_sc_b, m_wq, m_wk)
    a_c, q_c, k_c, v_c, f_c = mlstm_inputs(pc[..., COL_MQK - off:COL_MV - off], pc[..., COL_MV - off:],
                                           m_sc_w, m_sc_b, m_wq, m_wk)
    h_x = []
    h_c = []
    for d, reverse in enumerate((False, True)):
        li_c, lf_c = mlstm_gates(f_c, m_w_if[d], m_b_if[d])
        li_x, lf_x = mlstm_gates(f_x, m_w_if[d], m_b_if[d])
        hcd, st = mlstm_direction(q_c, k_c, v_c, li_c, lf_c, zero_state(bsz), reverse)
        hxd, _ = mlstm_direction(q_x, k_x, v_x, li_x, lf_x, st, reverse)
        h_x.append(hxd)
        if need_ctx:
            h_c.append(hcd)
    y_m_x = mlstm_out(h_x[0] + h_x[1], a_x, px[..., COL_MZ:COL_MQK], m_norm_g, m_skip, w_m_out)
    u = px[..., COL_CONV:COL_POOL].reshape(bsz * rows, GRID_W, 2 * D_CONV)
    y_conv_x = conv_module(u, conv_dw_w, conv_dw_b, conv_ln_g, conv_ln_b, w_conv_out).reshape(bsz, seq, D_MODEL)
    u = px[..., COL_POOL:COL_MZ].reshape(bsz, rows, GRID_W, D_POOL).transpose(0, 2, 1, 3)
    y_pool_x = pool_module(u.reshape(bsz * GRID_W, rows, D_POOL), pool_w, pool_scale, w_pool_out)
    y_pool_x = y_pool_x.reshape(bsz, GRID_W, rows, D_MODEL).transpose(0, 2, 1, 3).reshape(bsz, seq, D_MODEL)
    y_x = merge_branches(px[..., COL_GATE:COL_CONV], y_conv_x, y_m_x, y_pool_x, w_out)
    if not need_ctx:
        return y_x, None
    y_m_c = mlstm_out(h_c[0] + h_c[1], a_c, pc[..., COL_MZ:COL_MQK], m_norm_g, m_skip, w_m_out)
    y_conv_c = conv_module(pc[..., COL_CONV:COL_POOL], conv_dw_w, conv_dw_b, conv_ln_g, conv_ln_b, w_conv_out)
    y_pool_c = pool_module(pc[..., COL_POOL:COL_MZ], pool_w, pool_scale, w_pool_out)
    y_c = merge_branches(pc[..., COL_GATE:COL_CONV], y_conv_c, y_m_c, y_pool_c, w_out)
    return y_x, y_c


def moe_ffn(h, router_w, router_b, w_gu, b_gu, w_dn, b_dn):
    n_tok, d = h.shape
    logits = (h @ router_w + router_b).astype(jnp.float32)
    top_v, top_e = lax.top_k(logits, TOP_K)
    gate = jax.nn.softmax(top_v, axis=-1)
    flat_e = top_e.reshape(-1)
    n_assign = n_tok * TOP_K
    order = jnp.argsort(flat_e)
    sorted_e = flat_e[order]
    counts = jnp.bincount(flat_e, length=N_EXPERTS)
    padded = (counts + MOE_BLOCK - 1) // MOE_BLOCK * MOE_BLOCK
    pad_end = jnp.cumsum(padded)
    pad_start = pad_end - padded
    grp_start = jnp.cumsum(counts) - counts
    dest = pad_start[sorted_e] + jnp.arange(n_assign) - grp_start[sorted_e]
    n_blocks = -(-n_assign // MOE_BLOCK) + N_EXPERTS
    n_rows = n_blocks * MOE_BLOCK
    row_tok = jnp.zeros((n_rows,), jnp.int32).at[dest].set((order // TOP_K).astype(jnp.int32))
    row_w = jnp.zeros((n_rows,), jnp.float32).at[dest].set(gate.reshape(-1)[order])
    block_e = jnp.minimum(jnp.searchsorted(pad_end, jnp.arange(n_blocks) * MOE_BLOCK, side='right'),
                          N_EXPERTS - 1)

    def expert_block(args):
        tok, e = args
        gu = h[tok] @ w_gu[e] + b_gu[e]
        g = jnp.minimum(gu[:, :D_FF], SWIGLU_LIMIT)
        u = jnp.clip(gu[:, D_FF:], -SWIGLU_LIMIT, SWIGLU_LIMIT)
        return ((u + 1) * (g * jax.nn.sigmoid(SWIGLU_ALPHA * g))) @ w_dn[e] + b_dn[e]

    y = lax.map(expert_block, (row_tok.reshape(n_blocks, MOE_BLOCK), block_e))
    y = y.reshape(n_rows, d) * row_w[:, None].astype(h.dtype)
    return jnp.zeros_like(h).at[row_tok].add(y)


def setup_inputs(seed: int = 0) -> dict:
    key = jax.random.key(seed)
    ks = jax.random.split(key, 36)
    f32 = jnp.float32

    def nrm(k, shape, scale):
        return jax.random.normal(k, shape, f32) * scale

    def gain(k, shape):
        return 1.0 + 0.02 * jax.random.normal(k, shape, f32)

    L, D, H = DEPTH, D_MODEL, MLSTM_HEADS
    ig_b = nrm(ks[34], (L, 2, H), 0.1)
    fg_b = jnp.linspace(3.0, 6.0, H, dtype=f32) + nrm(ks[35], (L, 2, H), 0.1)
    return {
        'x': nrm(ks[0], (BATCH, SEQ, D), 1.0),
        'c': nrm(ks[1], (BATCH, D), 1.0),
        'ctx': nrm(ks[2], (BATCH, CTX_LEN, D), 1.0),
        'c_ctx': nrm(ks[3], (D,), 1.0),
        'w_ada': nrm(ks[4], (L, D, 6 * D), 0.5 * D ** -0.5),
        'b_ada': nrm(ks[5], (L, 6 * D), 0.02),
        'norm1_g': gain(ks[6], (L, D)),
        'norm2_g': gain(ks[7], (L, D)),
        'w_in': nrm(ks[8], (L, D, D_IN), D ** -0.5),
        'conv_dw_w': nrm(ks[9], (L, CONV_WIDTH, D_CONV), CONV_WIDTH ** -0.5),
        'conv_dw_b': nrm(ks[10], (L, D_CONV), 0.02),
        'conv_ln_g': gain(ks[11], (L, D_CONV)),
        'conv_ln_b': nrm(ks[12], (L, D_CONV), 0.02),
        'w_conv_out': nrm(ks[13], (L, D_CONV, D), D_CONV ** -0.5),
        'm_sc_w': nrm(ks[14], (L, SHORT_CONV, D_MLSTM), SHORT_CONV ** -0.5),
        'm_sc_b': nrm(ks[15], (L, D_MLSTM), 0.02),
        'm_wq': nrm(ks[16], (L, D_MLSTM // QK_BLOCK, QK_BLOCK, QK_BLOCK), QK_BLOCK ** -0.5),
        'm_wk': nrm(ks[17], (L, D_MLSTM // QK_BLOCK, QK_BLOCK, QK_BLOCK), QK_BLOCK ** -0.5),
        'm_w_if': nrm(ks[18], (L, 2, 3 * D_MLSTM, 2 * H), 0.02),
        'm_b_if': jnp.concatenate([ig_b, fg_b], axis=-1),
        'm_norm_g': gain(ks[19], (L, D_MLSTM)),
        'm_skip': gain(ks[20], (L, D_MLSTM)),
        'w_m_out': nrm(ks[21], (L, D_MLSTM, D), D_MLSTM ** -0.5),
        'pool_w': nrm(ks[22], (L, len(POOL_WINDOWS), POOL_GROUP, POOL_GROUP), POOL_GROUP ** -0.5),
        'pool_scale': gain(ks[23], (L, D_POOL)),
        'w_pool_out': nrm(ks[24], (L, D_POOL, D), D_POOL ** -0.5),
        'w_out': nrm(ks[25], (L, D, D), D ** -0.5),
        'router_w': nrm(ks[26], (L, D, N_EXPERTS), D ** -0.5),
        'router_b': nrm(ks[27], (L, N_EXPERTS), 0.01),
        'w_gu': nrm(ks[28], (L, N_EXPERTS, D, 2 * D_FF), D ** -0.5),
        'b_gu': nrm(ks[29], (L, N_EXPERTS, 2 * D_FF), 0.02),
        'w_dn': nrm(ks[30], (L, N_EXPERTS, D_FF, D), D_FF ** -0.5),
        'b_dn': nrm(ks[31], (L, N_EXPERTS, D), 0.02),
        'final_g': gain(ks[32], (D,)),
    }


def reference(x, c, ctx, c_ctx, w_ada, b_ada, norm1_g, norm2_g, w_in, conv_dw_w, conv_dw_b, conv_ln_g,
              conv_ln_b, w_conv_out, m_sc_w, m_sc_b, m_wq, m_wk, m_w_if, m_b_if, m_norm_g, m_skip, w_m_out,
              pool_w, pool_scale, w_pool_out, w_out, router_w, router_b, w_gu, b_gu, w_dn, b_dn, final_g):
    bsz, seq, d = x.shape
    rows = seq // GRID_W
    ctx_len = ctx.shape[1]
    s_c = jax.nn.silu(c)
    s_cc = jax.nn.silu(c_ctx)
    for l in range(DEPTH):
        need_ctx = l < DEPTH - 1
        mod_x = jnp.split((s_c @ w_ada[l] + b_ada[l])[:, None, :], 6, axis=-1)
        mod_c = jnp.split(s_cc @ w_ada[l] + b_ada[l], 6, axis=-1)
        hx = rmsnorm(x, norm1_g[l]) * (1 + mod_x[1]) + mod_x[0]
        hc = rmsnorm(ctx, norm1_g[l]) * (1 + mod_c[1]) + mod_c[0]
        y_x, y_c = mixer_sublayer(hx, hc, rows, need_ctx, w_in[l], conv_dw_w[l], conv_dw_b[l], conv_ln_g[l],
                                  conv_ln_b[l], w_conv_out[l], m_sc_w[l], m_sc_b[l], m_wq[l], m_wk[l],
                                  m_w_if[l], m_b_if[l], m_norm_g[l], m_skip[l], w_m_out[l],
                                  pool_w[l], pool_scale[l], w_pool_out[l], w_out[l])
        x = x + mod_x[2] * y_x
        hx = rmsnorm(x, norm2_g[l]) * (1 + mod_x[4]) + mod_x[3]
        if need_ctx:
            ctx = ctx + mod_c[2] * y_c
            hc = rmsnorm(ctx, norm2_g[l]) * (1 + mod_c[4]) + mod_c[3]
            tokens = jnp.concatenate([hx.reshape(-1, d), hc.reshape(-1, d)], axis=0)
            y = moe_ffn(tokens, router_w[l], router_b[l], w_gu[l], b_gu[l], w_dn[l], b_dn[l])
            x = x + mod_x[5] * y[:bsz * seq].reshape(bsz, seq, d)
            ctx = ctx + mod_c[5] * y[bsz * seq:].reshape(bsz, ctx_len, d)
        else:
            y = moe_ffn(hx.reshape(-1, d), router_w[l], router_b[l], w_gu[l], b_gu[l], w_dn[l], b_dn[l])
            x = x + mod_x[5] * y.reshape(bsz, seq, d)
    return rmsnorm(x, final_g)
```

```python
import functools

import jax
import jax.numpy as jnp
from jax import lax
from jax.experimental import pallas as pl
from jax.experimental.pallas import tpu as pltpu

F32 = jnp.float32
BF16 = jnp.bfloat16
HIGHEST = lax.Precision.HIGHEST

TILE = 256
GRID_W = 64
CONV_WIDTH = 31
CONV_HALO = 16
SHORT_CONV = 4
HEADS = 4
POOL_WINDOWS = (2, 4, 8, 16)
N_EXPERTS = 32
TOP_K = 4
MOE_BM = 256
SWIGLU_LIMIT = 7.0
SWIGLU_ALPHA = 1.702
RMS_EPS = 1e-6
LN_EPS = 1e-5
NEG = -1e30
VMEM_LIMIT = 56 * 1024 * 1024


def _cparams(sem):
    return pltpu.CompilerParams(dimension_semantics=sem, vmem_limit_bytes=VMEM_LIMIT)


def _sigmoid(x):
    return jax.nn.sigmoid(x)


def _log_sigmoid(x):
    return jnp.minimum(x, 0.0) - jnp.log1p(jnp.exp(-jnp.abs(x)))


def _ada_kernel(c_ref, w_ref, b_ref, o_ref):
    s = c_ref[...]
    s = s * _sigmoid(s)
    o_ref[...] = jnp.dot(s, w_ref[...], precision=HIGHEST, preferred_element_type=F32) + b_ref[...]


def _ada(c_all, w_ada, b_ada):
    depth, d, n = w_ada.shape
    mp = c_all.shape[0]
    tn = 1536
    return pl.pallas_call(
        _ada_kernel,
        grid=(depth, n // tn),
        in_specs=[
            pl.BlockSpec((mp, d), lambda l, j: (0, 0)),
            pl.BlockSpec((None, d, tn), lambda l, j: (l, 0, j)),
            pl.BlockSpec((None, 1, tn), lambda l, j: (l, 0, j)),
        ],
        out_specs=pl.BlockSpec((None, mp, tn), lambda l, j: (l, 0, j)),
        out_shape=jax.ShapeDtypeStruct((depth, mp, n), F32),
        compiler_params=_cparams(("arbitrary", "arbitrary")),
        name="ada",
    )(c_all, w_ada, b_ada.reshape(depth, 1, n))


def _in_kernel(x_ref, g_ref, mod_ref, w_ref, o_ref):
    x = x_ref[...]
    ms = jnp.mean(x * x, axis=-1, keepdims=True)
    y = x * lax.rsqrt(ms + RMS_EPS) * g_ref[...]
    h = y * (1.0 + mod_ref[1:2, :]) + mod_ref[0:1, :]
    o_ref[...] = jnp.dot(h.astype(BF16), w_ref[...], preferred_element_type=F32).astype(BF16)


def _mod_row(ntb, base):
    def f(i):
        return base + 2 * (i // ntb) + jnp.minimum(i % ntb, 1)
    return f


def _in_proj(xt, g1, modtab, w_in_bf, ntb, mod_base):
    t, d = xt.shape
    n = w_in_bf.shape[1]
    row = _mod_row(ntb, mod_base)
    return pl.pallas_call(
        _in_kernel,
        grid=(t // TILE,),
        in_specs=[
            pl.BlockSpec((TILE, d), lambda i: (i, 0)),
            pl.BlockSpec((1, d), lambda i: (0, 0)),
            pl.BlockSpec((None, 6, d), lambda i: (row(i), 0, 0)),
            pl.BlockSpec((d, n), lambda i: (0, 0), pipeline_mode=pl.Buffered(1)),
        ],
        out_specs=pl.BlockSpec((TILE, n), lambda i: (i, 0)),
        out_shape=jax.ShapeDtypeStruct((t, n), BF16),
        compiler_params=_cparams(("arbitrary",)),
        name="in_proj",
    )(xt, g1, modtab, w_in_bf)


def _mpre_kernel(ntb, u_ref, up_ref, un_ref, v_ref, scw_ref, scb_ref, wq_ref, wk_ref, wif_ref, wift_ref,
                 bif_ref, bift_ref, a_ref, q_ref, k_ref, g_ref, gt_ref):
    i = pl.program_id(1)
    d = u_ref.shape[1]
    dh = d // HEADS
    u = u_ref[...].astype(F32)
    up = jnp.where(i >= 2, up_ref[...].astype(F32)[8:16, :], 0.0)
    un = jnp.where(jnp.logical_and(i >= 1, i <= ntb - 2), un_ref[...].astype(F32)[0:8, :], 0.0)
    ext = jnp.concatenate([up, u, un], axis=0)
    n_ext = TILE + 16
    conv = scb_ref[...]
    for j in range(SHORT_CONV):
        sh = (SHORT_CONV // 2 - j) % n_ext
        shifted = ext if sh == 0 else pltpu.roll(ext, sh, 0)
        conv = conv + scw_ref[j:j + 1, :] * shifted[8:8 + TILE, :]
    a = conv * _sigmoid(conv)
    ab = a.astype(BF16)
    a_ref[...] = ab
    qs, ks = [], []
    for hb in range(HEADS):
        blk = ab[:, hb * dh:(hb + 1) * dh]
        qs.append(jnp.dot(blk, wq_ref[hb], preferred_element_type=F32))
        ks.append(jnp.dot(blk, wk_ref[hb], preferred_element_type=F32))
    q = jnp.concatenate(qs, axis=1)
    k = jnp.concatenate(ks, axis=1)
    qb = q.astype(BF16)
    kb = k.astype(BF16)
    vb = v_ref[...]
    q_ref[...] = (q * (dh ** -0.5)).astype(BF16)
    k_ref[...] = kb
    pre = bif_ref[...]
    pret = bift_ref[...]
    nt_dims = (((1,), (1,)), ((), ()))
    for part, f in enumerate((qb, kb, vb)):
        pre = pre + jnp.dot(f, wif_ref[part * d:(part + 1) * d, :], preferred_element_type=F32)
        pret = pret + lax.dot_general(wift_ref[:, part * d:(part + 1) * d], f, nt_dims,
                                      preferred_element_type=F32)
    col = lax.broadcasted_iota(jnp.int32, pre.shape, 1)
    pre = jnp.where(col % (2 * HEADS) >= HEADS, _log_sigmoid(pre), pre)
    rowi = lax.broadcasted_iota(jnp.int32, pret.shape, 0)
    pret = jnp.where(rowi % (2 * HEADS) >= HEADS, _log_sigmoid(pret), pret)
    g_ref[0] = pre[:, 0:2 * HEADS]
    g_ref[1] = pre[:, 2 * HEADS:4 * HEADS]
    gt_ref[0] = pret[0:2 * HEADS, :]
    gt_ref[1] = pret[2 * HEADS:4 * HEADS, :]


def _mlstm_pre(px, col_u, col_v, sc_w, sc_b, wq_bd, wk_bd, wif, wift, bif, bift, bsz, ntb):
    t = px.shape[0]
    d = sc_w.shape[1]
    hrows = 16
    nh = t // hrows
    per = TILE // hrows

    def tile(b, i):
        return b * ntb + i

    gh = 2 * HEADS
    return pl.pallas_call(
        functools.partial(_mpre_kernel, ntb),
        grid=(bsz, ntb),
        in_specs=[
            pl.BlockSpec((TILE, d), lambda b, i: (tile(b, i), col_u)),
            pl.BlockSpec((hrows, d), lambda b, i: (jnp.maximum(tile(b, i) * per - 1, 0), col_u)),
            pl.BlockSpec((hrows, d), lambda b, i: (jnp.minimum((tile(b, i) + 1) * per, nh - 1), col_u)),
            pl.BlockSpec((TILE, d), lambda b, i: (tile(b, i), col_v)),
            pl.BlockSpec((SHORT_CONV, d), lambda b, i: (0, 0)),
            pl.BlockSpec((1, d), lambda b, i: (0, 0)),
            pl.BlockSpec((HEADS, d // HEADS, d // HEADS), lambda b, i: (0, 0, 0)),
            pl.BlockSpec((HEADS, d // HEADS, d // HEADS), lambda b, i: (0, 0, 0)),
            pl.BlockSpec((3 * d, 2 * gh), lambda b, i: (0, 0)),
            pl.BlockSpec((2 * gh, 3 * d), lambda b, i: (0, 0)),
            pl.BlockSpec((1, 2 * gh), lambda b, i: (0, 0)),
            pl.BlockSpec((2 * gh, 1), lambda b, i: (0, 0)),
        ],
        out_specs=[
            pl.BlockSpec((TILE, d), lambda b, i: (tile(b, i), 0)),
            pl.BlockSpec((TILE, d), lambda b, i: (tile(b, i), 0)),
            pl.BlockSpec((TILE, d), lambda b, i: (tile(b, i), 0)),
            pl.BlockSpec((2, TILE, gh), lambda b, i: (0, tile(b, i), 0)),
            pl.BlockSpec((2, gh, TILE), lambda b, i: (0, 0, tile(b, i))),
        ],
        out_shape=[
            jax.ShapeDtypeStruct((t, d), BF16),
            jax.ShapeDtypeStruct((t, d), BF16),
            jax.ShapeDtypeStruct((t, d), BF16),
            jax.ShapeDtypeStruct((2, t, gh), F32),
            jax.ShapeDtypeStruct((2, gh, t), F32),
        ],
        compiler_params=_cparams(("arbitrary", "arbitrary")),
        name="mlstm_pre",
    )(px, px, px, px, sc_w, sc_b, wq_bd, wk_bd, wif, wift, bif, bift)


def _scan_kernel(q_ref, k_ref, v_ref, g_ref, gt_ref, h_ref, ct_s, n_s, m_s):
    dr = pl.program_id(1)
    i = pl.program_id(2)
    dh = q_ref.shape[1] // HEADS

    @pl.when(i == 0)
    def _():
        ct_s[...] = jnp.zeros(ct_s.shape, F32)
        n_s[...] = jnp.zeros(n_s.shape, F32)
        m_s[...] = jnp.zeros(m_s.shape, F32)

    x = g_ref[...]
    xt = gt_ref[...]
    rid = lax.broadcasted_iota(jnp.int32, x.shape, 0)
    cid = lax.broadcasted_iota(jnp.int32, xt.shape, 1)
    p = x
    pt = xt
    s = 1
    while s < TILE:
        p = p + jnp.where(rid >= s, pltpu.roll(p, s, 0), 0.0)
        pt = pt + jnp.where(cid >= s, pltpu.roll(pt, s, 1), 0.0)
        s *= 2
    tot = p[TILE - 1:TILE, :]
    tott = pt[:, TILE - 1:TILE]
    fwd = dr == 0
    bc = jnp.where(fwd, p, tot - p + x)
    br = jnp.where(fwd, pt, tott - pt + xt)
    r2 = lax.broadcasted_iota(jnp.int32, (TILE, TILE), 0)
    c2 = lax.broadcasted_iota(jnp.int32, (TILE, TILE), 1)
    tri = (r2 - c2) * (1 - 2 * dr) >= 0
    nt_dims = (((1,), (1,)), ((), ()))
    tn_dims = (((0,), (0,)), ((), ()))
    for hh in range(HEADS):
        sl = slice(hh * dh, (hh + 1) * dh)
        q = q_ref[:, sl]
        k = k_ref[:, sl]
        v = v_ref[:, sl]
        b_col = bc[:, HEADS + hh:HEADS + hh + 1]
        li_col = x[:, hh:hh + 1]
        b_row = br[HEADS + hh:HEADS + hh + 1, :]
        li_row = xt[hh:hh + 1, :]
        b_end = tot[:, HEADS + hh:HEADS + hh + 1]
        m_prev = m_s[hh][0:1, 0:1]
        logd = jnp.where(tri, b_col - b_row + li_row, NEG)
        inter = b_col + m_prev
        m_t = jnp.maximum(inter, jnp.max(logd, axis=1, keepdims=True))
        dmat = jnp.exp(logd - m_t)
        sc = lax.dot_general(q, k, nt_dims, preferred_element_type=F32) * dmat
        w_inter = jnp.exp(inter - m_t)
        ct = ct_s[hh]
        nvec = n_s[hh]
        num = (jnp.dot(sc.astype(BF16), v, preferred_element_type=F32)
               + w_inter * jnp.dot(q, ct.astype(BF16), preferred_element_type=F32))
        den = (jnp.sum(sc, axis=1, keepdims=True)
               + w_inter * jnp.sum(q.astype(F32) * nvec, axis=1, keepdims=True))
        h = num / jnp.maximum(jnp.abs(den), jnp.exp(-m_t))
        h_ref[:, sl] = h.astype(h_ref.dtype)
        g_col = b_end - b_col + li_col
        m_new = jnp.maximum(b_end + m_prev, jnp.max(g_col, axis=0, keepdims=True))
        wj = jnp.exp(g_col - m_new)
        decay = jnp.exp(b_end + m_prev - m_new)
        vw = (v.astype(F32) * wj).astype(BF16)
        ct_s[hh] = decay * ct + lax.dot_general(k, vw, tn_dims, preferred_element_type=F32)
        n_s[hh] = decay * nvec + jnp.sum(k.astype(F32) * wj, axis=0, keepdims=True)
        m_s[hh] = jnp.broadcast_to(m_new, m_s.shape[1:])


def _mlstm_scan(q, k, px, col_v, gates, gates_t, bsz, ntb):
    t, d = q.shape
    dh = d // HEADS
    gh = 2 * HEADS

    def chunk(b, dr, i):
        c = jnp.where(dr == 0, i, jnp.where(i == 0, 0, ntb - i))
        return b * ntb + c

    return pl.pallas_call(
        _scan_kernel,
        grid=(bsz, 2, ntb),
        in_specs=[
            pl.BlockSpec((TILE, d), lambda b, dr, i: (chunk(b, dr, i), 0)),
            pl.BlockSpec((TILE, d), lambda b, dr, i: (chunk(b, dr, i), 0)),
            pl.BlockSpec((TILE, d), lambda b, dr, i: (chunk(b, dr, i), col_v)),
            pl.BlockSpec((None, TILE, gh), lambda b, dr, i: (dr, chunk(b, dr, i), 0)),
            pl.BlockSpec((None, gh, TILE), lambda b, dr, i: (dr, 0, chunk(b, dr, i))),
        ],
        out_specs=pl.BlockSpec((None, TILE, d), lambda b, dr, i: (dr, chunk(b, dr, i), 0)),
        out_shape=jax.ShapeDtypeStruct((2, t, d), BF16),
        scratch_shapes=[
            pltpu.VMEM((HEADS, dh, dh), F32),
            pltpu.VMEM((HEADS, 1, dh), F32),
            pltpu.VMEM((HEADS, 8, 128), F32),
        ],
        compiler_params=_cparams(("arbitrary", "arbitrary", "arbitrary")),
        name="mlstm_scan",
    )(q, k, px, gates, gates_t)


def _conv_kernel(u_ref, dw_ref, db_ref, lg_ref, lb_ref, wo_ref, o_ref, pad_s):
    i = pl.program_id(1)
    dc = dw_ref.shape[1]
    u = u_ref[...].astype(F32)
    a = u[:, :dc] * _sigmoid(u[:, dc:])
    zeros = jnp.zeros((CONV_HALO, dc), F32)

    def depthwise(seg_len, n_seg):
        stride = seg_len + 2 * CONV_HALO
        for s in range(n_seg):
            base = s * stride
            pad_s[base:base + CONV_HALO, :] = zeros
            pad_s[base + CONV_HALO:base + CONV_HALO + seg_len, :] = a[s * seg_len:(s + 1) * seg_len, :]
            pad_s[base + CONV_HALO + seg_len:base + stride, :] = zeros
        outs = []
        for s in range(n_seg):
            base = s * stride
            acc = jnp.zeros((seg_len, dc), F32)
            for j in range(CONV_WIDTH):
                off = base + CONV_HALO - CONV_WIDTH // 2 + j
                acc = acc + dw_ref[j:j + 1, :] * pad_s[off:off + seg_len, :]
            outs.append(acc)
        return outs[0] if n_seg == 1 else jnp.concatenate(outs, axis=0)

    def finish(acc):
        acc = acc + db_ref[...]
        mu = jnp.mean(acc, axis=-1, keepdims=True)
        var = jnp.mean(jnp.square(acc - mu), axis=-1, keepdims=True)
        y = (acc - mu) * lax.rsqrt(var + LN_EPS) * lg_ref[...] + lb_ref[...]
        y = y * _sigmoid(y)
        o_ref[...] = jnp.dot(y.astype(BF16), wo_ref[...], preferred_element_type=F32).astype(o_ref.dtype)

    @pl.when(i == 0)
    def _():
        finish(depthwise(TILE, 1))

    @pl.when(i > 0)
    def _():
        finish(depthwise(GRID_W, TILE // GRID_W))


def _conv_module(px, col, dw_w, dw_b, ln_g, ln_b, w_o_bf, bsz, ntb):
    t = px.shape[0]
    dc, d = w_o_bf.shape
    return pl.pallas_call(
        _conv_kernel,
        grid=(bsz, ntb),
        in_specs=[
            pl.BlockSpec((TILE, 2 * dc), lambda b, i: (b * ntb + i, col)),
            pl.BlockSpec((CONV_WIDTH, dc), lambda b, i: (0, 0)),
            pl.BlockSpec((1, dc), lambda b, i: (0, 0)),
            pl.BlockSpec((1, dc), lambda b, i: (0, 0)),
            pl.BlockSpec((1, dc), lambda b, i: (0, 0)),
            pl.BlockSpec((dc, d), lambda b, i: (0, 0)),
        ],
        out_specs=pl.BlockSpec((TILE, d), lambda b, i: (b * ntb + i, 0)),
        out_shape=jax.ShapeDtypeStruct((t, d), BF16),
        scratch_shapes=[pltpu.VMEM(((TILE // GRID_W) * (GRID_W + 2 * CONV_HALO), dc), F32)],
        compiler_params=_cparams(("arbitrary", "arbitrary")),
        name="conv_module",
    )(px, dw_w, dw_b, ln_g, ln_b, w_o_bf)


def _pool_kernel(ctx_len, u_ref, pw_ref, ps_ref, wo_ref, o_ref, d_s):
    nt, dp = u_ref.shape
    pg = dp // len(POOL_WINDOWS)
    rows = (nt - ctx_len) // GRID_W
    uf = u_ref[...].astype(F32)
    tq = lax.broadcasted_iota(jnp.int32, (ctx_len, ctx_len), 0)
    tk = lax.broadcasted_iota(jnp.int32, (ctx_len, ctx_len), 1)
    for g, w in enumerate(POOL_WINDOWS):
        cs = slice(g * pg, (g + 1) * pg)
        lo = jnp.maximum(tq - w // 2, 0)
        hi = jnp.minimum(tq + w // 2 - 1, ctx_len - 1)
        band = jnp.where(tk >= lo, jnp.where(tk <= hi, 1.0, 0.0), 0.0).astype(BF16)
        ssum = jnp.dot(band, u_ref[0:ctx_len, cs], preferred_element_type=F32)
        cnt = (hi - lo + 1)[:, 0:1].astype(F32)
        dd = ssum / cnt - uf[0:ctx_len, cs]
        y = jnp.dot(dd.astype(BF16), pw_ref[g], preferred_element_type=F32)
        d_s[0:ctx_len, cs] = (y * ps_ref[:, cs]).astype(BF16)
    ridx = lax.broadcasted_iota(jnp.int32, (rows, GRID_W, pg), 0)
    for g, w in enumerate(POOL_WINDOWS):
        cs = slice(g * pg, (g + 1) * pg)
        xg = uf[ctx_len:, cs].reshape(rows, GRID_W, pg)
        acc = xg
        for dlt in range(-(w // 2), w // 2):
            if dlt == 0:
                continue
            z = jnp.zeros((abs(dlt), GRID_W, pg), F32)
            if dlt > 0:
                acc = acc + jnp.concatenate([xg[dlt:], z], axis=0)
            else:
                acc = acc + jnp.concatenate([z, xg[:dlt]], axis=0)
        cnt = (jnp.minimum(ridx + w // 2 - 1, rows - 1) - jnp.maximum(ridx - w // 2, 0) + 1).astype(F32)
        dd = (acc / cnt - xg).reshape(rows * GRID_W, pg)
        y = jnp.dot(dd.astype(BF16), pw_ref[g], preferred_element_type=F32)
        d_s[ctx_len:, cs] = (y * ps_ref[:, cs]).astype(BF16)
    o_ref[...] = jnp.dot(d_s[...], wo_ref[...], preferred_element_type=F32).astype(o_ref.dtype)


def _pool_module(px, col, pool_w_bf, pool_scale, w_o_bf, bsz, nt, ctx_len):
    t = px.shape[0]
    dp, d = w_o_bf.shape
    ng = len(POOL_WINDOWS)
    return pl.pallas_call(
        functools.partial(_pool_kernel, ctx_len),
        grid=(bsz,),
        in_specs=[
            pl.BlockSpec((nt, dp), lambda b: (b, col)),
            pl.BlockSpec((ng, dp // ng, dp // ng), lambda b: (0, 0, 0)),
            pl.BlockSpec((1, dp), lambda b: (0, 0)),
            pl.BlockSpec((dp, d), lambda b: (0, 0)),
        ],
        out_specs=pl.BlockSpec((nt, d), lambda b: (b, 0)),
        out_shape=jax.ShapeDtypeStruct((t, d), BF16),
        scratch_shapes=[pltpu.VMEM((nt, dp), BF16)],
        compiler_params=_cparams(("arbitrary",)),
        name="pool_module",
    )(px, pool_w_bf, pool_scale, w_o_bf)


def _mix_out_kernel(x_ref, gate_ref, z_ref, a_ref, h_ref, yc_ref, yp_ref, mod_ref, ng_ref, skip_ref, wm_ref,
                    wo_ref, g2_ref, rw_ref, rb_ref, xo_ref, hx_ref, lg_ref):
    d = x_ref.shape[1]
    dh = d // HEADS
    hsum = h_ref[0].astype(F32) + h_ref[1].astype(F32)
    parts = []
    for hh in range(HEADS):
        hb = hsum[:, hh * dh:(hh + 1) * dh]
        mu = jnp.mean(hb, axis=-1, keepdims=True)
        var = jnp.mean(jnp.square(hb - mu), axis=-1, keepdims=True)
        parts.append((hb - mu) * lax.rsqrt(var + LN_EPS))
    hn = jnp.concatenate(parts, axis=1)
    z = z_ref[...].astype(F32)
    tm = (hn * ng_ref[...] + skip_ref[...] * a_ref[...].astype(F32)) * (z * _sigmoid(z))
    y_m = jnp.dot(tm.astype(BF16), wm_ref[...], preferred_element_type=F32)
    merged = (_sigmoid(gate_ref[:, 0:d].astype(F32)) * yc_ref[...].astype(F32)
              + _sigmoid(gate_ref[:, d:2 * d].astype(F32)) * y_m
              + _sigmoid(gate_ref[:, 2 * d:3 * d].astype(F32)) * yp_ref[...].astype(F32))
    y = jnp.dot(merged.astype(BF16), wo_ref[...], preferred_element_type=F32)
    x = x_ref[...] + mod_ref[2:3, :] * y
    xo_ref[...] = x
    ms = jnp.mean(x * x, axis=-1, keepdims=True)
    hx = x * lax.rsqrt(ms + RMS_EPS) * g2_ref[...]
    hx = hx * (1.0 + mod_ref[4:5, :]) + mod_ref[3:4, :]
    hx_ref[...] = hx
    lg_ref[...] = jnp.dot(hx, rw_ref[...], precision=HIGHEST, preferred_element_type=F32) + rb_ref[...]


def _mix_out(xt, px, col_z, a, h2, y_conv, y_pool, modtab, norm_g, skip, w_m_bf, w_out_bf, g2, router_w, router_b,
             ntb, mod_base):
    t, d = xt.shape
    ne = router_w.shape[1]
    row = _mod_row(ntb, mod_base)
    vec = lambda: pl.BlockSpec((1, d), lambda i: (0, 0))
    return pl.pallas_call(
        _mix_out_kernel,
        grid=(t // TILE,),
        in_specs=[
            pl.BlockSpec((TILE, d), lambda i: (i, 0)),
            pl.BlockSpec((TILE, 3 * d), lambda i: (i, 0)),
            pl.BlockSpec((TILE, d), lambda i: (i, col_z)),
            pl.BlockSpec((TILE, d), lambda i: (i, 0)),
            pl.BlockSpec((2, TILE, d), lambda i: (0, i, 0)),
            pl.BlockSpec((TILE, d), lambda i: (i, 0)),
            pl.BlockSpec((TILE, d), lambda i: (i, 0)),
            pl.BlockSpec((None, 6, d), lambda i: (row(i), 0, 0)),
            vec(), vec(),
            pl.BlockSpec((d, d), lambda i: (0, 0)),
            pl.BlockSpec((d, d), lambda i: (0, 0)),
            vec(),
            pl.BlockSpec((d, ne), lambda i: (0, 0)),
            pl.BlockSpec((1, ne), lambda i: (0, 0)),
        ],
        out_specs=[
            pl.BlockSpec((TILE, d), lambda i: (i, 0)),
            pl.BlockSpec((TILE, d), lambda i: (i, 0)),
            pl.BlockSpec((TILE, ne), lambda i: (i, 0)),
        ],
        out_shape=[
            jax.ShapeDtypeStruct((t, d), F32),
            jax.ShapeDtypeStruct((t, d), F32),
            jax.ShapeDtypeStruct((t, ne), F32),
        ],
        compiler_params=_cparams(("arbitrary",)),
        name="mix_out",
    )(xt, px, px, a, h2, y_conv, y_pool, modtab, norm_g, skip, w_m_bf, w_out_bf, g2, router_w, router_b)


def _expert_kernel(be_ref, nu_ref, tok_ref, tokn_ref, hx_hbm, wgu_ref, bgu_ref, wdn_ref, bdn_ref, y_ref,
                   xbuf, sem, wgu_s, wdn_s):
    i = pl.program_id(0)
    n_used = nu_ref[0]
    slot = i % 2
    bm = xbuf.shape[1]
    dff = wdn_s.shape[0]

    def row_copy(tok_smem, r, sl):
        t = tok_smem[0, 0, r]
        return pltpu.make_async_copy(hx_hbm.at[pl.ds(t, 1), :], xbuf.at[sl, pl.ds(r, 1), :], sem.at[sl])

    def issue(tok_smem, sl):
        def body(r, carry):
            row_copy(tok_smem, r, sl).start()
            return carry
        lax.fori_loop(0, bm, body, 0)

    @pl.when(i == 0)
    def _():
        issue(tok_ref, 0)

    @pl.when(i + 1 < n_used)
    def _():
        issue(tokn_ref, 1 - slot)

    e_now = be_ref[i]
    e_prev = be_ref[jnp.maximum(i - 1, 0)]

    @pl.when(jnp.logical_or(i == 0, e_now != e_prev))
    def _():
        wgu_s[...] = wgu_ref[...].astype(BF16)
        wdn_s[...] = wdn_ref[...].astype(BF16)

    @pl.when(i < n_used)
    def _():
        def wait_body(r, carry):
            row_copy(tok_ref, r, slot).wait()
            return carry
        lax.fori_loop(0, bm, wait_body, 0)
        x = xbuf[slot].astype(BF16)
        gu = jnp.dot(x, wgu_s[...], preferred_element_type=F32) + bgu_ref[...]
        g = jnp.minimum(gu[:, :dff], SWIGLU_LIMIT)
        u = jnp.clip(gu[:, dff:], -SWIGLU_LIMIT, SWIGLU_LIMIT)
        act = (u + 1.0) * (g * _sigmoid(SWIGLU_ALPHA * g))
        y_ref[...] = jnp.dot(act.astype(BF16), wdn_s[...], preferred_element_type=F32) + bdn_ref[...]

    @pl.when(i >= n_used)
    def _():
        y_ref[...] = jnp.zeros(y_ref.shape, y_ref.dtype)


def _experts(hx, row_tok, block_e, n_used, w_gu, b_gu, w_dn, b_dn):
    t, d = hx.shape
    ne, _, dff2 = w_gu.shape
    dff = dff2 // 2
    nb = block_e.shape[0]
    bm = MOE_BM
    tok3 = row_tok.reshape(nb, 1, bm)
    grid_spec = pltpu.PrefetchScalarGridSpec(
        num_scalar_prefetch=2,
        grid=(nb,),
        in_specs=[
            pl.BlockSpec((1, 1, bm), lambda i, be, nu: (i, 0, 0), memory_space=pltpu.SMEM),
            pl.BlockSpec((1, 1, bm), lambda i, be, nu: (jnp.minimum(i + 1, nb - 1), 0, 0),
                         memory_space=pltpu.SMEM),
            pl.BlockSpec(memory_space=pl.ANY),
            pl.BlockSpec((None, d, dff2), lambda i, be, nu: (be[i], 0, 0)),
            pl.BlockSpec((None, 1, dff2), lambda i, be, nu: (be[i], 0, 0)),
            pl.BlockSpec((None, dff, d), lambda i, be, nu: (be[i], 0, 0)),
            pl.BlockSpec((None, 1, d), lambda i, be, nu: (be[i], 0, 0)),
        ],
        out_specs=pl.BlockSpec((bm, d), lambda i, be, nu: (i, 0)),
        scratch_shapes=[
            pltpu.VMEM((2, bm, d), F32),
            pltpu.SemaphoreType.DMA((2,)),
            pltpu.VMEM((d, dff2), BF16),
            pltpu.VMEM((dff, d), BF16),
        ],
    )
    return pl.pallas_call(
        _expert_kernel,
        grid_spec=grid_spec,
        out_shape=jax.ShapeDtypeStruct((nb * bm, d), F32),
        compiler_params=_cparams(("arbitrary",)),
        name="moe_experts",
    )(block_e, n_used, tok3, tok3, hx, w_gu, b_gu.reshape(ne, 1, dff2), w_dn, b_dn.reshape(ne, 1, d))


def _combine_kernel(dst_ref, dstn_ref, ys_hbm, gw_ref, x_ref, mod_ref, o_ref, buf, sem):
    i = pl.program_id(0)
    nsteps = pl.num_programs(0)
    slot = i % 2
    n_copy = TOP_K * TILE

    def row_copy(dst_smem, e, sl):
        src = dst_smem[0, 0, e]
        kk = e // TILE
        r = e % TILE
        return pltpu.make_async_copy(ys_hbm.at[pl.ds(src, 1), :], buf.at[sl, kk, pl.ds(r, 1), :], sem.at[sl])

    def issue(dst_smem, sl):
        def body(e, carry):
            row_copy(dst_smem, e, sl).start()
            return carry
        lax.fori_loop(0, n_copy, body, 0)

    @pl.when(i == 0)
    def _():
        issue(dst_ref, 0)

    @pl.when(i + 1 < nsteps)
    def _():
        issue(dstn_ref, 1 - slot)

    def wait_body(e, carry):
        row_copy(dst_ref, e, slot).wait()
        return carry
    lax.fori_loop(0, n_copy, wait_body, 0)

    gw = gw_ref[...]
    y = gw[:, 0:1] * buf[slot, 0]
    for kk in range(1, TOP_K):
        y = y + gw[:, kk:kk + 1] * buf[slot, kk]
    o_ref[...] = x_ref[...] + mod_ref[5:6, :] * y


def _combine(ys, dest_tiles, gate_w, xt, modtab, ntb, mod_base):
    t, d = xt.shape
    nt = t // TILE
    row = _mod_row(ntb, mod_base)
    return pl.pallas_call(
        _combine_kernel,
        grid=(nt,),
        in_specs=[
            pl.BlockSpec((1, 1, TOP_K * TILE), lambda i: (i, 0, 0), memory_space=pltpu.SMEM),
            pl.BlockSpec((1, 1, TOP_K * TILE), lambda i: (jnp.minimum(i + 1, nt - 1), 0, 0),
                         memory_space=pltpu.SMEM),
            pl.BlockSpec(memory_space=pl.ANY),
            pl.BlockSpec((TILE, TOP_K), lambda i: (i, 0)),
            pl.BlockSpec((TILE, d), lambda i: (i, 0)),
            pl.BlockSpec((None, 6, d), lambda i: (row(i), 0, 0)),
        ],
        out_specs=pl.BlockSpec((TILE, d), lambda i: (i, 0)),
        out_shape=jax.ShapeDtypeStruct((t, d), F32),
        scratch_shapes=[
            pltpu.VMEM((2, TOP_K, TILE, d), F32),
            pltpu.SemaphoreType.DMA((2,)),
        ],
        compiler_params=_cparams(("arbitrary",)),
        name="moe_combine",
    )(dest_tiles, dest_tiles, ys, gate_w, xt, modtab)


def _final_kernel(x_ref, g_ref, o_ref):
    x = x_ref[...]
    ms = jnp.mean(x * x, axis=-1, keepdims=True)
    o_ref[...] = x * lax.rsqrt(ms + RMS_EPS) * g_ref[...]


def _final_norm(xt, final_g, bsz, ntb):
    t, d = xt.shape
    nlat = ntb - 1
    return pl.pallas_call(
        _final_kernel,
        grid=(bsz, nlat),
        in_specs=[
            pl.BlockSpec((TILE, d), lambda b, j: (b * ntb + 1 + j, 0)),
            pl.BlockSpec((1, d), lambda b, j: (0, 0)),
        ],
        out_specs=pl.BlockSpec((TILE, d), lambda b, j: (b * nlat + j, 0)),
        out_shape=jax.ShapeDtypeStruct((bsz * nlat * TILE, d), F32),
        compiler_params=_cparams(("arbitrary", "arbitrary")),
        name="final_norm",
    )(xt, final_g)


def _routing(logits, n_tiles):
    t = logits.shape[0]
    top_v, top_e = lax.top_k(logits, TOP_K)
    gate = jax.nn.softmax(top_v, axis=-1)
    onehot = (top_e[:, :, None] == jnp.arange(N_EXPERTS, dtype=top_e.dtype)[None, None, :]).astype(jnp.int32)
    per_tok = jnp.sum(onehot, axis=1)
    before = jnp.cumsum(per_tok, axis=0) - per_tok
    counts = jnp.sum(per_tok, axis=0)
    padded = (counts + MOE_BM - 1) // MOE_BM * MOE_BM
    pad_end = jnp.cumsum(padded)
    pad_start = pad_end - padded
    dest = jnp.take_along_axis(pad_start[None, :] + before, top_e, axis=1).astype(jnp.int32)
    n_blocks = (t * TOP_K) // MOE_BM + N_EXPERTS
    n_rows = n_blocks * MOE_BM
    tok_ids = jnp.broadcast_to(jnp.arange(t, dtype=jnp.int32)[:, None], (t, TOP_K))
    row_tok = jnp.zeros((n_rows,), jnp.int32).at[dest.reshape(-1)].set(tok_ids.reshape(-1))
    block_e = jnp.minimum(jnp.searchsorted(pad_end, jnp.arange(n_blocks, dtype=jnp.int32) * MOE_BM, side='right'),
                          N_EXPERTS - 1).astype(jnp.int32)
    n_used = (pad_end[-1] // MOE_BM).astype(jnp.int32).reshape(1)
    dest_tiles = dest.reshape(n_tiles, TILE, TOP_K).transpose(0, 2, 1).reshape(n_tiles, 1, TOP_K * TILE)
    return gate, row_tok, block_e, n_used, dest_tiles


def _block_diag(w, heads):
    nb, c, _ = w.shape
    per = nb // heads
    w = w.reshape(heads, per, c, c)
    eye = jnp.eye(per, dtype=w.dtype)
    full = eye[None, :, None, :, None] * w[:, :, :, None, :]
    return full.reshape(heads, per * c, per * c)


def kernel(x, c, ctx, c_ctx, w_ada, b_ada, norm1_g, norm2_g, w_in, conv_dw_w, conv_dw_b, conv_ln_g, conv_ln_b, w_conv_out, m_sc_w, m_sc_b, m_wq, m_wk, m_w_if, m_b_if, m_norm_g, m_skip, w_m_out, pool_w, pool_scale, w_pool_out, w_out, router_w, router_b, w_gu, b_gu, w_dn, b_dn, final_g):
    bsz, seq, d = x.shape
    ctx_len = ctx.shape[1]
    depth = w_ada.shape[0]
    assert ctx_len == TILE and seq % TILE == 0 and d % (128 * HEADS) == 0
    nt = ctx_len + seq
    ntb = nt // TILE
    t = bsz * nt
    n_tiles = t // TILE
    d_conv = w_conv_out.shape[1]
    d_pool = w_pool_out.shape[1]
    d_m = w_m_out.shape[1]
    assert d_m == d and 2 * d_conv == d and 2 * d_pool == d
    c_conv = 3 * d
    c_pool = c_conv + 2 * d_conv
    c_mz = c_pool + d_pool
    c_mqk = c_mz + d_m
    c_mv = c_mqk + d_m
    d_in = c_mv + d_m
    perm = jnp.concatenate([jnp.arange(0, c_pool), jnp.arange(c_mz, d_in), jnp.arange(c_pool, c_mz)])
    col_conv, col_z, col_u, col_v = 3, 4, 5, 6
    col_pool = (7 * d) // d_pool

    xt = jnp.concatenate([ctx, x], axis=1).reshape(t, d)

    mp = -(-(bsz + 1) // 8) * 8
    c_all = jnp.zeros((mp, d), F32).at[:bsz].set(c).at[bsz].set(c_ctx)
    mod = _ada(c_all, w_ada, b_ada)
    mod_x = mod[:, :bsz].reshape(depth, bsz, 1, 6, d)
    mod_c = jnp.broadcast_to(mod[:, bsz].reshape(depth, 1, 1, 6, d), (depth, bsz, 1, 6, d))
    modtab = jnp.concatenate([mod_c, mod_x], axis=2).reshape(depth * bsz * 2, 6, d)

    gh = 2 * HEADS
    for l in range(depth):
        mod_base = l * bsz * 2
        w_in_bf = jnp.take(w_in[l], perm, axis=1).astype(BF16)
        px = _in_proj(xt, norm1_g[l].reshape(1, d), modtab, w_in_bf, ntb, mod_base)

        wq_bd = _block_diag(m_wq[l], HEADS).astype(BF16)
        wk_bd = _block_diag(m_wk[l], HEADS).astype(BF16)
        wif = jnp.concatenate([m_w_if[l, 0], m_w_if[l, 1]], axis=1)
        bif = jnp.concatenate([m_b_if[l, 0], m_b_if[l, 1]], axis=0)
        a, q, k, gates, gates_t = _mlstm_pre(
            px, col_u, col_v, m_sc_w[l], m_sc_b[l].reshape(1, d), wq_bd, wk_bd,
            wif.astype(BF16), wif.T.astype(BF16), bif.reshape(1, 2 * gh), bif.reshape(2 * gh, 1), bsz, ntb)
        h2 = _mlstm_scan(q, k, px, col_v, gates, gates_t, bsz, ntb)

        y_conv = _conv_module(px, col_conv, conv_dw_w[l], conv_dw_b[l].reshape(1, d_conv),
                              conv_ln_g[l].reshape(1, d_conv), conv_ln_b[l].reshape(1, d_conv),
                              w_conv_out[l].astype(BF16), bsz, ntb)
        y_pool = _pool_module(px, col_pool, pool_w[l].astype(BF16), pool_scale[l].reshape(1, d_pool),
                              w_pool_out[l].astype(BF16), bsz, nt, ctx_len)

        xt, hx, logits = _mix_out(
            xt, px, col_z, a, h2, y_conv, y_pool, modtab, m_norm_g[l].reshape(1, d), m_skip[l].reshape(1, d),
            w_m_out[l].astype(BF16), w_out[l].astype(BF16), norm2_g[l].reshape(1, d), router_w[l],
            router_b[l].reshape(1, N_EXPERTS), ntb, mod_base)

        gate_w, row_tok, block_e, n_used, dest_tiles = _routing(logits, n_tiles)
        ys = _experts(hx, row_tok, block_e, n_used, w_gu[l], b_gu[l], w_dn[l], b_dn[l])
        xt = _combine(ys, dest_tiles, gate_w, xt, modtab, ntb, mod_base)

    out = _final_norm(xt, final_g.reshape(1, d), bsz, ntb)
    return out.reshape(bsz, seq, d)
```

```python
import functools

import jax
import jax.numpy as jnp
from jax import lax
from jax.experimental import pallas as pl
from jax.experimental.pallas import tpu as pltpu

F32 = jnp.float32
BF16 = jnp.bfloat16
HIGHEST = lax.Precision.HIGHEST

TILE = 256
GRID_W = 64
CONV_WIDTH = 31
SUBLANES = 8
CONV_HALO = 16
SHORT_CONV = 4
HEADS = 4
POOL_WINDOWS = (2, 4, 8, 16)
N_EXPERTS = 32
TOP_K = 4
MOE_BM = 512
MOE_CHUNK = 256
SWIGLU_LIMIT = 7.0
SWIGLU_ALPHA = 1.702
RMS_EPS = 1e-6
LN_EPS = 1e-5
NEG = -1e30
VMEM_LIMIT = 56 * 1024 * 1024


def _cparams(sem, no_bounds_checks=False):
    return pltpu.CompilerParams(dimension_semantics=sem, vmem_limit_bytes=VMEM_LIMIT,
                                disable_bounds_checks=no_bounds_checks)


def _sigmoid(x):
    return jax.nn.sigmoid(x)


def _log_sigmoid(x):
    return jnp.minimum(x, 0.0) - jnp.log1p(jnp.exp(-jnp.abs(x)))


def _ada_kernel(c_ref, w_ref, b_ref, o_ref):
    s = c_ref[...]
    s = s * _sigmoid(s)
    o_ref[...] = jnp.dot(s, w_ref[...], precision=HIGHEST, preferred_element_type=F32) + b_ref[...]


def _ada(c_all, w_ada, b_ada):
    depth, d, n = w_ada.shape
    mp = c_all.shape[0]
    tn = 1536
    return pl.pallas_call(
        _ada_kernel,
        grid=(depth, n // tn),
        in_specs=[
            pl.BlockSpec((mp, d), lambda l, j: (0, 0)),
            pl.BlockSpec((None, d, tn), lambda l, j: (l, 0, j)),
            pl.BlockSpec((None, 1, tn), lambda l, j: (l, 0, j)),
        ],
        out_specs=pl.BlockSpec((None, mp, tn), lambda l, j: (l, 0, j)),
        out_shape=jax.ShapeDtypeStruct((depth, mp, n), F32),
        compiler_params=_cparams(("arbitrary", "arbitrary")),
        name="ada",
    )(c_all, w_ada, b_ada.reshape(depth, 1, n))


def _in_kernel(x_ref, g_ref, mod_ref, w_ref, o_ref):
    x = x_ref[...]
    ms = jnp.mean(x * x, axis=-1, keepdims=True)
    y = x * lax.rsqrt(ms + RMS_EPS) * g_ref[...]
    h = y * (1.0 + mod_ref[1:2, :]) + mod_ref[0:1, :]
    o_ref[...] = jnp.dot(h.astype(BF16), w_ref[...], preferred_element_type=F32).astype(BF16)


def _mod_row(ntb, base):
    def f(i):
        return base + 2 * (i // ntb) + jnp.minimum(i % ntb, 1)
    return f


def _in_proj(xt, g1, modtab, w_in_bf, ntb, mod_base):
    t, d = xt.shape
    n = w_in_bf.shape[1]
    row = _mod_row(ntb, mod_base)
    return pl.pallas_call(
        _in_kernel,
        grid=(t // TILE,),
        in_specs=[
            pl.BlockSpec((TILE, d), lambda i: (i, 0)),
            pl.BlockSpec((1, d), lambda i: (0, 0)),
            pl.BlockSpec((None, 6, d), lambda i: (row(i), 0, 0)),
            pl.BlockSpec((d, n), lambda i: (0, 0), pipeline_mode=pl.Buffered(1)),
        ],
        out_specs=pl.BlockSpec((TILE, n), lambda i: (i, 0)),
        out_shape=jax.ShapeDtypeStruct((t, n), BF16),
        compiler_params=_cparams(("arbitrary",)),
        name="in_proj",
    )(xt, g1, modtab, w_in_bf)


def _mpre_kernel(ntb, u_ref, up_ref, un_ref, v_ref, scw_ref, scb_ref, wq_ref, wk_ref, wif_ref, wift_ref,
                 bif_ref, bift_ref, a_ref, q_ref, k_ref, g_ref, gt_ref):
    i = pl.program_id(1)
    d = u_ref.shape[1]
    dh = d // HEADS
    u = u_ref[...].astype(F32)
    up = jnp.where(i >= 2, up_ref[...].astype(F32)[8:16, :], 0.0)
    un = jnp.where(jnp.logical_and(i >= 1, i <= ntb - 2), un_ref[...].astype(F32)[0:8, :], 0.0)
    ext = jnp.concatenate([up, u, un], axis=0)
    n_ext = TILE + 16
    conv = scb_ref[...]
    for j in range(SHORT_CONV):
        sh = (SHORT_CONV // 2 - j) % n_ext
        shifted = ext if sh == 0 else pltpu.roll(ext, sh, 0)
        conv = conv + scw_ref[j:j + 1, :] * shifted[8:8 + TILE, :]
    a = conv * _sigmoid(conv)
    ab = a.astype(BF16)
    a_ref[...] = ab
    qs, ks = [], []
    for hb in range(HEADS):
        blk = ab[:, hb * dh:(hb + 1) * dh]
        qs.append(jnp.dot(blk, wq_ref[hb], preferred_element_type=F32))
        ks.append(jnp.dot(blk, wk_ref[hb], preferred_element_type=F32))
    q = jnp.concatenate(qs, axis=1)
    k = jnp.concatenate(ks, axis=1)
    qb = q.astype(BF16)
    kb = k.astype(BF16)
    vb = v_ref[...]
    q_ref[...] = (q * (dh ** -0.5)).astype(BF16)
    k_ref[...] = kb
    pre = bif_ref[...]
    pret = bift_ref[...]
    nt_dims = (((1,), (1,)), ((), ()))
    for part, f in enumerate((qb, kb, vb)):
        pre = pre + jnp.dot(f, wif_ref[part * d:(part + 1) * d, :], preferred_element_type=F32)
        pret = pret + lax.dot_general(wift_ref[:, part * d:(part + 1) * d], f, nt_dims,
                                      preferred_element_type=F32)
    col = lax.broadcasted_iota(jnp.int32, pre.shape, 1)
    pre = jnp.where(col % (2 * HEADS) >= HEADS, _log_sigmoid(pre), pre)
    rowi = lax.broadcasted_iota(jnp.int32, pret.shape, 0)
    pret = jnp.where(rowi % (2 * HEADS) >= HEADS, _log_sigmoid(pret), pret)
    g_ref[0] = pre[:, 0:2 * HEADS]
    g_ref[1] = pre[:, 2 * HEADS:4 * HEADS]
    gt_ref[0] = pret[0:2 * HEADS, :]
    gt_ref[1] = pret[2 * HEADS:4 * HEADS, :]


def _mlstm_pre(px, col_u, col_v, sc_w, sc_b, wq_bd, wk_bd, wif, wift, bif, bift, bsz, ntb):
    t = px.shape[0]
    d = sc_w.shape[1]
    hrows = 16
    nh = t // hrows
    per = TILE // hrows

    def tile(b, i):
        return b * ntb + i

    gh = 2 * HEADS
    return pl.pallas_call(
        functools.partial(_mpre_kernel, ntb),
        grid=(bsz, ntb),
        in_specs=[
            pl.BlockSpec((TILE, d), lambda b, i: (tile(b, i), col_u)),
            pl.BlockSpec((hrows, d), lambda b, i: (jnp.maximum(tile(b, i) * per - 1, 0), col_u)),
            pl.BlockSpec((hrows, d), lambda b, i: (jnp.minimum((tile(b, i) + 1) * per, nh - 1), col_u)),
            pl.BlockSpec((TILE, d), lambda b, i: (tile(b, i), col_v)),
            pl.BlockSpec((SHORT_CONV, d), lambda b, i: (0, 0)),
            pl.BlockSpec((1, d), lambda b, i: (0, 0)),
            pl.BlockSpec((HEADS, d // HEADS, d // HEADS), lambda b, i: (0, 0, 0)),
            pl.BlockSpec((HEADS, d // HEADS, d // HEADS), lambda b, i: (0, 0, 0)),
            pl.BlockSpec((3 * d, 2 * gh), lambda b, i: (0, 0)),
            pl.BlockSpec((2 * gh, 3 * d), lambda b, i: (0, 0)),
            pl.BlockSpec((1, 2 * gh), lambda b, i: (0, 0)),
            pl.BlockSpec((2 * gh, 1), lambda b, i: (0, 0)),
        ],
        out_specs=[
            pl.BlockSpec((TILE, d), lambda b, i: (tile(b, i), 0)),
            pl.BlockSpec((TILE, d), lambda b, i: (tile(b, i), 0)),
            pl.BlockSpec((TILE, d), lambda b, i: (tile(b, i), 0)),
            pl.BlockSpec((2, TILE, gh), lambda b, i: (0, tile(b, i), 0)),
            pl.BlockSpec((2, gh, TILE), lambda b, i: (0, 0, tile(b, i))),
        ],
        out_shape=[
            jax.ShapeDtypeStruct((t, d), BF16),
            jax.ShapeDtypeStruct((t, d), BF16),
            jax.ShapeDtypeStruct((t, d), BF16),
            jax.ShapeDtypeStruct((2, t, gh), F32),
            jax.ShapeDtypeStruct((2, gh, t), F32),
        ],
        compiler_params=_cparams(("arbitrary", "arbitrary")),
        name="mlstm_pre",
    )(px, px, px, px, sc_w, sc_b, wq_bd, wk_bd, wif, wift, bif, bift)


AUG = 128


def _scan_kernel(q_ref, k_ref, v_ref, g_ref, gt_ref, h_ref, cta_s, m_s):
    dr = pl.program_id(1)
    i = pl.program_id(2)
    dh = q_ref.shape[1] // HEADS

    @pl.when(i == 0)
    def _():
        cta_s[...] = jnp.zeros(cta_s.shape, F32)
        m_s[...] = jnp.zeros(m_s.shape, F32)

    x = g_ref[...]
    xt = gt_ref[...]
    rid = lax.broadcasted_iota(jnp.int32, x.shape, 0)
    cid = lax.broadcasted_iota(jnp.int32, xt.shape, 1)
    p = x
    pt = xt
    s = 1
    while s < TILE:
        p = p + jnp.where(rid >= s, pltpu.roll(p, s, 0), 0.0)
        pt = pt + jnp.where(cid >= s, pltpu.roll(pt, s, 1), 0.0)
        s *= 2
    tot = p[TILE - 1:TILE, :]
    tott = pt[:, TILE - 1:TILE]
    fwd = dr == 0
    bc = jnp.where(fwd, p, tot - p + x)
    br = jnp.where(fwd, pt, tott - pt + xt)
    amat = x[:, 0:HEADS] - bc[:, HEADS:2 * HEADS]
    rida = lax.broadcasted_iota(jnp.int32, amat.shape, 0)
    cpre = amat
    csuf = amat
    s = 1
    while s < TILE:
        cpre = jnp.maximum(cpre, jnp.where(rida >= s, pltpu.roll(cpre, s, 0), NEG))
        csuf = jnp.maximum(csuf, jnp.where(rida < TILE - s, pltpu.roll(csuf, TILE - s, 0), NEG))
        s *= 2
    cmax = jnp.where(fwd, cpre, csuf)
    r2 = lax.broadcasted_iota(jnp.int32, (TILE, TILE), 0)
    c2 = lax.broadcasted_iota(jnp.int32, (TILE, TILE), 1)
    tri = (r2 - c2) * (1 - 2 * dr) >= 0
    lane = lax.broadcasted_iota(jnp.int32, (TILE, AUG), 1)
    one_col = jnp.where(lane == 0, 1.0, 0.0)
    nt_dims = (((1,), (1,)), ((), ()))
    tn_dims = (((0,), (0,)), ((), ()))
    for hh in range(HEADS):
        sl = slice(hh * dh, (hh + 1) * dh)
        q = q_ref[:, sl]
        k = k_ref[:, sl]
        v = v_ref[:, sl]
        b_col = bc[:, HEADS + hh:HEADS + hh + 1]
        li_col = x[:, hh:hh + 1]
        row_vec = xt[hh:hh + 1, :] - br[HEADS + hh:HEADS + hh + 1, :]
        b_end = tot[:, HEADS + hh:HEADS + hh + 1]
        m_prev = m_s[hh][0:1, 0:1]
        col_vec = -jnp.maximum(m_prev, cmax[:, hh:hh + 1])
        m_t = b_col - col_vec
        dmat = jnp.exp(jnp.where(tri, row_vec + col_vec, NEG))
        sc = lax.dot_general(q, k, nt_dims, preferred_element_type=F32) * dmat
        w_inter = jnp.exp(m_prev + col_vec)
        cta = cta_s[hh]
        v_aug = jnp.concatenate([v, one_col.astype(BF16)], axis=1)
        nd = (jnp.dot(sc.astype(BF16), v_aug, preferred_element_type=F32)
              + w_inter * jnp.dot(q, cta.astype(BF16), preferred_element_type=F32))
        den = nd[:, dh:dh + 1]
        h = nd[:, 0:dh] / jnp.maximum(jnp.abs(den), jnp.exp(-m_t))
        h_ref[:, sl] = h.astype(h_ref.dtype)
        g_col = b_end - b_col + li_col
        m_new = jnp.maximum(b_end + m_prev, jnp.max(g_col, axis=0, keepdims=True))
        wj = jnp.exp(g_col - m_new)
        decay = jnp.exp(b_end + m_prev - m_new)
        vw = jnp.concatenate([(v.astype(F32) * wj).astype(BF16), (one_col * wj).astype(BF16)], axis=1)
        cta_s[hh] = decay * cta + lax.dot_general(k, vw, tn_dims, preferred_element_type=F32)
        m_s[hh] = jnp.broadcast_to(m_new, m_s.shape[1:])


def _mlstm_scan(q, k, px, col_v, gates, gates_t, bsz, ntb):
    t, d = q.shape
    dh = d // HEADS
    gh = 2 * HEADS

    def chunk(b, dr, i):
        c = jnp.where(dr == 0, i, jnp.where(i == 0, 0, ntb - i))
        return b * ntb + c

    return pl.pallas_call(
        _scan_kernel,
        grid=(bsz, 2, ntb),
        in_specs=[
            pl.BlockSpec((TILE, d), lambda b, dr, i: (chunk(b, dr, i), 0)),
            pl.BlockSpec((TILE, d), lambda b, dr, i: (chunk(b, dr, i), 0)),
            pl.BlockSpec((TILE, d), lambda b, dr, i: (chunk(b, dr, i), col_v)),
            pl.BlockSpec((None, TILE, gh), lambda b, dr, i: (dr, chunk(b, dr, i), 0)),
            pl.BlockSpec((None, gh, TILE), lambda b, dr, i: (dr, 0, chunk(b, dr, i))),
        ],
        out_specs=pl.BlockSpec((None, TILE, d), lambda b, dr, i: (dr, chunk(b, dr, i), 0)),
        out_shape=jax.ShapeDtypeStruct((2, t, d), BF16),
        scratch_shapes=[
            pltpu.VMEM((HEADS, dh, dh + AUG), F32),
            pltpu.VMEM((HEADS, 8, 128), F32),
        ],
        compiler_params=_cparams(("arbitrary", "arbitrary", "arbitrary")),
        name="mlstm_scan",
    )(q, k, px, gates, gates_t)


def _conv_kernel(u_ref, dw_ref, db_ref, lg_ref, lb_ref, wo_ref, o_ref, pad_s):
    i = pl.program_id(1)
    dc = dw_ref.shape[1]
    u = u_ref[...].astype(F32)
    a = u[:, :dc] * _sigmoid(u[:, dc:])
    zeros = jnp.zeros((CONV_HALO, dc), F32)

    def depthwise(seg_len, n_seg):
        stride = seg_len + 2 * CONV_HALO
        used = n_seg * stride
        for s in range(n_seg):
            base = s * stride
            pad_s[0, base:base + CONV_HALO, :] = zeros
            pad_s[0, base + CONV_HALO:base + CONV_HALO + seg_len, :] = a[s * seg_len:(s + 1) * seg_len, :]
            pad_s[0, base + CONV_HALO + seg_len:base + stride, :] = zeros
        for rho in range(1, SUBLANES):
            pad_s[rho, 0:used - SUBLANES, :] = pad_s[0, rho:used - SUBLANES + rho, :]
        outs = []
        for s in range(n_seg):
            base = s * stride
            acc = jnp.zeros((seg_len, dc), F32)
            for j in range(CONV_WIDTH):
                off = base + CONV_HALO - CONV_WIDTH // 2 + j
                rho = off % SUBLANES
                acc = acc + dw_ref[j:j + 1, :] * pad_s[rho, off - rho:off - rho + seg_len, :]
            outs.append(acc)
        return outs[0] if n_seg == 1 else jnp.concatenate(outs, axis=0)

    def finish(acc):
        acc = acc + db_ref[...]
        mu = jnp.mean(acc, axis=-1, keepdims=True)
        var = jnp.mean(jnp.square(acc - mu), axis=-1, keepdims=True)
        y = (acc - mu) * lax.rsqrt(var + LN_EPS) * lg_ref[...] + lb_ref[...]
        y = y * _sigmoid(y)
        o_ref[...] = jnp.dot(y.astype(BF16), wo_ref[...], preferred_element_type=F32).astype(o_ref.dtype)

    @pl.when(i == 0)
    def _():
        finish(depthwise(TILE, 1))

    @pl.when(i > 0)
    def _():
        finish(depthwise(GRID_W, TILE // GRID_W))


def _conv_module(px, col, dw_w, dw_b, ln_g, ln_b, w_o_bf, bsz, ntb):
    t = px.shape[0]
    dc, d = w_o_bf.shape
    return pl.pallas_call(
        _conv_kernel,
        grid=(bsz, ntb),
        in_specs=[
            pl.BlockSpec((TILE, 2 * dc), lambda b, i: (b * ntb + i, col)),
            pl.BlockSpec((CONV_WIDTH, dc), lambda b, i: (0, 0)),
            pl.BlockSpec((1, dc), lambda b, i: (0, 0)),
            pl.BlockSpec((1, dc), lambda b, i: (0, 0)),
            pl.BlockSpec((1, dc), lambda b, i: (0, 0)),
            pl.BlockSpec((dc, d), lambda b, i: (0, 0)),
        ],
        out_specs=pl.BlockSpec((TILE, d), lambda b, i: (b * ntb + i, 0)),
        out_shape=jax.ShapeDtypeStruct((t, d), BF16),
        scratch_shapes=[pltpu.VMEM((SUBLANES, (TILE // GRID_W) * (GRID_W + 2 * CONV_HALO), dc), F32)],
        compiler_params=_cparams(("arbitrary", "arbitrary")),
        name="conv_module",
    )(px, dw_w, dw_b, ln_g, ln_b, w_o_bf)


def _pool_kernel(ctx_len, u_ref, pw_ref, ps_ref, wo_ref, o_ref, d_s):
    nt, dp = u_ref.shape
    pg = dp // len(POOL_WINDOWS)
    rows = (nt - ctx_len) // GRID_W
    uf = u_ref[...].astype(F32)
    tq = lax.broadcasted_iota(jnp.int32, (ctx_len, ctx_len), 0)
    tk = lax.broadcasted_iota(jnp.int32, (ctx_len, ctx_len), 1)
    for g, w in enumerate(POOL_WINDOWS):
        cs = slice(g * pg, (g + 1) * pg)
        lo = jnp.maximum(tq - w // 2, 0)
        hi = jnp.minimum(tq + w // 2 - 1, ctx_len - 1)
        band = jnp.where(tk >= lo, jnp.where(tk <= hi, 1.0, 0.0), 0.0).astype(BF16)
        ssum = jnp.dot(band, u_ref[0:ctx_len, cs], preferred_element_type=F32)
        cnt = (hi - lo + 1)[:, 0:1].astype(F32)
        dd = ssum / cnt - uf[0:ctx_len, cs]
        y = jnp.dot(dd.astype(BF16), pw_ref[g], preferred_element_type=F32)
        d_s[0:ctx_len, cs] = (y * ps_ref[:, cs]).astype(BF16)
    ridx = lax.broadcasted_iota(jnp.int32, (rows, GRID_W, pg), 0)
    for g, w in enumerate(POOL_WINDOWS):
        cs = slice(g * pg, (g + 1) * pg)
        xg = uf[ctx_len:, cs].reshape(rows, GRID_W, pg)
        acc = xg
        for dlt in range(-(w // 2), w // 2):
            if dlt == 0:
                continue
            z = jnp.zeros((abs(dlt), GRID_W, pg), F32)
            if dlt > 0:
                acc = acc + jnp.concatenate([xg[dlt:], z], axis=0)
            else:
                acc = acc + jnp.concatenate([z, xg[:dlt]], axis=0)
        cnt = (jnp.minimum(ridx + w // 2 - 1, rows - 1) - jnp.maximum(ridx - w // 2, 0) + 1).astype(F32)
        dd = (acc / cnt - xg).reshape(rows * GRID_W, pg)
        y = jnp.dot(dd.astype(BF16), pw_ref[g], preferred_element_type=F32)
        d_s[ctx_len:, cs] = (y * ps_ref[:, cs]).astype(BF16)
    o_ref[...] = jnp.dot(d_s[...], wo_ref[...], preferred_element_type=F32).astype(o_ref.dtype)


def _pool_module(px, col, pool_w_bf, pool_scale, w_o_bf, bsz, nt, ctx_len):
    t = px.shape[0]
    dp, d = w_o_bf.shape
    ng = len(POOL_WINDOWS)
    return pl.pallas_call(
        functools.partial(_pool_kernel, ctx_len),
        grid=(bsz,),
        in_specs=[
            pl.BlockSpec((nt, dp), lambda b: (b, col)),
            pl.BlockSpec((ng, dp // ng, dp // ng), lambda b: (0, 0, 0)),
            pl.BlockSpec((1, dp), lambda b: (0, 0)),
            pl.BlockSpec((dp, d), lambda b: (0, 0)),
        ],
        out_specs=pl.BlockSpec((nt, d), lambda b: (b, 0)),
        out_shape=jax.ShapeDtypeStruct((t, d), BF16),
        scratch_shapes=[pltpu.VMEM((nt, dp), BF16)],
        compiler_params=_cparams(("arbitrary",)),
        name="pool_module",
    )(px, pool_w_bf, pool_scale, w_o_bf)


def _mix_out_kernel(x_ref, gate_ref, z_ref, a_ref, h_ref, yc_ref, yp_ref, mod_ref, ng_ref, skip_ref, wm_ref,
                    wo_ref, g2_ref, rw_ref, rb_ref, xo_ref, hx_ref, lg_ref):
    d = x_ref.shape[1]
    dh = d // HEADS
    hsum = h_ref[0].astype(F32) + h_ref[1].astype(F32)
    parts = []
    for hh in range(HEADS):
        hb = hsum[:, hh * dh:(hh + 1) * dh]
        mu = jnp.mean(hb, axis=-1, keepdims=True)
        var = jnp.mean(jnp.square(hb - mu), axis=-1, keepdims=True)
        parts.append((hb - mu) * lax.rsqrt(var + LN_EPS))
    hn = jnp.concatenate(parts, axis=1)
    z = z_ref[...].astype(F32)
    tm = (hn * ng_ref[...] + skip_ref[...] * a_ref[...].astype(F32)) * (z * _sigmoid(z))
    y_m = jnp.dot(tm.astype(BF16), wm_ref[...], preferred_element_type=F32)
    merged = (_sigmoid(gate_ref[:, 0:d].astype(F32)) * yc_ref[...].astype(F32)
              + _sigmoid(gate_ref[:, d:2 * d].astype(F32)) * y_m
              + _sigmoid(gate_ref[:, 2 * d:3 * d].astype(F32)) * yp_ref[...].astype(F32))
    y = jnp.dot(merged.astype(BF16), wo_ref[...], preferred_element_type=F32)
    x = x_ref[...] + mod_ref[2:3, :] * y
    xo_ref[...] = x
    ms = jnp.mean(x * x, axis=-1, keepdims=True)
    hx = x * lax.rsqrt(ms + RMS_EPS) * g2_ref[...]
    hx = hx * (1.0 + mod_ref[4:5, :]) + mod_ref[3:4, :]
    hx_ref[...] = hx
    lg_ref[...] = jnp.dot(hx, rw_ref[...], precision=HIGHEST, preferred_element_type=F32) + rb_ref[...]


def _mix_out(xt, px, col_z, a, h2, y_conv, y_pool, modtab, norm_g, skip, w_m_bf, w_out_bf, g2, router_w, router_b,
             ntb, mod_base):
    t, d = xt.shape
    ne = router_w.shape[1]
    row = _mod_row(ntb, mod_base)
    vec = lambda: pl.BlockSpec((1, d), lambda i: (0, 0))
    return pl.pallas_call(
        _mix_out_kernel,
        grid=(t // TILE,),
        in_specs=[
            pl.BlockSpec((TILE, d), lambda i: (i, 0)),
            pl.BlockSpec((TILE, 3 * d), lambda i: (i, 0)),
            pl.BlockSpec((TILE, d), lambda i: (i, col_z)),
            pl.BlockSpec((TILE, d), lambda i: (i, 0)),
            pl.BlockSpec((2, TILE, d), lambda i: (0, i, 0)),
            pl.BlockSpec((TILE, d), lambda i: (i, 0)),
            pl.BlockSpec((TILE, d), lambda i: (i, 0)),
            pl.BlockSpec((None, 6, d), lambda i: (row(i), 0, 0)),
            vec(), vec(),
            pl.BlockSpec((d, d), lambda i: (0, 0)),
            pl.BlockSpec((d, d), lambda i: (0, 0)),
            vec(),
            pl.BlockSpec((d, ne), lambda i: (0, 0)),
            pl.BlockSpec((1, ne), lambda i: (0, 0)),
        ],
        out_specs=[
            pl.BlockSpec((TILE, d), lambda i: (i, 0)),
            pl.BlockSpec((TILE, d), lambda i: (i, 0)),
            pl.BlockSpec((TILE, ne), lambda i: (i, 0)),
        ],
        out_shape=[
            jax.ShapeDtypeStruct((t, d), F32),
            jax.ShapeDtypeStruct((t, d), F32),
            jax.ShapeDtypeStruct((t, ne), F32),
        ],
        compiler_params=_cparams(("arbitrary",)),
        name="mix_out",
    )(xt, px, px, a, h2, y_conv, y_pool, modtab, norm_g, skip, w_m_bf, w_out_bf, g2, router_w, router_b)


def _expert_kernel(be_ref, nu_ref, tok_ref, tokn_ref, hx_hbm, wgu_ref, bgu_ref, wdn_ref, bdn_ref, y_ref,
                   xbuf_a, xbuf_b, sem_a, sem_b, fsem, xb_s, wgu_s, wdn_s):
    i = pl.program_id(0)
    n_used = nu_ref[0]
    half = xbuf_a.shape[0]
    dff = wdn_s.shape[0]
    n_chunk = dff // MOE_CHUNK
    rows_per_chunk = half // n_chunk

    def row_copy(tok_smem, r, xbuf, sem):
        t = tok_smem[0, 0, r]
        return pltpu.make_async_copy(hx_hbm.at[pl.ds(t, 1), :], xbuf.at[pl.ds(r % half, 1), :], sem.at[0])

    def wait_rows(xbuf, sem):
        pltpu.make_async_copy(hx_hbm.at[pl.ds(0, half), :], xbuf, sem.at[0]).wait()

    @pl.when(i == 0)
    def _():
        def body(r, carry):
            row_copy(tok_ref, r, xbuf_a, sem_a).start()
            return carry
        lax.fori_loop(0, half, body, 0)

    e_now = be_ref[i]
    e_prev = be_ref[jnp.maximum(i - 1, 0)]

    @pl.when(jnp.logical_or(i == 0, e_now != e_prev))
    def _():
        wgu_s[...] = wgu_ref[...].astype(BF16)
        wdn_s[...] = wdn_ref[...].astype(BF16)

    def half_block(x_cur, sem_cur, x_nxt, sem_nxt, tok_nxt, row0_nxt, out_rows, fence):
        wait_rows(x_cur, sem_cur)
        xb_s[...] = x_cur[...].astype(BF16)
        y = None
        for c in range(n_chunk):
            if c < n_chunk // 2:
                for r in range(2 * c * rows_per_chunk, 2 * (c + 1) * rows_per_chunk):
                    row_copy(tok_nxt, row0_nxt + r, x_nxt, sem_nxt).start()
            if fence and c == n_chunk // 2:
                pl.semaphore_signal(fsem, 1)
                pl.semaphore_wait(fsem, 1)
            cg = slice(c * MOE_CHUNK, (c + 1) * MOE_CHUNK)
            cu = slice(dff + c * MOE_CHUNK, dff + (c + 1) * MOE_CHUNK)
            g = jnp.dot(xb_s[...], wgu_s[:, cg], preferred_element_type=F32) + bgu_ref[:, cg]
            u = jnp.dot(xb_s[...], wgu_s[:, cu], preferred_element_type=F32) + bgu_ref[:, cu]
            g = jnp.minimum(g, SWIGLU_LIMIT)
            u = jnp.clip(u, -SWIGLU_LIMIT, SWIGLU_LIMIT)
            act = (u + 1.0) * (g * _sigmoid(SWIGLU_ALPHA * g))
            part = jnp.dot(act.astype(BF16), wdn_s[cg, :], preferred_element_type=F32)
            y = part if y is None else y + part
        y_ref[out_rows, :] = y + bdn_ref[...]

    used = i < n_used

    @pl.when(used)
    def _():
        half_block(xbuf_a, sem_a, xbuf_b, sem_b, tok_ref, half, slice(0, half), False)
        half_block(xbuf_b, sem_b, xbuf_a, sem_a, tokn_ref, 0, slice(half, 2 * half), True)

    @pl.when(jnp.logical_not(used))
    def _():
        @pl.when(i == n_used)
        def _():
            wait_rows(xbuf_a, sem_a)

        y_ref[...] = jnp.zeros(y_ref.shape, y_ref.dtype)


def _experts(layer, hx, row_tok, block_e, n_used, w_gu, b_gu, w_dn, b_dn):
    t, d = hx.shape
    depth, ne, _, dff2 = w_gu.shape
    dff = dff2 // 2
    nb = block_e.shape[0]
    bm = MOE_BM
    half = bm // 2
    tok3 = row_tok.reshape(nb, 1, bm)
    grid_spec = pltpu.PrefetchScalarGridSpec(
        num_scalar_prefetch=2,
        grid=(nb,),
        in_specs=[
            pl.BlockSpec((1, 1, bm), lambda i, be, nu: (i, 0, 0), memory_space=pltpu.SMEM),
            pl.BlockSpec((1, 1, bm), lambda i, be, nu: (jnp.minimum(i + 1, nb - 1), 0, 0),
                         memory_space=pltpu.SMEM),
            pl.BlockSpec(memory_space=pl.ANY),
            pl.BlockSpec((None, None, d, dff2), lambda i, be, nu: (layer, be[i], 0, 0)),
            pl.BlockSpec((None, None, 1, dff2), lambda i, be, nu: (layer, be[i], 0, 0)),
            pl.BlockSpec((None, None, dff, d), lambda i, be, nu: (layer, be[i], 0, 0)),
            pl.BlockSpec((None, None, 1, d), lambda i, be, nu: (layer, be[i], 0, 0)),
        ],
        out_specs=pl.BlockSpec((bm, d), lambda i, be, nu: (i, 0)),
        scratch_shapes=[
            pltpu.VMEM((half, d), F32),
            pltpu.VMEM((half, d), F32),
            pltpu.SemaphoreType.DMA((1,)),
            pltpu.SemaphoreType.DMA((1,)),
            pltpu.SemaphoreType.REGULAR,
            pltpu.VMEM((half, d), BF16),
            pltpu.VMEM((d, dff2), BF16),
            pltpu.VMEM((dff, d), BF16),
        ],
    )
    return pl.pallas_call(
        _expert_kernel,
        grid_spec=grid_spec,
        out_shape=jax.ShapeDtypeStruct((nb * bm, d), F32),
        compiler_params=_cparams(("arbitrary",), no_bounds_checks=True),
        name="moe_experts",
    )(block_e, n_used, tok3, tok3, hx, w_gu, b_gu.reshape(depth, ne, 1, dff2), w_dn,
      b_dn.reshape(depth, ne, 1, d))


def _combine_kernel(dst_ref, dstn_ref, ys_hbm, gw_ref, x_ref, mod_ref, o_ref, buf, sem):
    i = pl.program_id(0)
    nsteps = pl.num_programs(0)
    slot = i % 2
    n_copy = TOP_K * TILE

    def row_copy(dst_smem, e, sl):
        src = dst_smem[0, 0, e]
        kk = e // TILE
        r = e % TILE
        return pltpu.make_async_copy(ys_hbm.at[pl.ds(src, 1), :], buf.at[sl, kk, pl.ds(r, 1), :], sem.at[sl])

    @pl.when(i == 0)
    def _():
        def body(e, carry):
            row_copy(dst_ref, e, 0).start()
            return carry
        lax.fori_loop(0, n_copy, body, 0)

    @pl.when(i + 1 < nsteps)
    def _():
        for e in range(n_copy):
            row_copy(dstn_ref, e, 1 - slot).start()

    for kk in range(TOP_K):
        pltpu.make_async_copy(ys_hbm.at[pl.ds(0, TILE), :], buf.at[slot, kk], sem.at[slot]).wait()

    gw = gw_ref[...]
    y = gw[:, 0:1] * buf[slot, 0]
    for kk in range(1, TOP_K):
        y = y + gw[:, kk:kk + 1] * buf[slot, kk]
    o_ref[...] = x_ref[...] + mod_ref[5:6, :] * y


def _combine(ys, dest_tiles, gate_w, xt, modtab, ntb, mod_base):
    t, d = xt.shape
    nt = t // TILE
    row = _mod_row(ntb, mod_base)
    return pl.pallas_call(
        _combine_kernel,
        grid=(nt,),
        in_specs=[
            pl.BlockSpec((1, 1, TOP_K * TILE), lambda i: (i, 0, 0), memory_space=pltpu.SMEM),
            pl.BlockSpec((1, 1, TOP_K * TILE), lambda i: (jnp.minimum(i + 1, nt - 1), 0, 0),
                         memory_space=pltpu.SMEM),
            pl.BlockSpec(memory_space=pl.ANY),
            pl.BlockSpec((TILE, TOP_K), lambda i: (i, 0)),
            pl.BlockSpec((TILE, d), lambda i: (i, 0)),
            pl.BlockSpec((None, 6, d), lambda i: (row(i), 0, 0)),
        ],
        out_specs=pl.BlockSpec((TILE, d), lambda i: (i, 0)),
        out_shape=jax.ShapeDtypeStruct((t, d), F32),
        scratch_shapes=[
            pltpu.VMEM((2, TOP_K, TILE, d), F32),
            pltpu.SemaphoreType.DMA((2,)),
        ],
        compiler_params=_cparams(("arbitrary",), no_bounds_checks=True),
        name="moe_combine",
    )(dest_tiles, dest_tiles, ys, gate_w, xt, modtab)


IN_CHUNKS = 6
IN_DMA_CHUNKS = 4


def _comb_in_kernel(dst_ref, dstn_ref, ys_hbm, gw_ref, x_ref, modp_ref, g_ref, mod_ref, w_ref, xo_ref, o_ref,
                    buf_a, buf_b, sem_a, sem_b, fsem, hb_s):
    i = pl.program_id(0)
    nsteps = pl.num_programs(0)
    n_copy = TOP_K * TILE
    n = w_ref.shape[1]
    cw = n // IN_CHUNKS
    per = n_copy // IN_DMA_CHUNKS

    def row_copy(dst_smem, e, buf, sem):
        src = dst_smem[0, 0, e]
        return pltpu.make_async_copy(ys_hbm.at[pl.ds(src, 1), :], buf.at[e // TILE, pl.ds(e % TILE, 1), :],
                                     sem.at[0])

    def wait_rows(buf, sem):
        for kk in range(TOP_K):
            pltpu.make_async_copy(ys_hbm.at[pl.ds(0, TILE), :], buf.at[kk], sem.at[0]).wait()

    @pl.when(i == 0)
    def _():
        def body(e, carry):
            row_copy(dst_ref, e, buf_a, sem_a).start()
            return carry
        lax.fori_loop(0, n_copy, body, 0)

    def tile(buf_cur, sem_cur, buf_nxt, sem_nxt):
        wait_rows(buf_cur, sem_cur)
        gw = gw_ref[...]
        y = gw[:, 0:1] * buf_cur[0]
        for kk in range(1, TOP_K):
            y = y + gw[:, kk:kk + 1] * buf_cur[kk]
        x = x_ref[...] + modp_ref[5:6, :] * y
        xo_ref[...] = x
        ms = jnp.mean(x * x, axis=-1, keepdims=True)
        h = x * lax.rsqrt(ms + RMS_EPS) * g_ref[...]
        h = h * (1.0 + mod_ref[1:2, :]) + mod_ref[0:1, :]
        hb_s[...] = h.astype(BF16)
        for c in range(IN_CHUNKS):
            if c < IN_DMA_CHUNKS:
                for e in range(c * per, (c + 1) * per):
                    row_copy(dstn_ref, e, buf_nxt, sem_nxt).start()
            if c == IN_DMA_CHUNKS:
                pl.semaphore_signal(fsem, 1)
                pl.semaphore_wait(fsem, 1)
            cs = slice(c * cw, (c + 1) * cw)
            o_ref[:, cs] = jnp.dot(hb_s[...], w_ref[:, cs], preferred_element_type=F32).astype(o_ref.dtype)

    even = i % 2 == 0

    @pl.when(even)
    def _():
        tile(buf_a, sem_a, buf_b, sem_b)

    @pl.when(jnp.logical_not(even))
    def _():
        tile(buf_b, sem_b, buf_a, sem_a)

    @pl.when(jnp.logical_and(i == nsteps - 1, even))
    def _():
        wait_rows(buf_b, sem_b)

    @pl.when(jnp.logical_and(i == nsteps - 1, jnp.logical_not(even)))
    def _():
        wait_rows(buf_a, sem_a)


def _comb_in_proj(ys, dest_tiles, gate_w, xt, g1, modtab, w_in_bf, ntb, mod_base_prev, mod_base):
    t, d = xt.shape
    n = w_in_bf.shape[1]
    nt = t // TILE
    assert n % (IN_CHUNKS * 256) == 0 and (TOP_K * TILE) % IN_DMA_CHUNKS == 0
    row_prev = _mod_row(ntb, mod_base_prev)
    row = _mod_row(ntb, mod_base)
    return pl.pallas_call(
        _comb_in_kernel,
        grid=(nt,),
        in_specs=[
            pl.BlockSpec((1, 1, TOP_K * TILE), lambda i: (i, 0, 0), memory_space=pltpu.SMEM),
            pl.BlockSpec((1, 1, TOP_K * TILE), lambda i: (jnp.minimum(i + 1, nt - 1), 0, 0),
                         memory_space=pltpu.SMEM),
            pl.BlockSpec(memory_space=pl.ANY),
            pl.BlockSpec((TILE, TOP_K), lambda i: (i, 0)),
            pl.BlockSpec((TILE, d), lambda i: (i, 0)),
            pl.BlockSpec((None, 6, d), lambda i: (row_prev(i), 0, 0)),
            pl.BlockSpec((1, d), lambda i: (0, 0)),
            pl.BlockSpec((None, 6, d), lambda i: (row(i), 0, 0)),
            pl.BlockSpec((d, n), lambda i: (0, 0), pipeline_mode=pl.Buffered(1)),
        ],
        out_specs=[
            pl.BlockSpec((TILE, d), lambda i: (i, 0)),
            pl.BlockSpec((TILE, n), lambda i: (i, 0)),
        ],
        out_shape=[
            jax.ShapeDtypeStruct((t, d), F32),
            jax.ShapeDtypeStruct((t, n), BF16),
        ],
        scratch_shapes=[
            pltpu.VMEM((TOP_K, TILE, d), F32),
            pltpu.VMEM((TOP_K, TILE, d), F32),
            pltpu.SemaphoreType.DMA((1,)),
            pltpu.SemaphoreType.DMA((1,)),
            pltpu.SemaphoreType.REGULAR,
            pltpu.VMEM((TILE, d), BF16),
        ],
        compiler_params=_cparams(("arbitrary",), no_bounds_checks=True),
        name="comb_in_proj",
    )(dest_tiles, dest_tiles, ys, gate_w, xt, modtab, g1, modtab, w_in_bf)


def _final_kernel(x_ref, g_ref, o_ref):
    x = x_ref[...]
    ms = jnp.mean(x * x, axis=-1, keepdims=True)
    o_ref[...] = x * lax.rsqrt(ms + RMS_EPS) * g_ref[...]


def _final_norm(xt, final_g, bsz, ntb):
    t, d = xt.shape
    nlat = ntb - 1
    return pl.pallas_call(
        _final_kernel,
        grid=(bsz, nlat),
        in_specs=[
            pl.BlockSpec((TILE, d), lambda b, j: (b * ntb + 1 + j, 0)),
            pl.BlockSpec((1, d), lambda b, j: (0, 0)),
        ],
        out_specs=pl.BlockSpec((TILE, d), lambda b, j: (b * nlat + j, 0)),
        out_shape=jax.ShapeDtypeStruct((bsz * nlat * TILE, d), F32),
        compiler_params=_cparams(("arbitrary", "arbitrary")),
        name="final_norm",
    )(xt, final_g)


def _routing(logits, n_tiles):
    t = logits.shape[0]
    top_v, top_e = lax.top_k(logits, TOP_K)
    gate = jax.nn.softmax(top_v, axis=-1)
    n_assign = t * TOP_K
    flat_e = top_e.reshape(-1).astype(jnp.int32)
    order = jnp.argsort(flat_e, stable=True).astype(jnp.int32)
    rank = jnp.argsort(order).astype(jnp.int32)
    onehot = (flat_e[:, None] == jnp.arange(N_EXPERTS, dtype=jnp.int32)[None, :]).astype(jnp.int32)
    counts = jnp.sum(onehot, axis=0)
    grp_start = jnp.cumsum(counts) - counts
    padded = (counts + MOE_BM - 1) // MOE_BM * MOE_BM
    pad_end = jnp.cumsum(padded)
    pad_start = pad_end - padded
    dest = (rank + jnp.sum(onehot * (pad_start - grp_start)[None, :], axis=1)).reshape(t, TOP_K)
    n_blocks = n_assign // MOE_BM + N_EXPERTS
    blk = jnp.arange(n_blocks, dtype=jnp.int32)
    block_e = jnp.minimum(jnp.searchsorted(pad_end, blk * MOE_BM, side='right'), N_EXPERTS - 1).astype(jnp.int32)
    src0 = grp_start[block_e] + blk * MOE_BM - pad_start[block_e]
    lim = grp_start[block_e] + counts[block_e]
    src = src0[:, None] + jnp.arange(MOE_BM, dtype=jnp.int32)[None, :]
    valid = src < lim[:, None]
    row_tok = jnp.where(valid, order[jnp.clip(src, 0, n_assign - 1)] // TOP_K, 0).astype(jnp.int32).reshape(-1)
    n_used = (pad_end[-1] // MOE_BM).astype(jnp.int32).reshape(1)
    dest_tiles = dest.reshape(n_tiles, TILE, TOP_K).transpose(0, 2, 1).reshape(n_tiles, 1, TOP_K * TILE)
    return gate, row_tok, block_e, n_used, dest_tiles


def _block_diag(w, heads):
    nb, c, _ = w.shape
    per = nb // heads
    w = w.reshape(heads, per, c, c)
    eye = jnp.eye(per, dtype=w.dtype)
    full = eye[None, :, None, :, None] * w[:, :, :, None, :]
    return full.reshape(heads, per * c, per * c)


def kernel(x, c, ctx, c_ctx, w_ada, b_ada, norm1_g, norm2_g, w_in, conv_dw_w, conv_dw_b, conv_ln_g, conv_ln_b, w_conv_out, m_sc_w, m_sc_b, m_wq, m_wk, m_w_if, m_b_if, m_norm_g, m_skip, w_m_out, pool_w, pool_scale, w_pool_out, w_out, router_w, router_b, w_gu, b_gu, w_dn, b_dn, final_g):
    bsz, seq, d = x.shape
    ctx_len = ctx.shape[1]
    depth = w_ada.shape[0]
    assert ctx_len == TILE and seq % TILE == 0 and d % (128 * HEADS) == 0
    nt = ctx_len + seq
    ntb = nt // TILE
    t = bsz * nt
    n_tiles = t // TILE
    d_conv = w_conv_out.shape[1]
    d_pool = w_pool_out.shape[1]
    d_m = w_m_out.shape[1]
    assert d_m == d and 2 * d_conv == d and 2 * d_pool == d
    c_conv = 3 * d
    c_pool = c_conv + 2 * d_conv
    c_mz = c_pool + d_pool
    c_mqk = c_mz + d_m
    c_mv = c_mqk + d_m
    d_in = c_mv + d_m
    perm = jnp.concatenate([jnp.arange(0, c_pool), jnp.arange(c_mz, d_in), jnp.arange(c_pool, c_mz)])
    col_conv, col_z, col_u, col_v = 3, 4, 5, 6
    col_pool = (7 * d) // d_pool

    xt = jnp.concatenate([ctx, x], axis=1).reshape(t, d)

    mp = -(-(bsz + 1) // 8) * 8
    c_all = jnp.zeros((mp, d), F32).at[:bsz].set(c).at[bsz].set(c_ctx)
    mod = _ada(c_all, w_ada, b_ada)
    mod_x = mod[:, :bsz].reshape(depth, bsz, 1, 6, d)
    mod_c = jnp.broadcast_to(mod[:, bsz].reshape(depth, 1, 1, 6, d), (depth, bsz, 1, 6, d))
    modtab = jnp.concatenate([mod_c, mod_x], axis=2).reshape(depth * bsz * 2, 6, d)

    gh = 2 * HEADS
    for l in range(depth):
        mod_base = l * bsz * 2
        w_in_bf = jnp.take(w_in[l], perm, axis=1).astype(BF16)
        if l == 0:
            px = _in_proj(xt, norm1_g[l].reshape(1, d), modtab, w_in_bf, ntb, mod_base)
        else:
            xt, px = _comb_in_proj(ys, dest_tiles, gate_w, xt, norm1_g[l].reshape(1, d), modtab, w_in_bf, ntb,
                                   mod_base - bsz * 2, mod_base)

        wq_bd = _block_diag(m_wq[l], HEADS).astype(BF16)
        wk_bd = _block_diag(m_wk[l], HEADS).astype(BF16)
        wif = jnp.concatenate([m_w_if[l, 0], m_w_if[l, 1]], axis=1)
        bif = jnp.concatenate([m_b_if[l, 0], m_b_if[l, 1]], axis=0)
        a, q, k, gates, gates_t = _mlstm_pre(
            px, col_u, col_v, m_sc_w[l], m_sc_b[l].reshape(1, d), wq_bd, wk_bd,
            wif.astype(BF16), wif.T.astype(BF16), bif.reshape(1, 2 * gh), bif.reshape(2 * gh, 1), bsz, ntb)
        h2 = _mlstm_scan(q, k, px, col_v, gates, gates_t, bsz, ntb)

        y_conv = _conv_module(px, col_conv, conv_dw_w[l], conv_dw_b[l].reshape(1, d_conv),
                              conv_ln_g[l].reshape(1, d_conv), conv_ln_b[l].reshape(1, d_conv),
                              w_conv_out[l].astype(BF16), bsz, ntb)
        y_pool = _pool_module(px, col_pool, pool_w[l].astype(BF16), pool_scale[l].reshape(1, d_pool),
                              w_pool_out[l].astype(BF16), bsz, nt, ctx_len)

        xt, hx, logits = _mix_out(
            xt, px, col_z, a, h2, y_conv, y_pool, modtab, m_norm_g[l].reshape(1, d), m_skip[l].reshape(1, d),
            w_m_out[l].astype(BF16), w_out[l].astype(BF16), norm2_g[l].reshape(1, d), router_w[l],
            router_b[l].reshape(1, N_EXPERTS), ntb, mod_base)

        gate_w, row_tok, block_e, n_used, dest_tiles = _routing(logits, n_tiles)
        ys = _experts(l, hx, row_tok, block_e, n_used, w_gu, b_gu, w_dn, b_dn)

    xt = _combine(ys, dest_tiles, gate_w, xt, modtab, ntb, (depth - 1) * bsz * 2)
    out = _final_norm(xt, final_g.reshape(1, d), bsz, ntb)
    return out.reshape(bsz, seq, d)
```

```python
import functools

import jax
import jax.numpy as jnp
from jax import lax
from jax.experimental import pallas as pl
from jax.experimental.pallas import tpu as pltpu

F32 = jnp.float32
BF16 = jnp.bfloat16
HIGHEST = lax.Precision.HIGHEST

TILE = 256
GRID_W = 64
CONV_WIDTH = 31
SUBLANES = 8
CONV_HALO = 16
SHORT_CONV = 4
HEADS = 4
POOL_WINDOWS = (2, 4, 8, 16)
N_EXPERTS = 32
TOP_K = 4
MOE_BM = 512
MOE_CHUNK = 256
SWIGLU_LIMIT = 7.0
SWIGLU_ALPHA = 1.702
RMS_EPS = 1e-6
LN_EPS = 1e-5
NEG = -1e30
VMEM_LIMIT = 56 * 1024 * 1024


def _cparams(sem, no_bounds_checks=False):
    return pltpu.CompilerParams(dimension_semantics=sem, vmem_limit_bytes=VMEM_LIMIT,
                                disable_bounds_checks=no_bounds_checks)


def _sigmoid(x):
    return jax.nn.sigmoid(x)


def _log_sigmoid(x):
    return jnp.minimum(x, 0.0) - jnp.log1p(jnp.exp(-jnp.abs(x)))


def _ada_kernel(c_ref, w_ref, b_ref, o_ref):
    s = c_ref[...]
    s = s * _sigmoid(s)
    o_ref[...] = jnp.dot(s, w_ref[...], precision=HIGHEST, preferred_element_type=F32) + b_ref[...]


def _ada(c_all, w_ada, b_ada):
    depth, d, n = w_ada.shape
    mp = c_all.shape[0]
    tn = 1536
    return pl.pallas_call(
        _ada_kernel,
        grid=(depth, n // tn),
        in_specs=[
            pl.BlockSpec((mp, d), lambda l, j: (0, 0)),
            pl.BlockSpec((None, d, tn), lambda l, j: (l, 0, j)),
            pl.BlockSpec((None, 1, tn), lambda l, j: (l, 0, j)),
        ],
        out_specs=pl.BlockSpec((None, mp, tn), lambda l, j: (l, 0, j)),
        out_shape=jax.ShapeDtypeStruct((depth, mp, n), F32),
        compiler_params=_cparams(("arbitrary", "arbitrary")),
        name="ada",
    )(c_all, w_ada, b_ada.reshape(depth, 1, n))


def _in_kernel(x_ref, g_ref, mod_ref, w_ref, o_ref):
    x = x_ref[...]
    ms = jnp.mean(x * x, axis=-1, keepdims=True)
    y = x * lax.rsqrt(ms + RMS_EPS) * g_ref[...]
    h = y * (1.0 + mod_ref[1:2, :]) + mod_ref[0:1, :]
    o_ref[...] = jnp.dot(h.astype(BF16), w_ref[...], preferred_element_type=F32).astype(BF16)


def _mod_row(ntb, base):
    def f(i):
        return base + 2 * (i // ntb) + jnp.minimum(i % ntb, 1)
    return f


def _in_proj(xt, g1, modtab, w_in_bf, ntb, mod_base):
    t, d = xt.shape
    n = w_in_bf.shape[1]
    row = _mod_row(ntb, mod_base)
    return pl.pallas_call(
        _in_kernel,
        grid=(t // TILE,),
        in_specs=[
            pl.BlockSpec((TILE, d), lambda i: (i, 0)),
            pl.BlockSpec((1, d), lambda i: (0, 0)),
            pl.BlockSpec((None, 6, d), lambda i: (row(i), 0, 0)),
            pl.BlockSpec((d, n), lambda i: (0, 0), pipeline_mode=pl.Buffered(1)),
        ],
        out_specs=pl.BlockSpec((TILE, n), lambda i: (i, 0)),
        out_shape=jax.ShapeDtypeStruct((t, n), BF16),
        compiler_params=_cparams(("arbitrary",)),
        name="in_proj",
    )(xt, g1, modtab, w_in_bf)


def _mpre_kernel(ntb, u_ref, up_ref, un_ref, v_ref, scw_ref, scb_ref, wq_ref, wk_ref, wif_ref, wift_ref,
                 bif_ref, bift_ref, a_ref, q_ref, k_ref, g_ref, gt_ref):
    i = pl.program_id(1)
    d = u_ref.shape[1]
    dh = d // HEADS
    u = u_ref[...].astype(F32)
    up = jnp.where(i >= 2, up_ref[...].astype(F32)[8:16, :], 0.0)
    un = jnp.where(jnp.logical_and(i >= 1, i <= ntb - 2), un_ref[...].astype(F32)[0:8, :], 0.0)
    ext = jnp.concatenate([up, u, un], axis=0)
    n_ext = TILE + 16
    conv = scb_ref[...]
    for j in range(SHORT_CONV):
        sh = (SHORT_CONV // 2 - j) % n_ext
        shifted = ext if sh == 0 else pltpu.roll(ext, sh, 0)
        conv = conv + scw_ref[j:j + 1, :] * shifted[8:8 + TILE, :]
    a = conv * _sigmoid(conv)
    ab = a.astype(BF16)
    a_ref[...] = ab
    qs, ks = [], []
    for hb in range(HEADS):
        blk = ab[:, hb * dh:(hb + 1) * dh]
        qs.append(jnp.dot(blk, wq_ref[hb], preferred_element_type=F32))
        ks.append(jnp.dot(blk, wk_ref[hb], preferred_element_type=F32))
    q = jnp.concatenate(qs, axis=1)
    k = jnp.concatenate(ks, axis=1)
    qb = q.astype(BF16)
    kb = k.astype(BF16)
    vb = v_ref[...]
    q_ref[...] = (q * (dh ** -0.5)).astype(BF16)
    k_ref[...] = kb
    pre = bif_ref[...]
    pret = bift_ref[...]
    nt_dims = (((1,), (1,)), ((), ()))
    for part, f in enumerate((qb, kb, vb)):
        pre = pre + jnp.dot(f, wif_ref[part * d:(part + 1) * d, :], preferred_element_type=F32)
        pret = pret + lax.dot_general(wift_ref[:, part * d:(part + 1) * d], f, nt_dims,
                                      preferred_element_type=F32)
    col = lax.broadcasted_iota(jnp.int32, pre.shape, 1)
    pre = jnp.where(col % (2 * HEADS) >= HEADS, _log_sigmoid(pre), pre)
    rowi = lax.broadcasted_iota(jnp.int32, pret.shape, 0)
    pret = jnp.where(rowi % (2 * HEADS) >= HEADS, _log_sigmoid(pret), pret)
    g_ref[0] = pre[:, 0:2 * HEADS]
    g_ref[1] = pre[:, 2 * HEADS:4 * HEADS]
    gt_ref[0] = pret[0:2 * HEADS, :]
    gt_ref[1] = pret[2 * HEADS:4 * HEADS, :]


def _mlstm_pre(px, col_u, col_v, sc_w, sc_b, wq_bd, wk_bd, wif, wift, bif, bift, bsz, ntb):
    t = px.shape[0]
    d = sc_w.shape[1]
    hrows = 16
    nh = t // hrows
    per = TILE // hrows

    def tile(b, i):
        return b * ntb + i

    gh = 2 * HEADS
    return pl.pallas_call(
        functools.partial(_mpre_kernel, ntb),
        grid=(bsz, ntb),
        in_specs=[
            pl.BlockSpec((TILE, d), lambda b, i: (tile(b, i), col_u)),
            pl.BlockSpec((hrows, d), lambda b, i: (jnp.maximum(tile(b, i) * per - 1, 0), col_u)),
            pl.BlockSpec((hrows, d), lambda b, i: (jnp.minimum((tile(b, i) + 1) * per, nh - 1), col_u)),
            pl.BlockSpec((TILE, d), lambda b, i: (tile(b, i), col_v)),
            pl.BlockSpec((SHORT_CONV, d), lambda b, i: (0, 0)),
            pl.BlockSpec((1, d), lambda b, i: (0, 0)),
            pl.BlockSpec((HEADS, d // HEADS, d // HEADS), lambda b, i: (0, 0, 0)),
            pl.BlockSpec((HEADS, d // HEADS, d // HEADS), lambda b, i: (0, 0, 0)),
            pl.BlockSpec((3 * d, 2 * gh), lambda b, i: (0, 0)),
            pl.BlockSpec((2 * gh, 3 * d), lambda b, i: (0, 0)),
            pl.BlockSpec((1, 2 * gh), lambda b, i: (0, 0)),
            pl.BlockSpec((2 * gh, 1), lambda b, i: (0, 0)),
        ],
        out_specs=[
            pl.BlockSpec((TILE, d), lambda b, i: (tile(b, i), 0)),
            pl.BlockSpec((TILE, d), lambda b, i: (tile(b, i), 0)),
            pl.BlockSpec((TILE, d), lambda b, i: (tile(b, i), 0)),
            pl.BlockSpec((2, TILE, gh), lambda b, i: (0, tile(b, i), 0)),
            pl.BlockSpec((2, gh, TILE), lambda b, i: (0, 0, tile(b, i))),
        ],
        out_shape=[
            jax.ShapeDtypeStruct((t, d), BF16),
            jax.ShapeDtypeStruct((t, d), BF16),
            jax.ShapeDtypeStruct((t, d), BF16),
            jax.ShapeDtypeStruct((2, t, gh), F32),
            jax.ShapeDtypeStruct((2, gh, t), F32),
        ],
        compiler_params=_cparams(("arbitrary", "arbitrary")),
        name="mlstm_pre",
    )(px, px, px, px, sc_w, sc_b, wq_bd, wk_bd, wif, wift, bif, bift)


AUG = 128


def _scan_kernel(q_ref, k_ref, v_ref, g_ref, gt_ref, h_ref, cta_s, m_s):
    dr = pl.program_id(1)
    i = pl.program_id(2)
    dh = q_ref.shape[1] // HEADS

    @pl.when(i == 0)
    def _():
        cta_s[...] = jnp.zeros(cta_s.shape, F32)
        m_s[...] = jnp.zeros(m_s.shape, F32)

    x = g_ref[...]
    xt = gt_ref[...]
    rid = lax.broadcasted_iota(jnp.int32, x.shape, 0)
    cid = lax.broadcasted_iota(jnp.int32, xt.shape, 1)
    p = x
    pt = xt
    s = 1
    while s < TILE:
        p = p + jnp.where(rid >= s, pltpu.roll(p, s, 0), 0.0)
        pt = pt + jnp.where(cid >= s, pltpu.roll(pt, s, 1), 0.0)
        s *= 2
    tot = p[TILE - 1:TILE, :]
    tott = pt[:, TILE - 1:TILE]
    fwd = dr == 0
    bc = jnp.where(fwd, p, tot - p + x)
    br = jnp.where(fwd, pt, tott - pt + xt)
    amat = x[:, 0:HEADS] - bc[:, HEADS:2 * HEADS]
    rida = lax.broadcasted_iota(jnp.int32, amat.shape, 0)
    cpre = amat
    csuf = amat
    s = 1
    while s < TILE:
        cpre = jnp.maximum(cpre, jnp.where(rida >= s, pltpu.roll(cpre, s, 0), NEG))
        csuf = jnp.maximum(csuf, jnp.where(rida < TILE - s, pltpu.roll(csuf, TILE - s, 0), NEG))
        s *= 2
    cmax = jnp.where(fwd, cpre, csuf)
    r2 = lax.broadcasted_iota(jnp.int32, (TILE, TILE), 0)
    c2 = lax.broadcasted_iota(jnp.int32, (TILE, TILE), 1)
    tri = (r2 - c2) * (1 - 2 * dr) >= 0
    lane = lax.broadcasted_iota(jnp.int32, (TILE, AUG), 1)
    one_col = jnp.where(lane == 0, 1.0, 0.0)
    nt_dims = (((1,), (1,)), ((), ()))
    tn_dims = (((0,), (0,)), ((), ()))
    for hh in range(HEADS):
        sl = slice(hh * dh, (hh + 1) * dh)
        q = q_ref[:, sl]
        k = k_ref[:, sl]
        v = v_ref[:, sl]
        b_col = bc[:, HEADS + hh:HEADS + hh + 1]
        li_col = x[:, hh:hh + 1]
        row_vec = xt[hh:hh + 1, :] - br[HEADS + hh:HEADS + hh + 1, :]
        b_end = tot[:, HEADS + hh:HEADS + hh + 1]
        m_prev = m_s[hh][0:1, 0:1]
        col_vec = -jnp.maximum(m_prev, cmax[:, hh:hh + 1])
        m_t = b_col - col_vec
        dmat = jnp.exp(jnp.where(tri, row_vec + col_vec, NEG))
        sc = lax.dot_general(q, k, nt_dims, preferred_element_type=F32) * dmat
        w_inter = jnp.exp(m_prev + col_vec)
        cta = cta_s[hh]
        v_aug = jnp.concatenate([v, one_col.astype(BF16)], axis=1)
        nd = (jnp.dot(sc.astype(BF16), v_aug, preferred_element_type=F32)
              + w_inter * jnp.dot(q, cta.astype(BF16), preferred_element_type=F32))
        den = nd[:, dh:dh + 1]
        h = nd[:, 0:dh] / jnp.maximum(jnp.abs(den), jnp.exp(-m_t))
        h_ref[:, sl] = h.astype(h_ref.dtype)
        g_col = b_end - b_col + li_col
        m_new = jnp.maximum(b_end + m_prev, jnp.max(g_col, axis=0, keepdims=True))
        wj = jnp.exp(g_col - m_new)
        decay = jnp.exp(b_end + m_prev - m_new)
        vw = jnp.concatenate([(v.astype(F32) * wj).astype(BF16), (one_col * wj).astype(BF16)], axis=1)
        cta_s[hh] = decay * cta + lax.dot_general(k, vw, tn_dims, preferred_element_type=F32)
        m_s[hh] = jnp.broadcast_to(m_new, m_s.shape[1:])


def _mlstm_scan(q, k, px, col_v, gates, gates_t, bsz, ntb):
    t, d = q.shape
    dh = d // HEADS
    gh = 2 * HEADS

    def chunk(b, dr, i):
        c = jnp.where(dr == 0, i, jnp.where(i == 0, 0, ntb - i))
        return b * ntb + c

    return pl.pallas_call(
        _scan_kernel,
        grid=(bsz, 2, ntb),
        in_specs=[
            pl.BlockSpec((TILE, d), lambda b, dr, i: (chunk(b, dr, i), 0)),
            pl.BlockSpec((TILE, d), lambda b, dr, i: (chunk(b, dr, i), 0)),
            pl.BlockSpec((TILE, d), lambda b, dr, i: (chunk(b, dr, i), col_v)),
            pl.BlockSpec((None, TILE, gh), lambda b, dr, i: (dr, chunk(b, dr, i), 0)),
            pl.BlockSpec((None, gh, TILE), lambda b, dr, i: (dr, 0, chunk(b, dr, i))),
        ],
        out_specs=pl.BlockSpec((None, TILE, d), lambda b, dr, i: (dr, chunk(b, dr, i), 0)),
        out_shape=jax.ShapeDtypeStruct((2, t, d), BF16),
        scratch_shapes=[
            pltpu.VMEM((HEADS, dh, dh + AUG), F32),
            pltpu.VMEM((HEADS, 8, 128), F32),
        ],
        compiler_params=_cparams(("arbitrary", "arbitrary", "arbitrary")),
        name="mlstm_scan",
    )(q, k, px, gates, gates_t)


def _conv_kernel(u_ref, dw_ref, db_ref, lg_ref, lb_ref, wo_ref, o_ref, pad_s):
    i = pl.program_id(1)
    dc = dw_ref.shape[1]
    u = u_ref[...].astype(F32)
    a = u[:, :dc] * _sigmoid(u[:, dc:])
    zeros = jnp.zeros((CONV_HALO, dc), F32)

    def depthwise(seg_len, n_seg):
        stride = seg_len + 2 * CONV_HALO
        used = n_seg * stride
        for s in range(n_seg):
            base = s * stride
            pad_s[0, base:base + CONV_HALO, :] = zeros
            pad_s[0, base + CONV_HALO:base + CONV_HALO + seg_len, :] = a[s * seg_len:(s + 1) * seg_len, :]
            pad_s[0, base + CONV_HALO + seg_len:base + stride, :] = zeros
        for rho in range(1, SUBLANES):
            pad_s[rho, 0:used - SUBLANES, :] = pad_s[0, rho:used - SUBLANES + rho, :]
        outs = []
        for s in range(n_seg):
            base = s * stride
            acc = jnp.zeros((seg_len, dc), F32)
            for j in range(CONV_WIDTH):
                off = base + CONV_HALO - CONV_WIDTH // 2 + j
                rho = off % SUBLANES
                acc = acc + dw_ref[j:j + 1, :] * pad_s[rho, off - rho:off - rho + seg_len, :]
            outs.append(acc)
        return outs[0] if n_seg == 1 else jnp.concatenate(outs, axis=0)

    def finish(acc):
        acc = acc + db_ref[...]
        mu = jnp.mean(acc, axis=-1, keepdims=True)
        var = jnp.mean(jnp.square(acc - mu), axis=-1, keepdims=True)
        y = (acc - mu) * lax.rsqrt(var + LN_EPS) * lg_ref[...] + lb_ref[...]
        y = y * _sigmoid(y)
        o_ref[...] = jnp.dot(y.astype(BF16), wo_ref[...], preferred_element_type=F32).astype(o_ref.dtype)

    @pl.when(i == 0)
    def _():
        finish(depthwise(TILE, 1))

    @pl.when(i > 0)
    def _():
        finish(depthwise(GRID_W, TILE // GRID_W))


def _conv_module(px, col, dw_w, dw_b, ln_g, ln_b, w_o_bf, bsz, ntb):
    t = px.shape[0]
    dc, d = w_o_bf.shape
    return pl.pallas_call(
        _conv_kernel,
        grid=(bsz, ntb),
        in_specs=[
            pl.BlockSpec((TILE, 2 * dc), lambda b, i: (b * ntb + i, col)),
            pl.BlockSpec((CONV_WIDTH, dc), lambda b, i: (0, 0)),
            pl.BlockSpec((1, dc), lambda b, i: (0, 0)),
            pl.BlockSpec((1, dc), lambda b, i: (0, 0)),
            pl.BlockSpec((1, dc), lambda b, i: (0, 0)),
            pl.BlockSpec((dc, d), lambda b, i: (0, 0)),
        ],
        out_specs=pl.BlockSpec((TILE, d), lambda b, i: (b * ntb + i, 0)),
        out_shape=jax.ShapeDtypeStruct((t, d), BF16),
        scratch_shapes=[pltpu.VMEM((SUBLANES, (TILE // GRID_W) * (GRID_W + 2 * CONV_HALO), dc), F32)],
        compiler_params=_cparams(("arbitrary", "arbitrary")),
        name="conv_module",
    )(px, dw_w, dw_b, ln_g, ln_b, w_o_bf)


def _pool_kernel(ctx_len, u_ref, pw_ref, ps_ref, wo_ref, o_ref, d_s):
    nt, dp = u_ref.shape
    pg = dp // len(POOL_WINDOWS)
    rows = (nt - ctx_len) // GRID_W
    uf = u_ref[...].astype(F32)
    tq = lax.broadcasted_iota(jnp.int32, (ctx_len, ctx_len), 0)
    tk = lax.broadcasted_iota(jnp.int32, (ctx_len, ctx_len), 1)
    for g, w in enumerate(POOL_WINDOWS):
        cs = slice(g * pg, (g + 1) * pg)
        lo = jnp.maximum(tq - w // 2, 0)
        hi = jnp.minimum(tq + w // 2 - 1, ctx_len - 1)
        band = jnp.where(tk >= lo, jnp.where(tk <= hi, 1.0, 0.0), 0.0).astype(BF16)
        ssum = jnp.dot(band, u_ref[0:ctx_len, cs], preferred_element_type=F32)
        cnt = (hi - lo + 1)[:, 0:1].astype(F32)
        dd = ssum / cnt - uf[0:ctx_len, cs]
        y = jnp.dot(dd.astype(BF16), pw_ref[g], preferred_element_type=F32)
        d_s[0:ctx_len, cs] = (y * ps_ref[:, cs]).astype(BF16)
    ridx = lax.broadcasted_iota(jnp.int32, (rows, GRID_W, pg), 0)
    for g, w in enumerate(POOL_WINDOWS):
        cs = slice(g * pg, (g + 1) * pg)
        xg = uf[ctx_len:, cs].reshape(rows, GRID_W, pg)
        acc = xg
        for dlt in range(-(w // 2), w // 2):
            if dlt == 0:
                continue
            z = jnp.zeros((abs(dlt), GRID_W, pg), F32)
            if dlt > 0:
                acc = acc + jnp.concatenate([xg[dlt:], z], axis=0)
            else:
                acc = acc + jnp.concatenate([z, xg[:dlt]], axis=0)
        cnt = (jnp.minimum(ridx + w // 2 - 1, rows - 1) - jnp.maximum(ridx - w // 2, 0) + 1).astype(F32)
        dd = (acc / cnt - xg).reshape(rows * GRID_W, pg)
        y = jnp.dot(dd.astype(BF16), pw_ref[g], preferred_element_type=F32)
        d_s[ctx_len:, cs] = (y * ps_ref[:, cs]).astype(BF16)
    o_ref[...] = jnp.dot(d_s[...], wo_ref[...], preferred_element_type=F32).astype(o_ref.dtype)


def _pool_module(px, col, pool_w_bf, pool_scale, w_o_bf, bsz, nt, ctx_len):
    t = px.shape[0]
    dp, d = w_o_bf.shape
    ng = len(POOL_WINDOWS)
    return pl.pallas_call(
        functools.partial(_pool_kernel, ctx_len),
        grid=(bsz,),
        in_specs=[
            pl.BlockSpec((nt, dp), lambda b: (b, col)),
            pl.BlockSpec((ng, dp // ng, dp // ng), lambda b: (0, 0, 0)),
            pl.BlockSpec((1, dp), lambda b: (0, 0)),
            pl.BlockSpec((dp, d), lambda b: (0, 0)),
        ],
        out_specs=pl.BlockSpec((nt, d), lambda b: (b, 0)),
        out_shape=jax.ShapeDtypeStruct((t, d), BF16),
        scratch_shapes=[pltpu.VMEM((nt, dp), BF16)],
        compiler_params=_cparams(("arbitrary",)),
        name="pool_module",
    )(px, pool_w_bf, pool_scale, w_o_bf)


def _mix_out_kernel(x_ref, gate_ref, z_ref, a_ref, h_ref, yc_ref, yp_ref, mod_ref, ng_ref, skip_ref, wm_ref,
                    wo_ref, g2_ref, rw_ref, rb_ref, xo_ref, hx_ref, lg_ref):
    d = x_ref.shape[1]
    dh = d // HEADS
    hsum = h_ref[0].astype(F32) + h_ref[1].astype(F32)
    parts = []
    for hh in range(HEADS):
        hb = hsum[:, hh * dh:(hh + 1) * dh]
        mu = jnp.mean(hb, axis=-1, keepdims=True)
        var = jnp.mean(jnp.square(hb - mu), axis=-1, keepdims=True)
        parts.append((hb - mu) * lax.rsqrt(var + LN_EPS))
    hn = jnp.concatenate(parts, axis=1)
    z = z_ref[...].astype(F32)
    tm = (hn * ng_ref[...] + skip_ref[...] * a_ref[...].astype(F32)) * (z * _sigmoid(z))
    y_m = jnp.dot(tm.astype(BF16), wm_ref[...], preferred_element_type=F32)
    merged = (_sigmoid(gate_ref[:, 0:d].astype(F32)) * yc_ref[...].astype(F32)
              + _sigmoid(gate_ref[:, d:2 * d].astype(F32)) * y_m
              + _sigmoid(gate_ref[:, 2 * d:3 * d].astype(F32)) * yp_ref[...].astype(F32))
    y = jnp.dot(merged.astype(BF16), wo_ref[...], preferred_element_type=F32)
    x = x_ref[...] + mod_ref[2:3, :] * y
    xo_ref[...] = x
    ms = jnp.mean(x * x, axis=-1, keepdims=True)
    hx = x * lax.rsqrt(ms + RMS_EPS) * g2_ref[...]
    hx = hx * (1.0 + mod_ref[4:5, :]) + mod_ref[3:4, :]
    hx_ref[...] = hx.reshape(hx_ref.shape)
    lg_ref[...] = jnp.dot(hx.astype(BF16), rw_ref[...], preferred_element_type=F32) + rb_ref[...]


def _mix_out(xt, px, col_z, a, h2, y_conv, y_pool, modtab, norm_g, skip, w_m_bf, w_out_bf, g2, router_w, router_b,
             ntb, mod_base):
    t, d = xt.shape
    ne = router_w.shape[1]
    row = _mod_row(ntb, mod_base)
    vec = lambda: pl.BlockSpec((1, d), lambda i: (0, 0))
    return pl.pallas_call(
        _mix_out_kernel,
        grid=(t // TILE,),
        in_specs=[
            pl.BlockSpec((TILE, d), lambda i: (i, 0)),
            pl.BlockSpec((TILE, 3 * d), lambda i: (i, 0)),
            pl.BlockSpec((TILE, d), lambda i: (i, col_z)),
            pl.BlockSpec((TILE, d), lambda i: (i, 0)),
            pl.BlockSpec((2, TILE, d), lambda i: (0, i, 0)),
            pl.BlockSpec((TILE, d), lambda i: (i, 0)),
            pl.BlockSpec((TILE, d), lambda i: (i, 0)),
            pl.BlockSpec((None, 6, d), lambda i: (row(i), 0, 0)),
            vec(), vec(),
            pl.BlockSpec((d, d), lambda i: (0, 0)),
            pl.BlockSpec((d, d), lambda i: (0, 0)),
            vec(),
            pl.BlockSpec((d, ne), lambda i: (0, 0)),
            pl.BlockSpec((1, ne), lambda i: (0, 0)),
        ],
        out_specs=[
            pl.BlockSpec((TILE, d), lambda i: (i, 0)),
            pl.BlockSpec((TILE, SUBLANES, d // SUBLANES), lambda i: (i, 0, 0)),
            pl.BlockSpec((TILE, ne), lambda i: (i, 0)),
        ],
        out_shape=[
            jax.ShapeDtypeStruct((t, d), F32),
            jax.ShapeDtypeStruct((t, SUBLANES, d // SUBLANES), F32),
            jax.ShapeDtypeStruct((t, ne), F32),
        ],
        compiler_params=_cparams(("arbitrary",)),
        name="mix_out",
    )(xt, px, px, a, h2, y_conv, y_pool, modtab, norm_g, skip, w_m_bf, w_out_bf, g2, router_w, router_b)


def _expert_kernel(be_ref, nu_ref, tok_ref, tokn_ref, hx_hbm, wgu_ref, bgu_ref, wdn_ref, bdn_ref, y_ref,
                   xbuf_a, xbuf_b, sem_a, sem_b, fsem, xb_s, wgu_s, wdn_s):
    i = pl.program_id(0)
    n_used = nu_ref[0]
    half = xbuf_a.shape[0]
    dff = wdn_s.shape[0]
    n_chunk = dff // MOE_CHUNK
    rows_per_chunk = half // n_chunk

    def row_copy(tok_smem, r, xbuf, sem):
        t = tok_smem[0, 0, r]
        return pltpu.make_async_copy(hx_hbm.at[t], xbuf.at[r % half], sem.at[0])

    def wait_rows(xbuf, sem):
        pltpu.make_async_copy(hx_hbm.at[pl.ds(0, half)], xbuf, sem.at[0]).wait()

    @pl.when(i == 0)
    def _():
        def body(r, carry):
            row_copy(tok_ref, r, xbuf_a, sem_a).start()
            return carry
        lax.fori_loop(0, half, body, 0)

    e_now = be_ref[i]
    e_prev = be_ref[jnp.maximum(i - 1, 0)]

    @pl.when(jnp.logical_or(i == 0, e_now != e_prev))
    def _():
        wgu_s[...] = wgu_ref[...].astype(BF16)
        wdn_s[...] = wdn_ref[...].astype(BF16)

    def half_block(x_cur, sem_cur, x_nxt, sem_nxt, tok_nxt, row0_nxt, out_rows, fence):
        wait_rows(x_cur, sem_cur)
        xb_s[...] = x_cur[...].reshape(xb_s.shape).astype(BF16)
        y = None
        for c in range(n_chunk):
            if c < n_chunk // 2:
                for r in range(2 * c * rows_per_chunk, 2 * (c + 1) * rows_per_chunk):
                    row_copy(tok_nxt, row0_nxt + r, x_nxt, sem_nxt).start(priority=r % 2)
            if fence and c == n_chunk // 2:
                pl.semaphore_signal(fsem, 1)
                pl.semaphore_wait(fsem, 1)
            cg = slice(c * MOE_CHUNK, (c + 1) * MOE_CHUNK)
            cu = slice(dff + c * MOE_CHUNK, dff + (c + 1) * MOE_CHUNK)
            g = jnp.dot(xb_s[...], wgu_s[:, cg], preferred_element_type=F32) + bgu_ref[:, cg]
            u = jnp.dot(xb_s[...], wgu_s[:, cu], preferred_element_type=F32) + bgu_ref[:, cu]
            g = jnp.minimum(g, SWIGLU_LIMIT)
            u = jnp.clip(u, -SWIGLU_LIMIT, SWIGLU_LIMIT)
            act = (u + 1.0) * (g * _sigmoid(SWIGLU_ALPHA * g))
            part = jnp.dot(act.astype(BF16), wdn_s[cg, :], preferred_element_type=F32)
            y = part if y is None else y + part
        y_ref[out_rows] = (y + bdn_ref[...]).reshape((half,) + y_ref.shape[1:])

    used = i < n_used

    @pl.when(used)
    def _():
        half_block(xbuf_a, sem_a, xbuf_b, sem_b, tok_ref, half, slice(0, half), False)
        half_block(xbuf_b, sem_b, xbuf_a, sem_a, tokn_ref, 0, slice(half, 2 * half), True)

    @pl.when(jnp.logical_not(used))
    def _():
        @pl.when(i == n_used)
        def _():
            wait_rows(xbuf_a, sem_a)

        y_ref[...] = jnp.zeros(y_ref.shape, y_ref.dtype)


def _experts(layer, hx, row_tok, block_e, n_used, w_gu, b_gu, w_dn, b_dn):
    t, sub, dl = hx.shape
    d = sub * dl
    depth, ne, _, dff2 = w_gu.shape
    dff = dff2 // 2
    nb = block_e.shape[0]
    bm = MOE_BM
    half = bm // 2
    tok3 = row_tok.reshape(nb, 1, bm)
    grid_spec = pltpu.PrefetchScalarGridSpec(
        num_scalar_prefetch=2,
        grid=(nb,),
        in_specs=[
            pl.BlockSpec((1, 1, bm), lambda i, be, nu: (i, 0, 0), memory_space=pltpu.SMEM),
            pl.BlockSpec((1, 1, bm), lambda i, be, nu: (jnp.minimum(i + 1, nb - 1), 0, 0),
                         memory_space=pltpu.SMEM),
            pl.BlockSpec(memory_space=pl.ANY),
            pl.BlockSpec((None, None, d, dff2), lambda i, be, nu: (layer, be[i], 0, 0)),
            pl.BlockSpec((None, None, 1, dff2), lambda i, be, nu: (layer, be[i], 0, 0)),
            pl.BlockSpec((None, None, dff, d), lambda i, be, nu: (layer, be[i], 0, 0)),
            pl.BlockSpec((None, None, 1, d), lambda i, be, nu: (layer, be[i], 0, 0)),
        ],
        out_specs=pl.BlockSpec((bm, sub, dl), lambda i, be, nu: (i, 0, 0)),
        scratch_shapes=[
            pltpu.VMEM((half, sub, dl), F32),
            pltpu.VMEM((half, sub, dl), F32),
            pltpu.SemaphoreType.DMA((1,)),
            pltpu.SemaphoreType.DMA((1,)),
            pltpu.SemaphoreType.REGULAR,
            pltpu.VMEM((half, d), BF16),
            pltpu.VMEM((d, dff2), BF16),
            pltpu.VMEM((dff, d), BF16),
        ],
    )
    return pl.pallas_call(
        _expert_kernel,
        grid_spec=grid_spec,
        out_shape=jax.ShapeDtypeStruct((nb * bm, sub, dl), F32),
        compiler_params=_cparams(("arbitrary",), no_bounds_checks=True),
        name="moe_experts",
    )(block_e, n_used, tok3, tok3, hx, w_gu, b_gu.reshape(depth, ne, 1, dff2), w_dn,
      b_dn.reshape(depth, ne, 1, d))


def _combine_kernel(dst_ref, dstn_ref, ys_hbm, gw_ref, x_ref, mod_ref, o_ref, buf, sem):
    i = pl.program_id(0)
    nsteps = pl.num_programs(0)
    slot = i % 2
    n_copy = TOP_K * TILE

    def row_copy(dst_smem, e, sl):
        src = dst_smem[0, 0, e]
        kk = e // TILE
        r = e % TILE
        return pltpu.make_async_copy(ys_hbm.at[src], buf.at[sl, kk, r], sem.at[sl])

    @pl.when(i == 0)
    def _():
        def body(e, carry):
            row_copy(dst_ref, e, 0).start()
            return carry
        lax.fori_loop(0, n_copy, body, 0)

    @pl.when(i + 1 < nsteps)
    def _():
        for e in range(n_copy):
            row_copy(dstn_ref, e, 1 - slot).start(priority=e % 2)

    for kk in range(TOP_K):
        pltpu.make_async_copy(ys_hbm.at[pl.ds(0, TILE)], buf.at[slot, kk], sem.at[slot]).wait()

    gw = gw_ref[...]
    y = gw[:, 0:1] * buf[slot, 0].reshape(x_ref.shape)
    for kk in range(1, TOP_K):
        y = y + gw[:, kk:kk + 1] * buf[slot, kk].reshape(x_ref.shape)
    o_ref[...] = x_ref[...] + mod_ref[5:6, :] * y


def _combine(ys, dest_tiles, gate_w, xt, modtab, ntb, mod_base):
    t, d = xt.shape
    nt = t // TILE
    row = _mod_row(ntb, mod_base)
    return pl.pallas_call(
        _combine_kernel,
        grid=(nt,),
        in_specs=[
            pl.BlockSpec((1, 1, TOP_K * TILE), lambda i: (i, 0, 0), memory_space=pltpu.SMEM),
            pl.BlockSpec((1, 1, TOP_K * TILE), lambda i: (jnp.minimum(i + 1, nt - 1), 0, 0),
                         memory_space=pltpu.SMEM),
            pl.BlockSpec(memory_space=pl.ANY),
            pl.BlockSpec((TILE, TOP_K), lambda i: (i, 0)),
            pl.BlockSpec((TILE, d), lambda i: (i, 0)),
            pl.BlockSpec((None, 6, d), lambda i: (row(i), 0, 0)),
        ],
        out_specs=pl.BlockSpec((TILE, d), lambda i: (i, 0)),
        out_shape=jax.ShapeDtypeStruct((t, d), F32),
        scratch_shapes=[
            pltpu.VMEM((2, TOP_K, TILE) + ys.shape[1:], F32),
            pltpu.SemaphoreType.DMA((2,)),
        ],
        compiler_params=_cparams(("arbitrary",), no_bounds_checks=True),
        name="moe_combine",
    )(dest_tiles, dest_tiles, ys, gate_w, xt, modtab)


IN_CHUNKS = 6
IN_DMA_CHUNKS = 4


def _comb_in_kernel(dst_ref, dstn_ref, ys_hbm, gw_ref, x_ref, modp_ref, g_ref, mod_ref, w_ref, xo_ref, o_ref,
                    buf_a, buf_b, sem_a, sem_b, fsem, hb_s):
    i = pl.program_id(0)
    nsteps = pl.num_programs(0)
    n_copy = TOP_K * TILE
    n = w_ref.shape[1]
    cw = n // IN_CHUNKS
    per = n_copy // IN_DMA_CHUNKS

    def row_copy(dst_smem, e, buf, sem):
        src = dst_smem[0, 0, e]
        return pltpu.make_async_copy(ys_hbm.at[src], buf.at[e // TILE, e % TILE], sem.at[0])

    def wait_rows(buf, sem):
        for kk in range(TOP_K):
            pltpu.make_async_copy(ys_hbm.at[pl.ds(0, TILE)], buf.at[kk], sem.at[0]).wait()

    @pl.when(i == 0)
    def _():
        def body(e, carry):
            row_copy(dst_ref, e, buf_a, sem_a).start()
            return carry
        lax.fori_loop(0, n_copy, body, 0)

    def tile(buf_cur, sem_cur, buf_nxt, sem_nxt):
        wait_rows(buf_cur, sem_cur)
        gw = gw_ref[...]
        y = gw[:, 0:1] * buf_cur[0].reshape(x_ref.shape)
        for kk in range(1, TOP_K):
            y = y + gw[:, kk:kk + 1] * buf_cur[kk].reshape(x_ref.shape)
        x = x_ref[...] + modp_ref[5:6, :] * y
        xo_ref[...] = x
        ms = jnp.mean(x * x, axis=-1, keepdims=True)
        h = x * lax.rsqrt(ms + RMS_EPS) * g_ref[...]
        h = h * (1.0 + mod_ref[1:2, :]) + mod_ref[0:1, :]
        hb_s[...] = h.astype(BF16)
        for c in range(IN_CHUNKS):
            if c < IN_DMA_CHUNKS:
                for e in range(c * per, (c + 1) * per):
                    row_copy(dstn_ref, e, buf_nxt, sem_nxt).start(priority=e % 2)
            if c == IN_DMA_CHUNKS:
                pl.semaphore_signal(fsem, 1)
                pl.semaphore_wait(fsem, 1)
            cs = slice(c * cw, (c + 1) * cw)
            o_ref[:, cs] = jnp.dot(hb_s[...], w_ref[:, cs], preferred_element_type=F32).astype(o_ref.dtype)

    even = i % 2 == 0

    @pl.when(even)
    def _():
        tile(buf_a, sem_a, buf_b, sem_b)

    @pl.when(jnp.logical_not(even))
    def _():
        tile(buf_b, sem_b, buf_a, sem_a)

    @pl.when(jnp.logical_and(i == nsteps - 1, even))
    def _():
        wait_rows(buf_b, sem_b)

    @pl.when(jnp.logical_and(i == nsteps - 1, jnp.logical_not(even)))
    def _():
        wait_rows(buf_a, sem_a)


def _comb_in_proj(ys, dest_tiles, gate_w, xt, g1, modtab, w_in_bf, ntb, mod_base_prev, mod_base):
    t, d = xt.shape
    n = w_in_bf.shape[1]
    nt = t // TILE
    assert n % (IN_CHUNKS * 256) == 0 and (TOP_K * TILE) % IN_DMA_CHUNKS == 0
    row_prev = _mod_row(ntb, mod_base_prev)
    row = _mod_row(ntb, mod_base)
    return pl.pallas_call(
        _comb_in_kernel,
        grid=(nt,),
        in_specs=[
            pl.BlockSpec((1, 1, TOP_K * TILE), lambda i: (i, 0, 0), memory_space=pltpu.SMEM),
            pl.BlockSpec((1, 1, TOP_K * TILE), lambda i: (jnp.minimum(i + 1, nt - 1), 0, 0),
                         memory_space=pltpu.SMEM),
            pl.BlockSpec(memory_space=pl.ANY),
            pl.BlockSpec((TILE, TOP_K), lambda i: (i, 0)),
            pl.BlockSpec((TILE, d), lambda i: (i, 0)),
            pl.BlockSpec((None, 6, d), lambda i: (row_prev(i), 0, 0)),
            pl.BlockSpec((1, d), lambda i: (0, 0)),
            pl.BlockSpec((None, 6, d), lambda i: (row(i), 0, 0)),
            pl.BlockSpec((d, n), lambda i: (0, 0), pipeline_mode=pl.Buffered(1)),
        ],
        out_specs=[
            pl.BlockSpec((TILE, d), lambda i: (i, 0)),
            pl.BlockSpec((TILE, n), lambda i: (i, 0)),
        ],
        out_shape=[
            jax.ShapeDtypeStruct((t, d), F32),
            jax.ShapeDtypeStruct((t, n), BF16),
        ],
        scratch_shapes=[
            pltpu.VMEM((TOP_K, TILE) + ys.shape[1:], F32),
            pltpu.VMEM((TOP_K, TILE) + ys.shape[1:], F32),
            pltpu.SemaphoreType.DMA((1,)),
            pltpu.SemaphoreType.DMA((1,)),
            pltpu.SemaphoreType.REGULAR,
            pltpu.VMEM((TILE, d), BF16),
        ],
        compiler_params=_cparams(("arbitrary",), no_bounds_checks=True),
        name="comb_in_proj",
    )(dest_tiles, dest_tiles, ys, gate_w, xt, modtab, g1, modtab, w_in_bf)


def _final_kernel(x_ref, g_ref, o_ref):
    x = x_ref[...]
    ms = jnp.mean(x * x, axis=-1, keepdims=True)
    o_ref[...] = x * lax.rsqrt(ms + RMS_EPS) * g_ref[...]


def _final_norm(xt, final_g, bsz, ntb):
    t, d = xt.shape
    nlat = ntb - 1
    return pl.pallas_call(
        _final_kernel,
        grid=(bsz, nlat),
        in_specs=[
            pl.BlockSpec((TILE, d), lambda b, j: (b * ntb + 1 + j, 0)),
            pl.BlockSpec((1, d), lambda b, j: (0, 0)),
        ],
        out_specs=pl.BlockSpec((TILE, d), lambda b, j: (b * nlat + j, 0)),
        out_shape=jax.ShapeDtypeStruct((bsz * nlat * TILE, d), F32),
        compiler_params=_cparams(("arbitrary", "arbitrary")),
        name="final_norm",
    )(xt, final_g)


def _route_kernel(lg_ref, te_ref, gw_ref, pos_ref, cnt_ref, cnt_s):
    i = pl.program_id(0)

    @pl.when(i == 0)
    def _():
        cnt_s[...] = jnp.zeros(cnt_s.shape, F32)

    lg = lg_ref[...]
    ne = lg.shape[1]
    lane = lax.broadcasted_iota(jnp.int32, lg.shape, 1)
    work = lg
    vals, idxs, hots = [], [], []
    for _ in range(TOP_K):
        m = jnp.max(work, axis=1, keepdims=True)
        idx = jnp.min(jnp.where(work == m, lane, ne), axis=1, keepdims=True)
        hot = lane == idx
        vals.append(m)
        idxs.append(idx)
        hots.append(jnp.where(hot, 1.0, 0.0))
        work = jnp.where(hot, -jnp.inf, work)
    exps = [jnp.exp(v - vals[0]) for v in vals]
    ssum = exps[0]
    for e in exps[1:]:
        ssum = ssum + e
    hot_all = hots[0]
    for h in hots[1:]:
        hot_all = hot_all + h
    r2 = lax.broadcasted_iota(jnp.int32, (TILE, TILE), 0)
    c2 = lax.broadcasted_iota(jnp.int32, (TILE, TILE), 1)
    earlier = jnp.where(r2 > c2, 1.0, 0.0).astype(BF16)
    before = jnp.dot(earlier, hot_all.astype(BF16), preferred_element_type=F32) + cnt_s[...]
    for kk in range(TOP_K):
        te_ref[:, kk:kk + 1] = idxs[kk]
        gw_ref[:, kk:kk + 1] = exps[kk] / ssum
        pos_ref[:, kk:kk + 1] = jnp.sum(hots[kk] * before, axis=1, keepdims=True).astype(jnp.int32)
    cnt_s[...] = cnt_s[...] + jnp.sum(hot_all, axis=0, keepdims=True)
    cnt_ref[...] = cnt_s[...]


def _route(logits):
    t, ne = logits.shape
    return pl.pallas_call(
        _route_kernel,
        grid=(t // TILE,),
        in_specs=[pl.BlockSpec((TILE, ne), lambda i: (i, 0))],
        out_specs=[
            pl.BlockSpec((TILE, TOP_K), lambda i: (i, 0)),
            pl.BlockSpec((TILE, TOP_K), lambda i: (i, 0)),
            pl.BlockSpec((TILE, TOP_K), lambda i: (i, 0)),
            pl.BlockSpec((1, ne), lambda i: (0, 0)),
        ],
        out_shape=[
            jax.ShapeDtypeStruct((t, TOP_K), jnp.int32),
            jax.ShapeDtypeStruct((t, TOP_K), F32),
            jax.ShapeDtypeStruct((t, TOP_K), jnp.int32),
            jax.ShapeDtypeStruct((1, ne), F32),
        ],
        scratch_shapes=[pltpu.VMEM((1, ne), F32)],
        compiler_params=_cparams(("arbitrary",)),
        name="route",
    )(logits)


def _routing(logits, n_tiles):
    t = logits.shape[0]
    top_e, gate, pos, cnt = _route(logits)
    n_assign = t * TOP_K
    experts = jnp.arange(N_EXPERTS, dtype=jnp.int32)
    flat_e = top_e.reshape(-1)
    order = jnp.argsort(flat_e, stable=True).astype(jnp.int32)
    counts = cnt.reshape(N_EXPERTS).astype(jnp.int32)
    grp_start = jnp.cumsum(counts) - counts
    padded = (counts + MOE_BM - 1) // MOE_BM * MOE_BM
    pad_end = jnp.cumsum(padded)
    pad_start = pad_end - padded
    dest = pos + jnp.sum(jnp.where(top_e[:, :, None] == experts[None, None, :], pad_start[None, None, :], 0), axis=2)
    n_blocks = n_assign // MOE_BM + N_EXPERTS
    blk = jnp.arange(n_blocks, dtype=jnp.int32)
    block_e = jnp.minimum(jnp.sum((pad_end[None, :] <= (blk * MOE_BM)[:, None]).astype(jnp.int32), axis=1),
                          N_EXPERTS - 1)
    src0 = grp_start[block_e] + blk * MOE_BM - pad_start[block_e]
    lim = grp_start[block_e] + counts[block_e]
    src = src0[:, None] + jnp.arange(MOE_BM, dtype=jnp.int32)[None, :]
    valid = src < lim[:, None]
    row_tok = jnp.where(valid, order[jnp.clip(src, 0, n_assign - 1)] // TOP_K, 0).astype(jnp.int32).reshape(-1)
    n_used = (pad_end[-1] // MOE_BM).astype(jnp.int32).reshape(1)
    dest_tiles = dest.reshape(n_tiles, TILE, TOP_K).transpose(0, 2, 1).reshape(n_tiles, 1, TOP_K * TILE)
    return gate, row_tok, block_e, n_used, dest_tiles


def _block_diag(w, heads):
    nb, c, _ = w.shape
    per = nb // heads
    w = w.reshape(heads, per, c, c)
    eye = jnp.eye(per, dtype=w.dtype)
    full = eye[None, :, None, :, None] * w[:, :, :, None, :]
    return full.reshape(heads, per * c, per * c)


def kernel(x, c, ctx, c_ctx, w_ada, b_ada, norm1_g, norm2_g, w_in, conv_dw_w, conv_dw_b, conv_ln_g, conv_ln_b, w_conv_out, m_sc_w, m_sc_b, m_wq, m_wk, m_w_if, m_b_if, m_norm_g, m_skip, w_m_out, pool_w, pool_scale, w_pool_out, w_out, router_w, router_b, w_gu, b_gu, w_dn, b_dn, final_g):
    bsz, seq, d = x.shape
    ctx_len = ctx.shape[1]
    depth = w_ada.shape[0]
    assert ctx_len == TILE and seq % TILE == 0 and d % (128 * HEADS) == 0
    nt = ctx_len + seq
    ntb = nt // TILE
    t = bsz * nt
    n_tiles = t // TILE
    d_conv = w_conv_out.shape[1]
    d_pool = w_pool_out.shape[1]
    d_m = w_m_out.shape[1]
    assert d_m == d and 2 * d_conv == d and 2 * d_pool == d
    c_conv = 3 * d
    c_pool = c_conv + 2 * d_conv
    c_mz = c_pool + d_pool
    c_mqk = c_mz + d_m
    c_mv = c_mqk + d_m
    d_in = c_mv + d_m
    col_conv, col_z, col_u, col_v = 3, 4, 5, 6
    col_pool = (7 * d) // d_pool

    xt = jnp.concatenate([ctx, x], axis=1).reshape(t, d)

    mp = -(-(bsz + 1) // 8) * 8
    c_all = jnp.zeros((mp, d), F32).at[:bsz].set(c).at[bsz].set(c_ctx)
    mod = _ada(c_all, w_ada, b_ada)
    mod_x = mod[:, :bsz].reshape(depth, bsz, 1, 6, d)
    mod_c = jnp.broadcast_to(mod[:, bsz].reshape(depth, 1, 1, 6, d), (depth, bsz, 1, 6, d))
    modtab = jnp.concatenate([mod_c, mod_x], axis=2).reshape(depth * bsz * 2, 6, d)

    gh = 2 * HEADS
    for l in range(depth):
        mod_base = l * bsz * 2
        w_in_bf = jnp.concatenate([w_in[l, :, 0:c_pool], w_in[l, :, c_mz:d_in], w_in[l, :, c_pool:c_mz]],
                                  axis=1).astype(BF16)
        if l == 0:
            px = _in_proj(xt, norm1_g[l].reshape(1, d), modtab, w_in_bf, ntb, mod_base)
        else:
            xt, px = _comb_in_proj(ys, dest_tiles, gate_w, xt, norm1_g[l].reshape(1, d), modtab, w_in_bf, ntb,
                                   mod_base - bsz * 2, mod_base)

        wq_bd = _block_diag(m_wq[l], HEADS).astype(BF16)
        wk_bd = _block_diag(m_wk[l], HEADS).astype(BF16)
        wif = jnp.concatenate([m_w_if[l, 0], m_w_if[l, 1]], axis=1)
        bif = jnp.concatenate([m_b_if[l, 0], m_b_if[l, 1]], axis=0)
        a, q, k, gates, gates_t = _mlstm_pre(
            px, col_u, col_v, m_sc_w[l], m_sc_b[l].reshape(1, d), wq_bd, wk_bd,
            wif.astype(BF16), wif.T.astype(BF16), bif.reshape(1, 2 * gh), bif.reshape(2 * gh, 1), bsz, ntb)
        h2 = _mlstm_scan(q, k, px, col_v, gates, gates_t, bsz, ntb)

        y_conv = _conv_module(px, col_conv, conv_dw_w[l], conv_dw_b[l].reshape(1, d_conv),
                              conv_ln_g[l].reshape(1, d_conv), conv_ln_b[l].reshape(1, d_conv),
                              w_conv_out[l].astype(BF16), bsz, ntb)
        y_pool = _pool_module(px, col_pool, pool_w[l].astype(BF16), pool_scale[l].reshape(1, d_pool),
                              w_pool_out[l].astype(BF16), bsz, nt, ctx_len)

        xt, hx, logits = _mix_out(
            xt, px, col_z, a, h2, y_conv, y_pool, modtab, m_norm_g[l].reshape(1, d), m_skip[l].reshape(1, d),
            w_m_out[l].astype(BF16), w_out[l].astype(BF16), norm2_g[l].reshape(1, d), router_w[l].astype(BF16),
            router_b[l].reshape(1, N_EXPERTS), ntb, mod_base)

        gate_w, row_tok, block_e, n_used, dest_tiles = _routing(logits, n_tiles)
        ys = _experts(l, hx, row_tok, block_e, n_used, w_gu, b_gu, w_dn, b_dn)

    xt = _combine(ys, dest_tiles, gate_w, xt, modtab, ntb, (depth - 1) * bsz * 2)
    out = _final_norm(xt, final_g.reshape(1, d), bsz, ntb)
    return out.reshape(bsz, seq, d)
```

```python
import functools

import jax
import jax.numpy as jnp
from jax import lax
from jax.experimental import pallas as pl
from jax.experimental.pallas import tpu as pltpu

F32 = jnp.float32
BF16 = jnp.bfloat16
HIGHEST = lax.Precision.HIGHEST

TILE = 256
GRID_W = 64
CONV_WIDTH = 31
SUBLANES = 8
CONV_HALO = 16
SHORT_CONV = 4
HEADS = 4
POOL_WINDOWS = (2, 4, 8, 16)
N_EXPERTS = 32
TOP_K = 4
MOE_BM = 512
MOE_CHUNK = 256
SWIGLU_LIMIT = 7.0
SWIGLU_ALPHA = 1.702
RMS_EPS = 1e-6
LN_EPS = 1e-5
NEG = -1e30
VMEM_LIMIT = 56 * 1024 * 1024


def _cparams(sem, no_bounds_checks=False):
    return pltpu.CompilerParams(dimension_semantics=sem, vmem_limit_bytes=VMEM_LIMIT,
                                disable_bounds_checks=no_bounds_checks)


def _sigmoid(x):
    return jax.nn.sigmoid(x)


def _log_sigmoid(x):
    return jnp.minimum(x, 0.0) - jnp.log1p(jnp.exp(-jnp.abs(x)))


def _ada_kernel(c_ref, w_ref, b_ref, o_ref):
    s = c_ref[...]
    s = s * _sigmoid(s)
    o_ref[...] = jnp.dot(s, w_ref[...], precision=HIGHEST, preferred_element_type=F32) + b_ref[...]


def _ada(c_all, w_ada, b_ada):
    depth, d, n = w_ada.shape
    mp = c_all.shape[0]
    tn = 1536
    return pl.pallas_call(
        _ada_kernel,
        grid=(depth, n // tn),
        in_specs=[
            pl.BlockSpec((mp, d), lambda l, j: (0, 0)),
            pl.BlockSpec((None, d, tn), lambda l, j: (l, 0, j)),
            pl.BlockSpec((None, 1, tn), lambda l, j: (l, 0, j)),
        ],
        out_specs=pl.BlockSpec((None, mp, tn), lambda l, j: (l, 0, j)),
        out_shape=jax.ShapeDtypeStruct((depth, mp, n), F32),
        compiler_params=_cparams(("arbitrary", "arbitrary")),
        name="ada",
    )(c_all, w_ada, b_ada.reshape(depth, 1, n))


def _in_kernel(x_ref, g_ref, mod_ref, w_ref, o_ref):
    x = x_ref[...]
    ms = jnp.mean(x * x, axis=-1, keepdims=True)
    y = x * lax.rsqrt(ms + RMS_EPS) * g_ref[...]
    h = y * (1.0 + mod_ref[1:2, :]) + mod_ref[0:1, :]
    o_ref[...] = jnp.dot(h.astype(BF16), w_ref[...], preferred_element_type=F32).astype(BF16)


def _mod_row(ntb, base):
    def f(i):
        return base + 2 * (i // ntb) + jnp.minimum(i % ntb, 1)
    return f


def _in_proj(xt, g1, modtab, w_in_bf, ntb, mod_base):
    t, d = xt.shape
    n = w_in_bf.shape[1]
    row = _mod_row(ntb, mod_base)
    return pl.pallas_call(
        _in_kernel,
        grid=(t // TILE,),
        in_specs=[
            pl.BlockSpec((TILE, d), lambda i: (i, 0)),
            pl.BlockSpec((1, d), lambda i: (0, 0)),
            pl.BlockSpec((None, 6, d), lambda i: (row(i), 0, 0)),
            pl.BlockSpec((d, n), lambda i: (0, 0), pipeline_mode=pl.Buffered(1)),
        ],
        out_specs=pl.BlockSpec((TILE, n), lambda i: (i, 0)),
        out_shape=jax.ShapeDtypeStruct((t, n), BF16),
        compiler_params=_cparams(("arbitrary",)),
        name="in_proj",
    )(xt, g1, modtab, w_in_bf)


def _mpre_kernel(ntb, u_ref, up_ref, un_ref, v_ref, scw_ref, scb_ref, wq_ref, wk_ref, wif_ref, wift_ref,
                 bif_ref, bift_ref, a_ref, q_ref, k_ref, g_ref, gt_ref):
    i = pl.program_id(1)
    d = u_ref.shape[1]
    dh = d // HEADS
    u = u_ref[...].astype(F32)
    up = jnp.where(i >= 2, up_ref[...].astype(F32)[8:16, :], 0.0)
    un = jnp.where(jnp.logical_and(i >= 1, i <= ntb - 2), un_ref[...].astype(F32)[0:8, :], 0.0)
    ext = jnp.concatenate([up, u, un], axis=0)
    n_ext = TILE + 16
    conv = scb_ref[...]
    for j in range(SHORT_CONV):
        sh = (SHORT_CONV // 2 - j) % n_ext
        shifted = ext if sh == 0 else pltpu.roll(ext, sh, 0)
        conv = conv + scw_ref[j:j + 1, :] * shifted[8:8 + TILE, :]
    a = conv * _sigmoid(conv)
    ab = a.astype(BF16)
    a_ref[...] = ab
    qs, ks = [], []
    for hb in range(HEADS):
        blk = ab[:, hb * dh:(hb + 1) * dh]
        qs.append(jnp.dot(blk, wq_ref[hb], preferred_element_type=F32))
        ks.append(jnp.dot(blk, wk_ref[hb], preferred_element_type=F32))
    q = jnp.concatenate(qs, axis=1)
    k = jnp.concatenate(ks, axis=1)
    qb = q.astype(BF16)
    kb = k.astype(BF16)
    vb = v_ref[...]
    q_ref[...] = (q * (dh ** -0.5)).astype(BF16)
    k_ref[...] = kb
    pre = bif_ref[...]
    pret = bift_ref[...]
    nt_dims = (((1,), (1,)), ((), ()))
    for part, f in enumerate((qb, kb, vb)):
        pre = pre + jnp.dot(f, wif_ref[part * d:(part + 1) * d, :], preferred_element_type=F32)
        pret = pret + lax.dot_general(wift_ref[:, part * d:(part + 1) * d], f, nt_dims,
                                      preferred_element_type=F32)
    col = lax.broadcasted_iota(jnp.int32, pre.shape, 1)
    pre = jnp.where(col % (2 * HEADS) >= HEADS, _log_sigmoid(pre), pre)
    rowi = lax.broadcasted_iota(jnp.int32, pret.shape, 0)
    pret = jnp.where(rowi % (2 * HEADS) >= HEADS, _log_sigmoid(pret), pret)
    g_ref[0] = pre[:, 0:2 * HEADS]
    g_ref[1] = pre[:, 2 * HEADS:4 * HEADS]
    gt_ref[0] = pret[0:2 * HEADS, :]
    gt_ref[1] = pret[2 * HEADS:4 * HEADS, :]


def _mlstm_pre(px, col_u, col_v, sc_w, sc_b, wq_bd, wk_bd, wif, wift, bif, bift, bsz, ntb):
    t = px.shape[0]
    d = sc_w.shape[1]
    hrows = 16
    nh = t // hrows
    per = TILE // hrows

    def tile(b, i):
        return b * ntb + i

    gh = 2 * HEADS
    return pl.pallas_call(
        functools.partial(_mpre_kernel, ntb),
        grid=(bsz, ntb),
        in_specs=[
            pl.BlockSpec((TILE, d), lambda b, i: (tile(b, i), col_u)),
            pl.BlockSpec((hrows, d), lambda b, i: (jnp.maximum(tile(b, i) * per - 1, 0), col_u)),
            pl.BlockSpec((hrows, d), lambda b, i: (jnp.minimum((tile(b, i) + 1) * per, nh - 1), col_u)),
            pl.BlockSpec((TILE, d), lambda b, i: (tile(b, i), col_v)),
            pl.BlockSpec((SHORT_CONV, d), lambda b, i: (0, 0)),
            pl.BlockSpec((1, d), lambda b, i: (0, 0)),
            pl.BlockSpec((HEADS, d // HEADS, d // HEADS), lambda b, i: (0, 0, 0)),
            pl.BlockSpec((HEADS, d // HEADS, d // HEADS), lambda b, i: (0, 0, 0)),
            pl.BlockSpec((3 * d, 2 * gh), lambda b, i: (0, 0)),
            pl.BlockSpec((2 * gh, 3 * d), lambda b, i: (0, 0)),
            pl.BlockSpec((1, 2 * gh), lambda b, i: (0, 0)),
            pl.BlockSpec((2 * gh, 1), lambda b, i: (0, 0)),
        ],
        out_specs=[
            pl.BlockSpec((TILE, d), lambda b, i: (tile(b, i), 0)),
            pl.BlockSpec((TILE, d), lambda b, i: (tile(b, i), 0)),
            pl.BlockSpec((TILE, d), lambda b, i: (tile(b, i), 0)),
            pl.BlockSpec((2, TILE, gh), lambda b, i: (0, tile(b, i), 0)),
            pl.BlockSpec((2, gh, TILE), lambda b, i: (0, 0, tile(b, i))),
        ],
        out_shape=[
            jax.ShapeDtypeStruct((t, d), BF16),
            jax.ShapeDtypeStruct((t, d), BF16),
            jax.ShapeDtypeStruct((t, d), BF16),
            jax.ShapeDtypeStruct((2, t, gh), F32),
            jax.ShapeDtypeStruct((2, gh, t), F32),
        ],
        compiler_params=_cparams(("arbitrary", "arbitrary")),
        name="mlstm_pre",
    )(px, px, px, px, sc_w, sc_b, wq_bd, wk_bd, wif, wift, bif, bift)


AUG = 128


def _scan_kernel(q_ref, k_ref, v_ref, g_ref, gt_ref, h_ref, cta_s, m_s):
    dr = pl.program_id(1)
    i = pl.program_id(2)
    dh = q_ref.shape[1] // HEADS

    @pl.when(i == 0)
    def _():
        cta_s[...] = jnp.zeros(cta_s.shape, F32)
        m_s[...] = jnp.zeros(m_s.shape, F32)

    x = g_ref[...]
    xt = gt_ref[...]
    rid = lax.broadcasted_iota(jnp.int32, x.shape, 0)
    cid = lax.broadcasted_iota(jnp.int32, xt.shape, 1)
    p = x
    pt = xt
    s = 1
    while s < TILE:
        p = p + jnp.where(rid >= s, pltpu.roll(p, s, 0), 0.0)
        pt = pt + jnp.where(cid >= s, pltpu.roll(pt, s, 1), 0.0)
        s *= 2
    tot = p[TILE - 1:TILE, :]
    tott = pt[:, TILE - 1:TILE]
    fwd = dr == 0
    bc = jnp.where(fwd, p, tot - p + x)
    br = jnp.where(fwd, pt, tott - pt + xt)
    amat = x[:, 0:HEADS] - bc[:, HEADS:2 * HEADS]
    rida = lax.broadcasted_iota(jnp.int32, amat.shape, 0)
    cpre = amat
    csuf = amat
    s = 1
    while s < TILE:
        cpre = jnp.maximum(cpre, jnp.where(rida >= s, pltpu.roll(cpre, s, 0), NEG))
        csuf = jnp.maximum(csuf, jnp.where(rida < TILE - s, pltpu.roll(csuf, TILE - s, 0), NEG))
        s *= 2
    cmax = jnp.where(fwd, cpre, csuf)
    r2 = lax.broadcasted_iota(jnp.int32, (TILE, TILE), 0)
    c2 = lax.broadcasted_iota(jnp.int32, (TILE, TILE), 1)
    tri = (r2 - c2) * (1 - 2 * dr) >= 0
    lane = lax.broadcasted_iota(jnp.int32, (TILE, AUG), 1)
    one_col = jnp.where(lane == 0, 1.0, 0.0)
    nt_dims = (((1,), (1,)), ((), ()))
    tn_dims = (((0,), (0,)), ((), ()))
    for hh in range(HEADS):
        sl = slice(hh * dh, (hh + 1) * dh)
        q = q_ref[:, sl]
        k = k_ref[:, sl]
        v = v_ref[:, sl]
        b_col = bc[:, HEADS + hh:HEADS + hh + 1]
        li_col = x[:, hh:hh + 1]
        row_vec = xt[hh:hh + 1, :] - br[HEADS + hh:HEADS + hh + 1, :]
        b_end = tot[:, HEADS + hh:HEADS + hh + 1]
        m_prev = m_s[hh][0:1, 0:1]
        col_vec = -jnp.maximum(m_prev, cmax[:, hh:hh + 1])
        m_t = b_col - col_vec
        dmat = jnp.exp(jnp.where(tri, row_vec + col_vec, NEG))
        sc = lax.dot_general(q, k, nt_dims, preferred_element_type=F32) * dmat
        w_inter = jnp.exp(m_prev + col_vec)
        cta = cta_s[hh]
        v_aug = jnp.concatenate([v, one_col.astype(BF16)], axis=1)
        nd = (jnp.dot(sc.astype(BF16), v_aug, preferred_element_type=F32)
              + w_inter * jnp.dot(q, cta.astype(BF16), preferred_element_type=F32))
        den = nd[:, dh:dh + 1]
        h = nd[:, 0:dh] / jnp.maximum(jnp.abs(den), jnp.exp(-m_t))
        h_ref[:, sl] = h.astype(h_ref.dtype)
        g_col = b_end - b_col + li_col
        m_new = jnp.maximum(b_end + m_prev, jnp.max(g_col, axis=0, keepdims=True))
        wj = jnp.exp(g_col - m_new)
        decay = jnp.exp(b_end + m_prev - m_new)
        vw = jnp.concatenate([(v.astype(F32) * wj).astype(BF16), (one_col * wj).astype(BF16)], axis=1)
        cta_s[hh] = decay * cta + lax.dot_general(k, vw, tn_dims, preferred_element_type=F32)
        m_s[hh] = jnp.broadcast_to(m_new, m_s.shape[1:])


def _mlstm_scan(q, k, px, col_v, gates, gates_t, bsz, ntb):
    t, d = q.shape
    dh = d // HEADS
    gh = 2 * HEADS

    def chunk(b, dr, i):
        c = jnp.where(dr == 0, i, jnp.where(i == 0, 0, ntb - i))
        return b * ntb + c

    return pl.pallas_call(
        _scan_kernel,
        grid=(bsz, 2, ntb),
        in_specs=[
            pl.BlockSpec((TILE, d), lambda b, dr, i: (chunk(b, dr, i), 0)),
            pl.BlockSpec((TILE, d), lambda b, dr, i: (chunk(b, dr, i), 0)),
            pl.BlockSpec((TILE, d), lambda b, dr, i: (chunk(b, dr, i), col_v)),
            pl.BlockSpec((None, TILE, gh), lambda b, dr, i: (dr, chunk(b, dr, i), 0)),
            pl.BlockSpec((None, gh, TILE), lambda b, dr, i: (dr, 0, chunk(b, dr, i))),
        ],
        out_specs=pl.BlockSpec((None, TILE, d), lambda b, dr, i: (dr, chunk(b, dr, i), 0)),
        out_shape=jax.ShapeDtypeStruct((2, t, d), BF16),
        scratch_shapes=[
            pltpu.VMEM((HEADS, dh, dh + AUG), F32),
            pltpu.VMEM((HEADS, 8, 128), F32),
        ],
        compiler_params=_cparams(("arbitrary", "arbitrary", "arbitrary")),
        name="mlstm_scan",
    )(q, k, px, gates, gates_t)


def _conv_kernel(u_ref, dw_ref, db_ref, lg_ref, lb_ref, wo_ref, o_ref, pad_s):
    i = pl.program_id(1)
    dc = dw_ref.shape[1]
    u = u_ref[...].astype(F32)
    a = u[:, :dc] * _sigmoid(u[:, dc:])
    zeros = jnp.zeros((CONV_HALO, dc), F32)

    def depthwise(seg_len, n_seg):
        stride = seg_len + 2 * CONV_HALO
        used = n_seg * stride
        for s in range(n_seg):
            base = s * stride
            pad_s[0, base:base + CONV_HALO, :] = zeros
            pad_s[0, base + CONV_HALO:base + CONV_HALO + seg_len, :] = a[s * seg_len:(s + 1) * seg_len, :]
            pad_s[0, base + CONV_HALO + seg_len:base + stride, :] = zeros
        for rho in range(1, SUBLANES):
            pad_s[rho, 0:used - SUBLANES, :] = pad_s[0, rho:used - SUBLANES + rho, :]
        outs = []
        for s in range(n_seg):
            base = s * stride
            acc = jnp.zeros((seg_len, dc), F32)
            for j in range(CONV_WIDTH):
                off = base + CONV_HALO - CONV_WIDTH // 2 + j
                rho = off % SUBLANES
                acc = acc + dw_ref[j:j + 1, :] * pad_s[rho, off - rho:off - rho + seg_len, :]
            outs.append(acc)
        return outs[0] if n_seg == 1 else jnp.concatenate(outs, axis=0)

    def finish(acc):
        acc = acc + db_ref[...]
        mu = jnp.mean(acc, axis=-1, keepdims=True)
        var = jnp.mean(jnp.square(acc - mu), axis=-1, keepdims=True)
        y = (acc - mu) * lax.rsqrt(var + LN_EPS) * lg_ref[...] + lb_ref[...]
        y = y * _sigmoid(y)
        o_ref[...] = jnp.dot(y.astype(BF16), wo_ref[...], preferred_element_type=F32).astype(o_ref.dtype)

    @pl.when(i == 0)
    def _():
        finish(depthwise(TILE, 1))

    @pl.when(i > 0)
    def _():
        finish(depthwise(GRID_W, TILE // GRID_W))


def _conv_module(px, col, dw_w, dw_b, ln_g, ln_b, w_o_bf, bsz, ntb):
    t = px.shape[0]
    dc, d = w_o_bf.shape
    return pl.pallas_call(
        _conv_kernel,
        grid=(bsz, ntb),
        in_specs=[
            pl.BlockSpec((TILE, 2 * dc), lambda b, i: (b * ntb + i, col)),
            pl.BlockSpec((CONV_WIDTH, dc), lambda b, i: (0, 0)),
            pl.BlockSpec((1, dc), lambda b, i: (0, 0)),
            pl.BlockSpec((1, dc), lambda b, i: (0, 0)),
            pl.BlockSpec((1, dc), lambda b, i: (0, 0)),
            pl.BlockSpec((dc, d), lambda b, i: (0, 0)),
        ],
        out_specs=pl.BlockSpec((TILE, d), lambda b, i: (b * ntb + i, 0)),
        out_shape=jax.ShapeDtypeStruct((t, d), BF16),
        scratch_shapes=[pltpu.VMEM((SUBLANES, (TILE // GRID_W) * (GRID_W + 2 * CONV_HALO), dc), F32)],
        compiler_params=_cparams(("arbitrary", "arbitrary")),
        name="conv_module",
    )(px, dw_w, dw_b, ln_g, ln_b, w_o_bf)


def _pool_kernel(ctx_len, u_ref, pw_ref, ps_ref, wo_ref, o_ref, d_s):
    nt, dp = u_ref.shape
    pg = dp // len(POOL_WINDOWS)
    rows = (nt - ctx_len) // GRID_W
    uf = u_ref[...].astype(F32)
    tq = lax.broadcasted_iota(jnp.int32, (ctx_len, ctx_len), 0)
    tk = lax.broadcasted_iota(jnp.int32, (ctx_len, ctx_len), 1)
    for g, w in enumerate(POOL_WINDOWS):
        cs = slice(g * pg, (g + 1) * pg)
        lo = jnp.maximum(tq - w // 2, 0)
        hi = jnp.minimum(tq + w // 2 - 1, ctx_len - 1)
        band = jnp.where(tk >= lo, jnp.where(tk <= hi, 1.0, 0.0), 0.0).astype(BF16)
        ssum = jnp.dot(band, u_ref[0:ctx_len, cs], preferred_element_type=F32)
        cnt = (hi - lo + 1)[:, 0:1].astype(F32)
        dd = ssum / cnt - uf[0:ctx_len, cs]
        y = jnp.dot(dd.astype(BF16), pw_ref[g], preferred_element_type=F32)
        d_s[0:ctx_len, cs] = (y * ps_ref[:, cs]).astype(BF16)
    ridx = lax.broadcasted_iota(jnp.int32, (rows, GRID_W, pg), 0)
    for g, w in enumerate(POOL_WINDOWS):
        cs = slice(g * pg, (g + 1) * pg)
        xg = uf[ctx_len:, cs].reshape(rows, GRID_W, pg)
        acc = xg
        for dlt in range(-(w // 2), w // 2):
            if dlt == 0:
                continue
            z = jnp.zeros((abs(dlt), GRID_W, pg), F32)
            if dlt > 0:
                acc = acc + jnp.concatenate([xg[dlt:], z], axis=0)
            else:
                acc = acc + jnp.concatenate([z, xg[:dlt]], axis=0)
        cnt = (jnp.minimum(ridx + w // 2 - 1, rows - 1) - jnp.maximum(ridx - w // 2, 0) + 1).astype(F32)
        dd = (acc / cnt - xg).reshape(rows * GRID_W, pg)
        y = jnp.dot(dd.astype(BF16), pw_ref[g], preferred_element_type=F32)
        d_s[ctx_len:, cs] = (y * ps_ref[:, cs]).astype(BF16)
    o_ref[...] = jnp.dot(d_s[...], wo_ref[...], preferred_element_type=F32).astype(o_ref.dtype)


def _pool_module(px, col, pool_w_bf, pool_scale, w_o_bf, bsz, nt, ctx_len):
    t = px.shape[0]
    dp, d = w_o_bf.shape
    ng = len(POOL_WINDOWS)
    return pl.pallas_call(
        functools.partial(_pool_kernel, ctx_len),
        grid=(bsz,),
        in_specs=[
            pl.BlockSpec((nt, dp), lambda b: (b, col)),
            pl.BlockSpec((ng, dp // ng, dp // ng), lambda b: (0, 0, 0)),
            pl.BlockSpec((1, dp), lambda b: (0, 0)),
            pl.BlockSpec((dp, d), lambda b: (0, 0)),
        ],
        out_specs=pl.BlockSpec((nt, d), lambda b: (b, 0)),
        out_shape=jax.ShapeDtypeStruct((t, d), BF16),
        scratch_shapes=[pltpu.VMEM((nt, dp), BF16)],
        compiler_params=_cparams(("arbitrary",)),
        name="pool_module",
    )(px, pool_w_bf, pool_scale, w_o_bf)


def _mix_out_kernel(x_ref, gate_ref, z_ref, a_ref, h_ref, yc_ref, yp_ref, mod_ref, ng_ref, skip_ref, wm_ref,
                    wo_ref, g2_ref, rw_ref, rb_ref, xo_ref, hx_ref, lg_ref):
    d = x_ref.shape[1]
    dh = d // HEADS
    hsum = h_ref[0].astype(F32) + h_ref[1].astype(F32)
    parts = []
    for hh in range(HEADS):
        hb = hsum[:, hh * dh:(hh + 1) * dh]
        mu = jnp.mean(hb, axis=-1, keepdims=True)
        var = jnp.mean(jnp.square(hb - mu), axis=-1, keepdims=True)
        parts.append((hb - mu) * lax.rsqrt(var + LN_EPS))
    hn = jnp.concatenate(parts, axis=1)
    z = z_ref[...].astype(F32)
    tm = (hn * ng_ref[...] + skip_ref[...] * a_ref[...].astype(F32)) * (z * _sigmoid(z))
    y_m = jnp.dot(tm.astype(BF16), wm_ref[...], preferred_element_type=F32)
    merged = (_sigmoid(gate_ref[:, 0:d].astype(F32)) * yc_ref[...].astype(F32)
              + _sigmoid(gate_ref[:, d:2 * d].astype(F32)) * y_m
              + _sigmoid(gate_ref[:, 2 * d:3 * d].astype(F32)) * yp_ref[...].astype(F32))
    y = jnp.dot(merged.astype(BF16), wo_ref[...], preferred_element_type=F32)
    x = x_ref[...] + mod_ref[2:3, :] * y
    xo_ref[...] = x
    ms = jnp.mean(x * x, axis=-1, keepdims=True)
    hx = x * lax.rsqrt(ms + RMS_EPS) * g2_ref[...]
    hx = hx * (1.0 + mod_ref[4:5, :]) + mod_ref[3:4, :]
    hx_ref[...] = hx.reshape(hx_ref.shape)
    lg_ref[...] = jnp.dot(hx.astype(BF16), rw_ref[...], preferred_element_type=F32) + rb_ref[...]


def _mix_out(xt, px, col_z, a, h2, y_conv, y_pool, modtab, norm_g, skip, w_m_bf, w_out_bf, g2, router_w, router_b,
             ntb, mod_base):
    t, d = xt.shape
    ne = router_w.shape[1]
    row = _mod_row(ntb, mod_base)
    vec = lambda: pl.BlockSpec((1, d), lambda i: (0, 0))
    return pl.pallas_call(
        _mix_out_kernel,
        grid=(t // TILE,),
        in_specs=[
            pl.BlockSpec((TILE, d), lambda i: (i, 0)),
            pl.BlockSpec((TILE, 3 * d), lambda i: (i, 0)),
            pl.BlockSpec((TILE, d), lambda i: (i, col_z)),
            pl.BlockSpec((TILE, d), lambda i: (i, 0)),
            pl.BlockSpec((2, TILE, d), lambda i: (0, i, 0)),
            pl.BlockSpec((TILE, d), lambda i: (i, 0)),
            pl.BlockSpec((TILE, d), lambda i: (i, 0)),
            pl.BlockSpec((None, 6, d), lambda i: (row(i), 0, 0)),
            vec(), vec(),
            pl.BlockSpec((d, d), lambda i: (0, 0)),
            pl.BlockSpec((d, d), lambda i: (0, 0)),
            vec(),
            pl.BlockSpec((d, ne), lambda i: (0, 0)),
            pl.BlockSpec((1, ne), lambda i: (0, 0)),
        ],
        out_specs=[
            pl.BlockSpec((TILE, d), lambda i: (i, 0)),
            pl.BlockSpec((TILE, SUBLANES, d // SUBLANES), lambda i: (i, 0, 0)),
            pl.BlockSpec((TILE, ne), lambda i: (i, 0)),
        ],
        out_shape=[
            jax.ShapeDtypeStruct((t, d), F32),
            jax.ShapeDtypeStruct((t, SUBLANES, d // SUBLANES), F32),
            jax.ShapeDtypeStruct((t, ne), F32),
        ],
        compiler_params=_cparams(("arbitrary",)),
        name="mix_out",
    )(xt, px, px, a, h2, y_conv, y_pool, modtab, norm_g, skip, w_m_bf, w_out_bf, g2, router_w, router_b)


def _expert_kernel(be_ref, nu_ref, tok_ref, tokn_ref, hx_hbm, wgu_ref, bgu_ref, wdn_ref, bdn_ref, y_ref,
                   xbuf_a, xbuf_b, xbuf_c, xbuf_d, sem_a, sem_b, sem_c, sem_d, xb_s, wgu_s, wdn_s):
    i = pl.program_id(0)
    n_used = nu_ref[0]
    half = xbuf_a.shape[0]
    dff = wdn_s.shape[0]
    n_chunk = dff // MOE_CHUNK

    def row_copy(tok_smem, r, xbuf, sem):
        t = tok_smem[0, 0, r]
        return pltpu.make_async_copy(hx_hbm.at[t], xbuf.at[r % half], sem.at[0])

    def wait_rows(xbuf, sem):
        pltpu.make_async_copy(hx_hbm.at[pl.ds(0, half)], xbuf, sem.at[0]).wait()

    @pl.when(i == 0)
    def _():
        def body(r, carry):
            row_copy(tok_ref, r, xbuf_a, sem_a).start()
            row_copy(tok_ref, half + r, xbuf_b, sem_b).start()
            return carry
        lax.fori_loop(0, half, body, 0)

    e_now = be_ref[i]
    e_prev = be_ref[jnp.maximum(i - 1, 0)]

    @pl.when(jnp.logical_or(i == 0, e_now != e_prev))
    def _():
        wgu_s[...] = wgu_ref[...].astype(BF16)
        wdn_s[...] = wdn_ref[...].astype(BF16)

    def half_block(x_cur, sem_cur, out_rows, prefetch):
        wait_rows(x_cur, sem_cur)
        xb_s[...] = x_cur[...].reshape(xb_s.shape).astype(BF16)
        y = None
        for c in range(n_chunk):
            for r, (xbuf, sem) in (prefetch[c] if prefetch else ()):
                row_copy(tokn_ref, r, xbuf, sem).start(priority=r % 2)
            cg = slice(c * MOE_CHUNK, (c + 1) * MOE_CHUNK)
            cu = slice(dff + c * MOE_CHUNK, dff + (c + 1) * MOE_CHUNK)
            g = jnp.dot(xb_s[...], wgu_s[:, cg], preferred_element_type=F32) + bgu_ref[:, cg]
            u = jnp.dot(xb_s[...], wgu_s[:, cu], preferred_element_type=F32) + bgu_ref[:, cu]
            g = jnp.minimum(g, SWIGLU_LIMIT)
            u = jnp.clip(u, -SWIGLU_LIMIT, SWIGLU_LIMIT)
            act = (u + 1.0) * (g * _sigmoid(SWIGLU_ALPHA * g))
            part = jnp.dot(act.astype(BF16), wdn_s[cg, :], preferred_element_type=F32)
            y = part if y is None else y + part
        y_ref[out_rows] = (y + bdn_ref[...]).reshape((half,) + y_ref.shape[1:])

    used = i < n_used
    even = i % 2 == 0

    def block(cur, nxt):
        (xa, sa), (xb, sb) = cur
        per = 2 * half // n_chunk
        prefetch = [[(r, nxt[r // half]) for r in range(c * per, (c + 1) * per)] for c in range(n_chunk)]
        half_block(xa, sa, slice(0, half), prefetch)
        half_block(xb, sb, slice(half, 2 * half), None)

    bufs_even = ((xbuf_a, sem_a), (xbuf_b, sem_b))
    bufs_odd = ((xbuf_c, sem_c), (xbuf_d, sem_d))

    @pl.when(jnp.logical_and(used, even))
    def _():
        block(bufs_even, bufs_odd)

    @pl.when(jnp.logical_and(used, jnp.logical_not(even)))
    def _():
        block(bufs_odd, bufs_even)

    @pl.when(jnp.logical_not(used))
    def _():
        @pl.when(jnp.logical_and(i == n_used, even))
        def _():
            wait_rows(xbuf_a, sem_a)
            wait_rows(xbuf_b, sem_b)

        @pl.when(jnp.logical_and(i == n_used, jnp.logical_not(even)))
        def _():
            wait_rows(xbuf_c, sem_c)
            wait_rows(xbuf_d, sem_d)

        y_ref[...] = jnp.zeros(y_ref.shape, y_ref.dtype)


def _experts(layer, hx, row_tok, block_e, n_used, w_gu, b_gu, w_dn, b_dn):
    t, sub, dl = hx.shape
    d = sub * dl
    depth, ne, _, dff2 = w_gu.shape
    dff = dff2 // 2
    nb = block_e.shape[0]
    bm = MOE_BM
    half = bm // 2
    tok3 = row_tok.reshape(nb, 1, bm)
    grid_spec = pltpu.PrefetchScalarGridSpec(
        num_scalar_prefetch=2,
        grid=(nb,),
        in_specs=[
            pl.BlockSpec((1, 1, bm), lambda i, be, nu: (i, 0, 0), memory_space=pltpu.SMEM),
            pl.BlockSpec((1, 1, bm), lambda i, be, nu: (jnp.minimum(i + 1, nb - 1), 0, 0),
                         memory_space=pltpu.SMEM),
            pl.BlockSpec(memory_space=pl.ANY),
            pl.BlockSpec((None, None, d, dff2), lambda i, be, nu: (layer, be[i], 0, 0)),
            pl.BlockSpec((None, None, 1, dff2), lambda i, be, nu: (layer, be[i], 0, 0)),
            pl.BlockSpec((None, None, dff, d), lambda i, be, nu: (layer, be[i], 0, 0)),
            pl.BlockSpec((None, None, 1, d), lambda i, be, nu: (layer, be[i], 0, 0)),
        ],
        out_specs=pl.BlockSpec((bm, sub, dl), lambda i, be, nu: (i, 0, 0)),
        scratch_shapes=[
            pltpu.VMEM((half, sub, dl), F32),
            pltpu.VMEM((half, sub, dl), F32),
            pltpu.VMEM((half, sub, dl), F32),
            pltpu.VMEM((half, sub, dl), F32),
            pltpu.SemaphoreType.DMA((1,)),
            pltpu.SemaphoreType.DMA((1,)),
            pltpu.SemaphoreType.DMA((1,)),
            pltpu.SemaphoreType.DMA((1,)),
            pltpu.VMEM((half, d), BF16),
            pltpu.VMEM((d, dff2), BF16),
            pltpu.VMEM((dff, d), BF16),
        ],
    )
    return pl.pallas_call(
        _expert_kernel,
        grid_spec=grid_spec,
        out_shape=jax.ShapeDtypeStruct((nb * bm, sub, dl), F32),
        compiler_params=_cparams(("arbitrary",), no_bounds_checks=True),
        name="moe_experts",
    )(block_e, n_used, tok3, tok3, hx, w_gu, b_gu.reshape(depth, ne, 1, dff2), w_dn,
      b_dn.reshape(depth, ne, 1, d))


def _combine_kernel(dst_ref, dstn_ref, ys_hbm, gw_ref, x_ref, mod_ref, o_ref, buf, sem):
    i = pl.program_id(0)
    nsteps = pl.num_programs(0)
    slot = i % 2
    n_copy = TOP_K * TILE

    def row_copy(dst_smem, e, sl):
        src = dst_smem[0, 0, e]
        kk = e // TILE
        r = e % TILE
        return pltpu.make_async_copy(ys_hbm.at[src], buf.at[sl, kk, r], sem.at[sl])

    @pl.when(i == 0)
    def _():
        def body(e, carry):
            row_copy(dst_ref, e, 0).start()
            return carry
        lax.fori_loop(0, n_copy, body, 0)

    @pl.when(i + 1 < nsteps)
    def _():
        for e in range(n_copy):
            row_copy(dstn_ref, e, 1 - slot).start(priority=e % 2)

    for kk in range(TOP_K):
        pltpu.make_async_copy(ys_hbm.at[pl.ds(0, TILE)], buf.at[slot, kk], sem.at[slot]).wait()

    gw = gw_ref[...]
    y = gw[:, 0:1] * buf[slot, 0].reshape(x_ref.shape)
    for kk in range(1, TOP_K):
        y = y + gw[:, kk:kk + 1] * buf[slot, kk].reshape(x_ref.shape)
    o_ref[...] = x_ref[...] + mod_ref[5:6, :] * y


def _combine(ys, dest_tiles, gate_w, xt, modtab, ntb, mod_base):
    t, d = xt.shape
    nt = t // TILE
    row = _mod_row(ntb, mod_base)
    return pl.pallas_call(
        _combine_kernel,
        grid=(nt,),
        in_specs=[
            pl.BlockSpec((1, 1, TOP_K * TILE), lambda i: (i, 0, 0), memory_space=pltpu.SMEM),
            pl.BlockSpec((1, 1, TOP_K * TILE), lambda i: (jnp.minimum(i + 1, nt - 1), 0, 0),
                         memory_space=pltpu.SMEM),
            pl.BlockSpec(memory_space=pl.ANY),
            pl.BlockSpec((TILE, TOP_K), lambda i: (i, 0)),
            pl.BlockSpec((TILE, d), lambda i: (i, 0)),
            pl.BlockSpec((None, 6, d), lambda i: (row(i), 0, 0)),
        ],
        out_specs=pl.BlockSpec((TILE, d), lambda i: (i, 0)),
        out_shape=jax.ShapeDtypeStruct((t, d), F32),
        scratch_shapes=[
            pltpu.VMEM((2, TOP_K, TILE) + ys.shape[1:], F32),
            pltpu.SemaphoreType.DMA((2,)),
        ],
        compiler_params=_cparams(("arbitrary",), no_bounds_checks=True),
        name="moe_combine",
    )(dest_tiles, dest_tiles, ys, gate_w, xt, modtab)


IN_CHUNKS = 6
IN_DMA_CHUNKS = 4


def _comb_in_kernel(dst_ref, dstn_ref, ys_hbm, gw_ref, x_ref, modp_ref, g_ref, mod_ref, w_ref, xo_ref, o_ref,
                    buf_a, buf_b, sem_a, sem_b, fsem, hb_s):
    i = pl.program_id(0)
    nsteps = pl.num_programs(0)
    n_copy = TOP_K * TILE
    n = w_ref.shape[1]
    cw = n // IN_CHUNKS
    per = n_copy // IN_DMA_CHUNKS

    def row_copy(dst_smem, e, buf, sem):
        src = dst_smem[0, 0, e]
        return pltpu.make_async_copy(ys_hbm.at[src], buf.at[e // TILE, e % TILE], sem.at[0])

    def wait_rows(buf, sem):
        for kk in range(TOP_K):
            pltpu.make_async_copy(ys_hbm.at[pl.ds(0, TILE)], buf.at[kk], sem.at[0]).wait()

    @pl.when(i == 0)
    def _():
        def body(e, carry):
            row_copy(dst_ref, e, buf_a, sem_a).start()
            return carry
        lax.fori_loop(0, n_copy, body, 0)

    def tile(buf_cur, sem_cur, buf_nxt, sem_nxt):
        wait_rows(buf_cur, sem_cur)
        gw = gw_ref[...]
        y = gw[:, 0:1] * buf_cur[0].reshape(x_ref.shape)
        for kk in range(1, TOP_K):
            y = y + gw[:, kk:kk + 1] * buf_cur[kk].reshape(x_ref.shape)
        x = x_ref[...] + modp_ref[5:6, :] * y
        xo_ref[...] = x
        ms = jnp.mean(x * x, axis=-1, keepdims=True)
        h = x * lax.rsqrt(ms + RMS_EPS) * g_ref[...]
        h = h * (1.0 + mod_ref[1:2, :]) + mod_ref[0:1, :]
        hb_s[...] = h.astype(BF16)
        for c in range(IN_CHUNKS):
            if c < IN_DMA_CHUNKS:
                for e in range(c * per, (c + 1) * per):
                    row_copy(dstn_ref, e, buf_nxt, sem_nxt).start(priority=e % 2)
            if c == IN_DMA_CHUNKS:
                pl.semaphore_signal(fsem, 1)
                pl.semaphore_wait(fsem, 1)
            cs = slice(c * cw, (c + 1) * cw)
            o_ref[:, cs] = jnp.dot(hb_s[...], w_ref[:, cs], preferred_element_type=F32).astype(o_ref.dtype)

    even = i % 2 == 0

    @pl.when(even)
    def _():
        tile(buf_a, sem_a, buf_b, sem_b)

    @pl.when(jnp.logical_not(even))
    def _():
        tile(buf_b, sem_b, buf_a, sem_a)

    @pl.when(jnp.logical_and(i == nsteps - 1, even))
    def _():
        wait_rows(buf_b, sem_b)

    @pl.when(jnp.logical_and(i == nsteps - 1, jnp.logical_not(even)))
    def _():
        wait_rows(buf_a, sem_a)


def _comb_in_proj(ys, dest_tiles, gate_w, xt, g1, modtab, w_in_bf, ntb, mod_base_prev, mod_base):
    t, d = xt.shape
    n = w_in_bf.shape[1]
    nt = t // TILE
    assert n % (IN_CHUNKS * 256) == 0 and (TOP_K * TILE) % IN_DMA_CHUNKS == 0
    row_prev = _mod_row(ntb, mod_base_prev)
    row = _mod_row(ntb, mod_base)
    return pl.pallas_call(
        _comb_in_kernel,
        grid=(nt,),
        in_specs=[
            pl.BlockSpec((1, 1, TOP_K * TILE), lambda i: (i, 0, 0), memory_space=pltpu.SMEM),
            pl.BlockSpec((1, 1, TOP_K * TILE), lambda i: (jnp.minimum(i + 1, nt - 1), 0, 0),
                         memory_space=pltpu.SMEM),
            pl.BlockSpec(memory_space=pl.ANY),
            pl.BlockSpec((TILE, TOP_K), lambda i: (i, 0)),
            pl.BlockSpec((TILE, d), lambda i: (i, 0)),
            pl.BlockSpec((None, 6, d), lambda i: (row_prev(i), 0, 0)),
            pl.BlockSpec((1, d), lambda i: (0, 0)),
            pl.BlockSpec((None, 6, d), lambda i: (row(i), 0, 0)),
            pl.BlockSpec((d, n), lambda i: (0, 0), pipeline_mode=pl.Buffered(1)),
        ],
        out_specs=[
            pl.BlockSpec((TILE, d), lambda i: (i, 0)),
            pl.BlockSpec((TILE, n), lambda i: (i, 0)),
        ],
        out_shape=[
            jax.ShapeDtypeStruct((t, d), F32),
            jax.ShapeDtypeStruct((t, n), BF16),
        ],
        scratch_shapes=[
            pltpu.VMEM((TOP_K, TILE) + ys.shape[1:], F32),
            pltpu.VMEM((TOP_K, TILE) + ys.shape[1:], F32),
            pltpu.SemaphoreType.DMA((1,)),
            pltpu.SemaphoreType.DMA((1,)),
            pltpu.SemaphoreType.REGULAR,
            pltpu.VMEM((TILE, d), BF16),
        ],
        compiler_params=_cparams(("arbitrary",), no_bounds_checks=True),
        name="comb_in_proj",
    )(dest_tiles, dest_tiles, ys, gate_w, xt, modtab, g1, modtab, w_in_bf)


def _final_kernel(x_ref, g_ref, o_ref):
    x = x_ref[...]
    ms = jnp.mean(x * x, axis=-1, keepdims=True)
    o_ref[...] = x * lax.rsqrt(ms + RMS_EPS) * g_ref[...]


def _final_norm(xt, final_g, bsz, ntb):
    t, d = xt.shape
    nlat = ntb - 1
    return pl.pallas_call(
        _final_kernel,
        grid=(bsz, nlat),
        in_specs=[
            pl.BlockSpec((TILE, d), lambda b, j: (b * ntb + 1 + j, 0)),
            pl.BlockSpec((1, d), lambda b, j: (0, 0)),
        ],
        out_specs=pl.BlockSpec((TILE, d), lambda b, j: (b * nlat + j, 0)),
        out_shape=jax.ShapeDtypeStruct((bsz * nlat * TILE, d), F32),
        compiler_params=_cparams(("arbitrary", "arbitrary")),
        name="final_norm",
    )(xt, final_g)


def _route_kernel(lg_ref, te_ref, gw_ref, pos_ref, cnt_ref, cnt_s):
    i = pl.program_id(0)

    @pl.when(i == 0)
    def _():
        cnt_s[...] = jnp.zeros(cnt_s.shape, F32)

    lg = lg_ref[...]
    ne = lg.shape[1]
    lane = lax.broadcasted_iota(jnp.int32, lg.shape, 1)
    work = lg
    vals, idxs, hots = [], [], []
    for _ in range(TOP_K):
        m = jnp.max(work, axis=1, keepdims=True)
        idx = jnp.min(jnp.where(work == m, lane, ne), axis=1, keepdims=True)
        hot = lane == idx
        vals.append(m)
        idxs.append(idx)
        hots.append(jnp.where(hot, 1.0, 0.0))
        work = jnp.where(hot, -jnp.inf, work)
    exps = [jnp.exp(v - vals[0]) for v in vals]
    ssum = exps[0]
    for e in exps[1:]:
        ssum = ssum + e
    hot_all = hots[0]
    for h in hots[1:]:
        hot_all = hot_all + h
    r2 = lax.broadcasted_iota(jnp.int32, (TILE, TILE), 0)
    c2 = lax.broadcasted_iota(jnp.int32, (TILE, TILE), 1)
    earlier = jnp.where(r2 > c2, 1.0, 0.0).astype(BF16)
    before = jnp.dot(earlier, hot_all.astype(BF16), preferred_element_type=F32) + cnt_s[...]
    for kk in range(TOP_K):
        te_ref[:, kk:kk + 1] = idxs[kk]
        gw_ref[:, kk:kk + 1] = exps[kk] / ssum
        pos_ref[:, kk:kk + 1] = jnp.sum(hots[kk] * before, axis=1, keepdims=True).astype(jnp.int32)
    cnt_s[...] = cnt_s[...] + jnp.sum(hot_all, axis=0, keepdims=True)
    cnt_ref[...] = cnt_s[...]


def _route(logits):
    t, ne = logits.shape
    return pl.pallas_call(
        _route_kernel,
        grid=(t // TILE,),
        in_specs=[pl.BlockSpec((TILE, ne), lambda i: (i, 0))],
        out_specs=[
            pl.BlockSpec((TILE, TOP_K), lambda i: (i, 0)),
            pl.BlockSpec((TILE, TOP_K), lambda i: (i, 0)),
            pl.BlockSpec((TILE, TOP_K), lambda i: (i, 0)),
            pl.BlockSpec((1, ne), lambda i: (0, 0)),
        ],
        out_shape=[
            jax.ShapeDtypeStruct((t, TOP_K), jnp.int32),
            jax.ShapeDtypeStruct((t, TOP_K), F32),
            jax.ShapeDtypeStruct((t, TOP_K), jnp.int32),
            jax.ShapeDtypeStruct((1, ne), F32),
        ],
        scratch_shapes=[pltpu.VMEM((1, ne), F32)],
        compiler_params=_cparams(("arbitrary",)),
        name="route",
    )(logits)


def _routing(logits, n_tiles):
    t = logits.shape[0]
    top_e, gate, pos, cnt = _route(logits)
    n_assign = t * TOP_K
    experts = jnp.arange(N_EXPERTS, dtype=jnp.int32)
    flat_e = top_e.reshape(-1)
    order = jnp.argsort(flat_e, stable=True).astype(jnp.int32)
    counts = cnt.reshape(N_EXPERTS).astype(jnp.int32)
    grp_start = jnp.cumsum(counts) - counts
    padded = (counts + MOE_BM - 1) // MOE_BM * MOE_BM
    pad_end = jnp.cumsum(padded)
    pad_start = pad_end - padded
    dest = pos + jnp.sum(jnp.where(top_e[:, :, None] == experts[None, None, :], pad_start[None, None, :], 0), axis=2)
    n_blocks = n_assign // MOE_BM + N_EXPERTS
    blk = jnp.arange(n_blocks, dtype=jnp.int32)
    block_e = jnp.minimum(jnp.sum((pad_end[None, :] <= (blk * MOE_BM)[:, None]).astype(jnp.int32), axis=1),
                          N_EXPERTS - 1)
    src0 = grp_start[block_e] + blk * MOE_BM - pad_start[block_e]
    lim = grp_start[block_e] + counts[block_e]
    src = src0[:, None] + jnp.arange(MOE_BM, dtype=jnp.int32)[None, :]
    valid = src < lim[:, None]
    row_tok = jnp.where(valid, order[jnp.clip(src, 0, n_assign - 1)] // TOP_K, 0).astype(jnp.int32).reshape(-1)
    n_used = (pad_end[-1] // MOE_BM).astype(jnp.int32).reshape(1)
    dest_tiles = dest.reshape(n_tiles, TILE, TOP_K).transpose(0, 2, 1).reshape(n_tiles, 1, TOP_K * TILE)
    return gate, row_tok, block_e, n_used, dest_tiles


def _block_diag(w, heads):
    nb, c, _ = w.shape
    per = nb // heads
    w = w.reshape(heads, per, c, c)
    eye = jnp.eye(per, dtype=w.dtype)
    full = eye[None, :, None, :, None] * w[:, :, :, None, :]
    return full.reshape(heads, per * c, per * c)


def kernel(x, c, ctx, c_ctx, w_ada, b_ada, norm1_g, norm2_g, w_in, conv_dw_w, conv_dw_b, conv_ln_g, conv_ln_b, w_conv_out, m_sc_w, m_sc_b, m_wq, m_wk, m_w_if, m_b_if, m_norm_g, m_skip, w_m_out, pool_w, pool_scale, w_pool_out, w_out, router_w, router_b, w_gu, b_gu, w_dn, b_dn, final_g):
    bsz, seq, d = x.shape
    ctx_len = ctx.shape[1]
    depth = w_ada.shape[0]
    assert ctx_len == TILE and seq % TILE == 0 and d % (128 * HEADS) == 0
    nt = ctx_len + seq
    ntb = nt // TILE
    t = bsz * nt
    n_tiles = t // TILE
    d_conv = w_conv_out.shape[1]
    d_pool = w_pool_out.shape[1]
    d_m = w_m_out.shape[1]
    assert d_m == d and 2 * d_conv == d and 2 * d_pool == d
    c_conv = 3 * d
    c_pool = c_conv + 2 * d_conv
    c_mz = c_pool + d_pool
    c_mqk = c_mz + d_m
    c_mv = c_mqk + d_m
    d_in = c_mv + d_m
    col_conv, col_z, col_u, col_v = 3, 4, 5, 6
    col_pool = (7 * d) // d_pool

    xt = jnp.concatenate([ctx, x], axis=1).reshape(t, d)

    mp = -(-(bsz + 1) // 8) * 8
    c_all = jnp.zeros((mp, d), F32).at[:bsz].set(c).at[bsz].set(c_ctx)
    mod = _ada(c_all, w_ada, b_ada)
    mod_x = mod[:, :bsz].reshape(depth, bsz, 1, 6, d)
    mod_c = jnp.broadcast_to(mod[:, bsz].reshape(depth, 1, 1, 6, d), (depth, bsz, 1, 6, d))
    modtab = jnp.concatenate([mod_c, mod_x], axis=2).reshape(depth * bsz * 2, 6, d)

    gh = 2 * HEADS
    for l in range(depth):
        mod_base = l * bsz * 2
        w_in_bf = jnp.concatenate([w_in[l, :, 0:c_pool], w_in[l, :, c_mz:d_in], w_in[l, :, c_pool:c_mz]],
                                  axis=1).astype(BF16)
        if l == 0:
            px = _in_proj(xt, norm1_g[l].reshape(1, d), modtab, w_in_bf, ntb, mod_base)
        else:
            xt, px = _comb_in_proj(ys, dest_tiles, gate_w, xt, norm1_g[l].reshape(1, d), modtab, w_in_bf, ntb,
                                   mod_base - bsz * 2, mod_base)

        wq_bd = _block_diag(m_wq[l], HEADS).astype(BF16)
        wk_bd = _block_diag(m_wk[l], HEADS).astype(BF16)
        wif = jnp.concatenate([m_w_if[l, 0], m_w_if[l, 1]], axis=1)
        bif = jnp.concatenate([m_b_if[l, 0], m_b_if[l, 1]], axis=0)
        a, q, k, gates, gates_t = _mlstm_pre(
            px, col_u, col_v, m_sc_w[l], m_sc_b[l].reshape(1, d), wq_bd, wk_bd,
            wif.astype(BF16), wif.T.astype(BF16), bif.reshape(1, 2 * gh), bif.reshape(2 * gh, 1), bsz, ntb)
        h2 = _mlstm_scan(q, k, px, col_v, gates, gates_t, bsz, ntb)

        y_conv = _conv_module(px, col_conv, conv_dw_w[l], conv_dw_b[l].reshape(1, d_conv),
                              conv_ln_g[l].reshape(1, d_conv), conv_ln_b[l].reshape(1, d_conv),
                              w_conv_out[l].astype(BF16), bsz, ntb)
        y_pool = _pool_module(px, col_pool, pool_w[l].astype(BF16), pool_scale[l].reshape(1, d_pool),
                              w_pool_out[l].astype(BF16), bsz, nt, ctx_len)

        xt, hx, logits = _mix_out(
            xt, px, col_z, a, h2, y_conv, y_pool, modtab, m_norm_g[l].reshape(1, d), m_skip[l].reshape(1, d),
            w_m_out[l].astype(BF16), w_out[l].astype(BF16), norm2_g[l].reshape(1, d), router_w[l].astype(BF16),
            router_b[l].reshape(1, N_EXPERTS), ntb, mod_base)

        gate_w, row_tok, block_e, n_used, dest_tiles = _routing(logits, n_tiles)
        ys = _experts(l, hx, row_tok, block_e, n_used, w_gu, b_gu, w_dn, b_dn)

    xt = _combine(ys, dest_tiles, gate_w, xt, modtab, ntb, (depth - 1) * bsz * 2)
    out = _final_norm(xt, final_g.reshape(1, d), bsz, ntb)
    return out.reshape(bsz, seq, d)
```

```python
import functools

import jax
import jax.numpy as jnp
from jax import lax
from jax.experimental import pallas as pl
from jax.experimental.pallas import tpu as pltpu

F32 = jnp.float32
BF16 = jnp.bfloat16
HIGHEST = lax.Precision.HIGHEST

TILE = 256
GRID_W = 64
CONV_WIDTH = 31
SUBLANES = 8
CONV_HALO = 16
SHORT_CONV = 4
HEADS = 4
POOL_WINDOWS = (2, 4, 8, 16)
N_EXPERTS = 32
TOP_K = 4
MOE_BM = 512
MOE_CHUNK = 256
SWIGLU_LIMIT = 7.0
SWIGLU_ALPHA = 1.702
RMS_EPS = 1e-6
LN_EPS = 1e-5
NEG = -1e30
VMEM_LIMIT = 56 * 1024 * 1024


def _cparams(sem, no_bounds_checks=False):
    return pltpu.CompilerParams(dimension_semantics=sem, vmem_limit_bytes=VMEM_LIMIT,
                                disable_bounds_checks=no_bounds_checks)


def _sigmoid(x):
    return jax.nn.sigmoid(x)


def _log_sigmoid(x):
    return jnp.minimum(x, 0.0) - jnp.log1p(jnp.exp(-jnp.abs(x)))


def _ada_kernel(c_ref, w_ref, b_ref, o_ref):
    s = c_ref[...]
    s = s * _sigmoid(s)
    o_ref[...] = jnp.dot(s, w_ref[...], precision=HIGHEST, preferred_element_type=F32) + b_ref[...]


def _ada(c_all, w_ada, b_ada):
    depth, d, n = w_ada.shape
    mp = c_all.shape[0]
    tn = 1536
    return pl.pallas_call(
        _ada_kernel,
        grid=(depth, n // tn),
        in_specs=[
            pl.BlockSpec((mp, d), lambda l, j: (0, 0)),
            pl.BlockSpec((None, d, tn), lambda l, j: (l, 0, j)),
            pl.BlockSpec((None, 1, tn), lambda l, j: (l, 0, j)),
        ],
        out_specs=pl.BlockSpec((None, mp, tn), lambda l, j: (l, 0, j)),
        out_shape=jax.ShapeDtypeStruct((depth, mp, n), F32),
        compiler_params=_cparams(("arbitrary", "arbitrary")),
        name="ada",
    )(c_all, w_ada, b_ada.reshape(depth, 1, n))


def _in_kernel(x_ref, g_ref, mod_ref, w_ref, o_ref):
    x = x_ref[...]
    ms = jnp.mean(x * x, axis=-1, keepdims=True)
    y = x * lax.rsqrt(ms + RMS_EPS) * g_ref[...]
    h = y * (1.0 + mod_ref[1:2, :]) + mod_ref[0:1, :]
    o_ref[...] = jnp.dot(h.astype(BF16), w_ref[...], preferred_element_type=F32).astype(BF16)


def _mod_row(ntb, base):
    def f(i):
        return base + 2 * (i // ntb) + jnp.minimum(i % ntb, 1)
    return f


def _in_proj(xt, g1, modtab, w_in_bf, ntb, mod_base):
    t, d = xt.shape
    n = w_in_bf.shape[1]
    row = _mod_row(ntb, mod_base)
    return pl.pallas_call(
        _in_kernel,
        grid=(t // TILE,),
        in_specs=[
            pl.BlockSpec((TILE, d), lambda i: (i, 0)),
            pl.BlockSpec((1, d), lambda i: (0, 0)),
            pl.BlockSpec((None, 6, d), lambda i: (row(i), 0, 0)),
            pl.BlockSpec((d, n), lambda i: (0, 0), pipeline_mode=pl.Buffered(1)),
        ],
        out_specs=pl.BlockSpec((TILE, n), lambda i: (i, 0)),
        out_shape=jax.ShapeDtypeStruct((t, n), BF16),
        compiler_params=_cparams(("arbitrary",)),
        name="in_proj",
    )(xt, g1, modtab, w_in_bf)


def _mpre_kernel(ntb, u_ref, up_ref, un_ref, v_ref, scw_ref, scb_ref, wq_ref, wk_ref, wif_ref, wift_ref,
                 bif_ref, bift_ref, a_ref, q_ref, k_ref, g_ref, gt_ref):
    i = pl.program_id(1)
    d = u_ref.shape[1]
    dh = d // HEADS
    u = u_ref[...].astype(F32)
    up = jnp.where(i >= 2, up_ref[...].astype(F32)[8:16, :], 0.0)
    un = jnp.where(jnp.logical_and(i >= 1, i <= ntb - 2), un_ref[...].astype(F32)[0:8, :], 0.0)
    ext = jnp.concatenate([up, u, un], axis=0)
    n_ext = TILE + 16
    conv = scb_ref[...]
    for j in range(SHORT_CONV):
        sh = (SHORT_CONV // 2 - j) % n_ext
        shifted = ext if sh == 0 else pltpu.roll(ext, sh, 0)
        conv = conv + scw_ref[j:j + 1, :] * shifted[8:8 + TILE, :]
    a = conv * _sigmoid(conv)
    ab = a.astype(BF16)
    a_ref[...] = ab
    qs, ks = [], []
    for hb in range(HEADS):
        blk = ab[:, hb * dh:(hb + 1) * dh]
        qs.append(jnp.dot(blk, wq_ref[hb], preferred_element_type=F32))
        ks.append(jnp.dot(blk, wk_ref[hb], preferred_element_type=F32))
    q = jnp.concatenate(qs, axis=1)
    k = jnp.concatenate(ks, axis=1)
    qb = q.astype(BF16)
    kb = k.astype(BF16)
    vb = v_ref[...]
    q_ref[...] = (q * (dh ** -0.5)).astype(BF16)
    k_ref[...] = kb
    pre = bif_ref[...]
    pret = bift_ref[...]
    nt_dims = (((1,), (1,)), ((), ()))
    for part, f in enumerate((qb, kb, vb)):
        pre = pre + jnp.dot(f, wif_ref[part * d:(part + 1) * d, :], preferred_element_type=F32)
        pret = pret + lax.dot_general(wift_ref[:, part * d:(part + 1) * d], f, nt_dims,
                                      preferred_element_type=F32)
    col = lax.broadcasted_iota(jnp.int32, pre.shape, 1)
    pre = jnp.where(col % (2 * HEADS) >= HEADS, _log_sigmoid(pre), pre)
    rowi = lax.broadcasted_iota(jnp.int32, pret.shape, 0)
    pret = jnp.where(rowi % (2 * HEADS) >= HEADS, _log_sigmoid(pret), pret)
    g_ref[0] = pre[:, 0:2 * HEADS]
    g_ref[1] = pre[:, 2 * HEADS:4 * HEADS]
    gt_ref[0] = pret[0:2 * HEADS, :]
    gt_ref[1] = pret[2 * HEADS:4 * HEADS, :]


def _mlstm_pre(px, col_u, col_v, sc_w, sc_b, wq_bd, wk_bd, wif, wift, bif, bift, bsz, ntb):
    t = px.shape[0]
    d = sc_w.shape[1]
    hrows = 16
    nh = t // hrows
    per = TILE // hrows

    def tile(b, i):
        return b * ntb + i

    gh = 2 * HEADS
    return pl.pallas_call(
        functools.partial(_mpre_kernel, ntb),
        grid=(bsz, ntb),
        in_specs=[
            pl.BlockSpec((TILE, d), lambda b, i: (tile(b, i), col_u)),
            pl.BlockSpec((hrows, d), lambda b, i: (jnp.maximum(tile(b, i) * per - 1, 0), col_u)),
            pl.BlockSpec((hrows, d), lambda b, i: (jnp.minimum((tile(b, i) + 1) * per, nh - 1), col_u)),
            pl.BlockSpec((TILE, d), lambda b, i: (tile(b, i), col_v)),
            pl.BlockSpec((SHORT_CONV, d), lambda b, i: (0, 0)),
            pl.BlockSpec((1, d), lambda b, i: (0, 0)),
            pl.BlockSpec((HEADS, d // HEADS, d // HEADS), lambda b, i: (0, 0, 0)),
            pl.BlockSpec((HEADS, d // HEADS, d // HEADS), lambda b, i: (0, 0, 0)),
            pl.BlockSpec((3 * d, 2 * gh), lambda b, i: (0, 0)),
            pl.BlockSpec((2 * gh, 3 * d), lambda b, i: (0, 0)),
            pl.BlockSpec((1, 2 * gh), lambda b, i: (0, 0)),
            pl.BlockSpec((2 * gh, 1), lambda b, i: (0, 0)),
        ],
        out_specs=[
            pl.BlockSpec((TILE, d), lambda b, i: (tile(b, i), 0)),
            pl.BlockSpec((TILE, d), lambda b, i: (tile(b, i), 0)),
            pl.BlockSpec((TILE, d), lambda b, i: (tile(b, i), 0)),
            pl.BlockSpec((2, TILE, gh), lambda b, i: (0, tile(b, i), 0)),
            pl.BlockSpec((2, gh, TILE), lambda b, i: (0, 0, tile(b, i))),
        ],
        out_shape=[
            jax.ShapeDtypeStruct((t, d), BF16),
            jax.ShapeDtypeStruct((t, d), BF16),
            jax.ShapeDtypeStruct((t, d), BF16),
            jax.ShapeDtypeStruct((2, t, gh), F32),
            jax.ShapeDtypeStruct((2, gh, t), F32),
        ],
        compiler_params=_cparams(("arbitrary", "arbitrary")),
        name="mlstm_pre",
    )(px, px, px, px, sc_w, sc_b, wq_bd, wk_bd, wif, wift, bif, bift)


AUG = 128


def _scan_kernel(q_ref, k_ref, v_ref, g_ref, gt_ref, h_ref, cta_s, m_s):
    dr = pl.program_id(1)
    i = pl.program_id(2)
    dh = q_ref.shape[1] // HEADS

    @pl.when(i == 0)
    def _():
        cta_s[...] = jnp.zeros(cta_s.shape, F32)
        m_s[...] = jnp.zeros(m_s.shape, F32)

    x = g_ref[...]
    xt = gt_ref[...]
    rid = lax.broadcasted_iota(jnp.int32, x.shape, 0)
    cid = lax.broadcasted_iota(jnp.int32, xt.shape, 1)
    p = x
    pt = xt
    s = 1
    while s < TILE:
        p = p + jnp.where(rid >= s, pltpu.roll(p, s, 0), 0.0)
        pt = pt + jnp.where(cid >= s, pltpu.roll(pt, s, 1), 0.0)
        s *= 2
    tot = p[TILE - 1:TILE, :]
    tott = pt[:, TILE - 1:TILE]
    fwd = dr == 0
    bc = jnp.where(fwd, p, tot - p + x)
    br = jnp.where(fwd, pt, tott - pt + xt)
    amat = x[:, 0:HEADS] - bc[:, HEADS:2 * HEADS]
    rida = lax.broadcasted_iota(jnp.int32, amat.shape, 0)
    cpre = amat
    csuf = amat
    s = 1
    while s < TILE:
        cpre = jnp.maximum(cpre, jnp.where(rida >= s, pltpu.roll(cpre, s, 0), NEG))
        csuf = jnp.maximum(csuf, jnp.where(rida < TILE - s, pltpu.roll(csuf, TILE - s, 0), NEG))
        s *= 2
    cmax = jnp.where(fwd, cpre, csuf)
    r2 = lax.broadcasted_iota(jnp.int32, (TILE, TILE), 0)
    c2 = lax.broadcasted_iota(jnp.int32, (TILE, TILE), 1)
    tri = (r2 - c2) * (1 - 2 * dr) >= 0
    lane = lax.broadcasted_iota(jnp.int32, (TILE, AUG), 1)
    one_col = jnp.where(lane == 0, 1.0, 0.0)
    nt_dims = (((1,), (1,)), ((), ()))
    tn_dims = (((0,), (0,)), ((), ()))
    for hh in range(HEADS):
        sl = slice(hh * dh, (hh + 1) * dh)
        q = q_ref[:, sl]
        k = k_ref[:, sl]
        v = v_ref[:, sl]
        b_col = bc[:, HEADS + hh:HEADS + hh + 1]
        li_col = x[:, hh:hh + 1]
        row_vec = xt[hh:hh + 1, :] - br[HEADS + hh:HEADS + hh + 1, :]
        b_end = tot[:, HEADS + hh:HEADS + hh + 1]
        m_prev = m_s[hh][0:1, 0:1]
        col_vec = -jnp.maximum(m_prev, cmax[:, hh:hh + 1])
        m_t = b_col - col_vec
        dmat = jnp.exp(jnp.where(tri, row_vec + col_vec, NEG))
        sc = lax.dot_general(q, k, nt_dims, preferred_element_type=F32) * dmat
        w_inter = jnp.exp(m_prev + col_vec)
        cta = cta_s[hh]
        v_aug = jnp.concatenate([v, one_col.astype(BF16)], axis=1)
        nd = (jnp.dot(sc.astype(BF16), v_aug, preferred_element_type=F32)
              + w_inter * jnp.dot(q, cta.astype(BF16), preferred_element_type=F32))
        den = nd[:, dh:dh + 1]
        h = nd[:, 0:dh] / jnp.maximum(jnp.abs(den), jnp.exp(-m_t))
        h_ref[:, sl] = h.astype(h_ref.dtype)
        g_col = b_end - b_col + li_col
        m_new = jnp.maximum(b_end + m_prev, jnp.max(g_col, axis=0, keepdims=True))
        wj = jnp.exp(g_col - m_new)
        decay = jnp.exp(b_end + m_prev - m_new)
        vw = jnp.concatenate([(v.astype(F32) * wj).astype(BF16), (one_col * wj).astype(BF16)], axis=1)
        cta_s[hh] = decay * cta + lax.dot_general(k, vw, tn_dims, preferred_element_type=F32)
        m_s[hh] = jnp.broadcast_to(m_new, m_s.shape[1:])


def _mlstm_scan(q, k, px, col_v, gates, gates_t, bsz, ntb):
    t, d = q.shape
    dh = d // HEADS
    gh = 2 * HEADS

    def chunk(b, dr, i):
        c = jnp.where(dr == 0, i, jnp.where(i == 0, 0, ntb - i))
        return b * ntb + c

    return pl.pallas_call(
        _scan_kernel,
        grid=(bsz, 2, ntb),
        in_specs=[
            pl.BlockSpec((TILE, d), lambda b, dr, i: (chunk(b, dr, i), 0)),
            pl.BlockSpec((TILE, d), lambda b, dr, i: (chunk(b, dr, i), 0)),
            pl.BlockSpec((TILE, d), lambda b, dr, i: (chunk(b, dr, i), col_v)),
            pl.BlockSpec((None, TILE, gh), lambda b, dr, i: (dr, chunk(b, dr, i), 0)),
            pl.BlockSpec((None, gh, TILE), lambda b, dr, i: (dr, 0, chunk(b, dr, i))),
        ],
        out_specs=pl.BlockSpec((None, TILE, d), lambda b, dr, i: (dr, chunk(b, dr, i), 0)),
        out_shape=jax.ShapeDtypeStruct((2, t, d), BF16),
        scratch_shapes=[
            pltpu.VMEM((HEADS, dh, dh + AUG), F32),
            pltpu.VMEM((HEADS, 8, 128), F32),
        ],
        compiler_params=_cparams(("arbitrary", "arbitrary", "arbitrary")),
        name="mlstm_scan",
    )(q, k, px, gates, gates_t)


def _conv_kernel(u_ref, dw_ref, db_ref, lg_ref, lb_ref, wo_ref, o_ref, pad_s):
    i = pl.program_id(1)
    dc = dw_ref.shape[1]
    u = u_ref[...].astype(F32)
    a = u[:, :dc] * _sigmoid(u[:, dc:])
    zeros = jnp.zeros((CONV_HALO, dc), F32)

    def depthwise(seg_len, n_seg):
        stride = seg_len + 2 * CONV_HALO
        used = n_seg * stride
        for s in range(n_seg):
            base = s * stride
            pad_s[0, base:base + CONV_HALO, :] = zeros
            pad_s[0, base + CONV_HALO:base + CONV_HALO + seg_len, :] = a[s * seg_len:(s + 1) * seg_len, :]
            pad_s[0, base + CONV_HALO + seg_len:base + stride, :] = zeros
        for rho in range(1, SUBLANES):
            pad_s[rho, 0:used - SUBLANES, :] = pad_s[0, rho:used - SUBLANES + rho, :]
        outs = []
        for s in range(n_seg):
            base = s * stride
            acc = jnp.zeros((seg_len, dc), F32)
            for j in range(CONV_WIDTH):
                off = base + CONV_HALO - CONV_WIDTH // 2 + j
                rho = off % SUBLANES
                acc = acc + dw_ref[j:j + 1, :] * pad_s[rho, off - rho:off - rho + seg_len, :]
            outs.append(acc)
        return outs[0] if n_seg == 1 else jnp.concatenate(outs, axis=0)

    def finish(acc):
        acc = acc + db_ref[...]
        mu = jnp.mean(acc, axis=-1, keepdims=True)
        var = jnp.mean(jnp.square(acc - mu), axis=-1, keepdims=True)
        y = (acc - mu) * lax.rsqrt(var + LN_EPS) * lg_ref[...] + lb_ref[...]
        y = y * _sigmoid(y)
        o_ref[...] = jnp.dot(y.astype(BF16), wo_ref[...], preferred_element_type=F32).astype(o_ref.dtype)

    @pl.when(i == 0)
    def _():
        finish(depthwise(TILE, 1))

    @pl.when(i > 0)
    def _():
        finish(depthwise(GRID_W, TILE // GRID_W))


def _conv_module(px, col, dw_w, dw_b, ln_g, ln_b, w_o_bf, bsz, ntb):
    t = px.shape[0]
    dc, d = w_o_bf.shape
    return pl.pallas_call(
        _conv_kernel,
        grid=(bsz, ntb),
        in_specs=[
            pl.BlockSpec((TILE, 2 * dc), lambda b, i: (b * ntb + i, col)),
            pl.BlockSpec((CONV_WIDTH, dc), lambda b, i: (0, 0)),
            pl.BlockSpec((1, dc), lambda b, i: (0, 0)),
            pl.BlockSpec((1, dc), lambda b, i: (0, 0)),
            pl.BlockSpec((1, dc), lambda b, i: (0, 0)),
            pl.BlockSpec((dc, d), lambda b, i: (0, 0)),
        ],
        out_specs=pl.BlockSpec((TILE, d), lambda b, i: (b * ntb + i, 0)),
        out_shape=jax.ShapeDtypeStruct((t, d), BF16),
        scratch_shapes=[pltpu.VMEM((SUBLANES, (TILE // GRID_W) * (GRID_W + 2 * CONV_HALO), dc), F32)],
        compiler_params=_cparams(("arbitrary", "arbitrary")),
        name="conv_module",
    )(px, dw_w, dw_b, ln_g, ln_b, w_o_bf)


def _pool_kernel(ctx_len, u_ref, pw_ref, ps_ref, wo_ref, o_ref, d_s):
    nt, dp = u_ref.shape
    pg = dp // len(POOL_WINDOWS)
    rows = (nt - ctx_len) // GRID_W
    uf = u_ref[...].astype(F32)
    tq = lax.broadcasted_iota(jnp.int32, (ctx_len, ctx_len), 0)
    tk = lax.broadcasted_iota(jnp.int32, (ctx_len, ctx_len), 1)
    for g, w in enumerate(POOL_WINDOWS):
        cs = slice(g * pg, (g + 1) * pg)
        lo = jnp.maximum(tq - w // 2, 0)
        hi = jnp.minimum(tq + w // 2 - 1, ctx_len - 1)
        band = jnp.where(tk >= lo, jnp.where(tk <= hi, 1.0, 0.0), 0.0).astype(BF16)
        ssum = jnp.dot(band, u_ref[0:ctx_len, cs], preferred_element_type=F32)
        cnt = (hi - lo + 1)[:, 0:1].astype(F32)
        dd = ssum / cnt - uf[0:ctx_len, cs]
        y = jnp.dot(dd.astype(BF16), pw_ref[g], preferred_element_type=F32)
        d_s[0:ctx_len, cs] = (y * ps_ref[:, cs]).astype(BF16)
    ridx = lax.broadcasted_iota(jnp.int32, (rows, GRID_W, pg), 0)
    for g, w in enumerate(POOL_WINDOWS):
        cs = slice(g * pg, (g + 1) * pg)
        xg = uf[ctx_len:, cs].reshape(rows, GRID_W, pg)
        acc = xg
        for dlt in range(-(w // 2), w // 2):
            if dlt == 0:
                continue
            z = jnp.zeros((abs(dlt), GRID_W, pg), F32)
            if dlt > 0:
                acc = acc + jnp.concatenate([xg[dlt:], z], axis=0)
            else:
                acc = acc + jnp.concatenate([z, xg[:dlt]], axis=0)
        cnt = (jnp.minimum(ridx + w // 2 - 1, rows - 1) - jnp.maximum(ridx - w // 2, 0) + 1).astype(F32)
        dd = (acc / cnt - xg).reshape(rows * GRID_W, pg)
        y = jnp.dot(dd.astype(BF16), pw_ref[g], preferred_element_type=F32)
        d_s[ctx_len:, cs] = (y * ps_ref[:, cs]).astype(BF16)
    o_ref[...] = jnp.dot(d_s[...], wo_ref[...], preferred_element_type=F32).astype(o_ref.dtype)


def _pool_module(px, col, pool_w_bf, pool_scale, w_o_bf, bsz, nt, ctx_len):
    t = px.shape[0]
    dp, d = w_o_bf.shape
    ng = len(POOL_WINDOWS)
    return pl.pallas_call(
        functools.partial(_pool_kernel, ctx_len),
        grid=(bsz,),
        in_specs=[
            pl.BlockSpec((nt, dp), lambda b: (b, col)),
            pl.BlockSpec((ng, dp // ng, dp // ng), lambda b: (0, 0, 0)),
            pl.BlockSpec((1, dp), lambda b: (0, 0)),
            pl.BlockSpec((dp, d), lambda b: (0, 0)),
        ],
        out_specs=pl.BlockSpec((nt, d), lambda b: (b, 0)),
        out_shape=jax.ShapeDtypeStruct((t, d), BF16),
        scratch_shapes=[pltpu.VMEM((nt, dp), BF16)],
        compiler_params=_cparams(("arbitrary",)),
        name="pool_module",
    )(px, pool_w_bf, pool_scale, w_o_bf)


def _mix_out_kernel(x_ref, gate_ref, z_ref, a_ref, h_ref, yc_ref, yp_ref, mod_ref, ng_ref, skip_ref, wm_ref,
                    wo_ref, g2_ref, rw_ref, rb_ref, xo_ref, hx_ref, lg_ref):
    d = x_ref.shape[1]
    dh = d // HEADS
    hsum = h_ref[0].astype(F32) + h_ref[1].astype(F32)
    parts = []
    for hh in range(HEADS):
        hb = hsum[:, hh * dh:(hh + 1) * dh]
        mu = jnp.mean(hb, axis=-1, keepdims=True)
        var = jnp.mean(jnp.square(hb - mu), axis=-1, keepdims=True)
        parts.append((hb - mu) * lax.rsqrt(var + LN_EPS))
    hn = jnp.concatenate(parts, axis=1)
    z = z_ref[...].astype(F32)
    tm = (hn * ng_ref[...] + skip_ref[...] * a_ref[...].astype(F32)) * (z * _sigmoid(z))
    y_m = jnp.dot(tm.astype(BF16), wm_ref[...], preferred_element_type=F32)
    merged = (_sigmoid(gate_ref[:, 0:d].astype(F32)) * yc_ref[...].astype(F32)
              + _sigmoid(gate_ref[:, d:2 * d].astype(F32)) * y_m
              + _sigmoid(gate_ref[:, 2 * d:3 * d].astype(F32)) * yp_ref[...].astype(F32))
    y = jnp.dot(merged.astype(BF16), wo_ref[...], preferred_element_type=F32)
    x = x_ref[...] + mod_ref[2:3, :] * y
    xo_ref[...] = x
    ms = jnp.mean(x * x, axis=-1, keepdims=True)
    hx = x * lax.rsqrt(ms + RMS_EPS) * g2_ref[...]
    hx = hx * (1.0 + mod_ref[4:5, :]) + mod_ref[3:4, :]
    hx_ref[...] = hx.reshape(hx_ref.shape)
    lg_ref[...] = lax.dot_general(rw_ref[...], hx.astype(BF16), (((1,), (1,)), ((), ())),
                                  preferred_element_type=F32) + rb_ref[...]


def _mix_out(xt, px, col_z, a, h2, y_conv, y_pool, modtab, norm_g, skip, w_m_bf, w_out_bf, g2, router_w, router_b,
             ntb, mod_base):
    t, d = xt.shape
    ne = router_w.shape[0]
    row = _mod_row(ntb, mod_base)
    vec = lambda: pl.BlockSpec((1, d), lambda i: (0, 0))
    return pl.pallas_call(
        _mix_out_kernel,
        grid=(t // TILE,),
        in_specs=[
            pl.BlockSpec((TILE, d), lambda i: (i, 0)),
            pl.BlockSpec((TILE, 3 * d), lambda i: (i, 0)),
            pl.BlockSpec((TILE, d), lambda i: (i, col_z)),
            pl.BlockSpec((TILE, d), lambda i: (i, 0)),
            pl.BlockSpec((2, TILE, d), lambda i: (0, i, 0)),
            pl.BlockSpec((TILE, d), lambda i: (i, 0)),
            pl.BlockSpec((TILE, d), lambda i: (i, 0)),
            pl.BlockSpec((None, 6, d), lambda i: (row(i), 0, 0)),
            vec(), vec(),
            pl.BlockSpec((d, d), lambda i: (0, 0)),
            pl.BlockSpec((d, d), lambda i: (0, 0)),
            vec(),
            pl.BlockSpec((ne, d), lambda i: (0, 0)),
            pl.BlockSpec((ne, 1), lambda i: (0, 0)),
        ],
        out_specs=[
            pl.BlockSpec((TILE, d), lambda i: (i, 0)),
            pl.BlockSpec((TILE, SUBLANES, d // SUBLANES), lambda i: (i, 0, 0)),
            pl.BlockSpec((ne, TILE), lambda i: (0, i)),
        ],
        out_shape=[
            jax.ShapeDtypeStruct((t, d), F32),
            jax.ShapeDtypeStruct((t, SUBLANES, d // SUBLANES), F32),
            jax.ShapeDtypeStruct((ne, t), F32),
        ],
        compiler_params=_cparams(("arbitrary",)),
        name="mix_out",
    )(xt, px, px, a, h2, y_conv, y_pool, modtab, norm_g, skip, w_m_bf, w_out_bf, g2, router_w, router_b)


def _expert_kernel(be_ref, nu_ref, tok_ref, tokn_ref, hx_hbm, wgu_ref, bgu_ref, wdn_ref, bdn_ref, y_ref,
                   xbuf_a, xbuf_b, xbuf_c, xbuf_d, sem_a, sem_b, sem_c, sem_d, xb_s, act_s, wgu_s, wdn_s):
    i = pl.program_id(0)
    n_used = nu_ref[0]
    half = xbuf_a.shape[0]
    dff = wdn_s.shape[0]
    n_chunk = dff // MOE_CHUNK

    def row_copy(tok_smem, r, xbuf, sem):
        t = tok_smem[0, 0, r]
        return pltpu.make_async_copy(hx_hbm.at[t], xbuf.at[r % half], sem.at[0])

    def wait_rows(xbuf, sem):
        pltpu.make_async_copy(hx_hbm.at[pl.ds(0, half)], xbuf, sem.at[0]).wait()

    @pl.when(i == 0)
    def _():
        def body(r, carry):
            row_copy(tok_ref, r, xbuf_a, sem_a).start()
            row_copy(tok_ref, half + r, xbuf_b, sem_b).start()
            return carry
        lax.fori_loop(0, half, body, 0)

    e_now = be_ref[i]
    e_prev = be_ref[jnp.maximum(i - 1, 0)]

    @pl.when(jnp.logical_or(i == 0, e_now != e_prev))
    def _():
        wgu_s[...] = wgu_ref[...].astype(BF16)
        wdn_s[...] = wdn_ref[...].astype(BF16)

    def half_block(x_cur, sem_cur, out_rows, prefetch):
        wait_rows(x_cur, sem_cur)
        xb_s[...] = x_cur[...].reshape(xb_s.shape).astype(BF16)
        for c in range(n_chunk):
            for r, (xbuf, sem) in (prefetch[c] if prefetch else ()):
                row_copy(tokn_ref, r, xbuf, sem).start(priority=r % 2)
            cg = slice(c * MOE_CHUNK, (c + 1) * MOE_CHUNK)
            cu = slice(dff + c * MOE_CHUNK, dff + (c + 1) * MOE_CHUNK)
            g = jnp.dot(xb_s[...], wgu_s[:, cg], preferred_element_type=F32) + bgu_ref[:, cg]
            u = jnp.dot(xb_s[...], wgu_s[:, cu], preferred_element_type=F32) + bgu_ref[:, cu]
            g = jnp.minimum(g, SWIGLU_LIMIT)
            u = jnp.clip(u, -SWIGLU_LIMIT, SWIGLU_LIMIT)
            act_s[:, cg] = ((u + 1.0) * (g * _sigmoid(SWIGLU_ALPHA * g))).astype(BF16)
        y = jnp.dot(act_s[...], wdn_s[...], preferred_element_type=F32)
        y_ref[out_rows] = (y + bdn_ref[...]).reshape((half,) + y_ref.shape[1:])

    used = i < n_used
    even = i % 2 == 0

    def block(cur, nxt):
        (xa, sa), (xb, sb) = cur
        per = 2 * half // n_chunk
        prefetch = [[(r, nxt[r // half]) for r in range(c * per, (c + 1) * per)] for c in range(n_chunk)]
        half_block(xa, sa, slice(0, half), prefetch)
        half_block(xb, sb, slice(half, 2 * half), None)

    bufs_even = ((xbuf_a, sem_a), (xbuf_b, sem_b))
    bufs_odd = ((xbuf_c, sem_c), (xbuf_d, sem_d))

    @pl.when(jnp.logical_and(used, even))
    def _():
        block(bufs_even, bufs_odd)

    @pl.when(jnp.logical_and(used, jnp.logical_not(even)))
    def _():
        block(bufs_odd, bufs_even)

    @pl.when(jnp.logical_not(used))
    def _():
        @pl.when(jnp.logical_and(i == n_used, even))
        def _():
            wait_rows(xbuf_a, sem_a)
            wait_rows(xbuf_b, sem_b)

        @pl.when(jnp.logical_and(i == n_used, jnp.logical_not(even)))
        def _():
            wait_rows(xbuf_c, sem_c)
            wait_rows(xbuf_d, sem_d)

        y_ref[...] = jnp.zeros(y_ref.shape, y_ref.dtype)


def _experts(layer, hx, row_tok, block_e, n_used, w_gu, b_gu, w_dn, b_dn):
    t, sub, dl = hx.shape
    d = sub * dl
    depth, ne, _, dff2 = w_gu.shape
    dff = dff2 // 2
    nb = block_e.shape[0]
    bm = MOE_BM
    half = bm // 2
    tok3 = row_tok.reshape(nb, 1, bm)
    grid_spec = pltpu.PrefetchScalarGridSpec(
        num_scalar_prefetch=2,
        grid=(nb,),
        in_specs=[
            pl.BlockSpec((1, 1, bm), lambda i, be, nu: (i, 0, 0), memory_space=pltpu.SMEM),
            pl.BlockSpec((1, 1, bm), lambda i, be, nu: (jnp.minimum(i + 1, nb - 1), 0, 0),
                         memory_space=pltpu.SMEM),
            pl.BlockSpec(memory_space=pl.ANY),
            pl.BlockSpec((None, None, d, dff2), lambda i, be, nu: (layer, be[i], 0, 0)),
            pl.BlockSpec((None, None, 1, dff2), lambda i, be, nu: (layer, be[i], 0, 0)),
            pl.BlockSpec((None, None, dff, d), lambda i, be, nu: (layer, be[i], 0, 0)),
            pl.BlockSpec((None, None, 1, d), lambda i, be, nu: (layer, be[i], 0, 0)),
        ],
        out_specs=pl.BlockSpec((bm, sub, dl), lambda i, be, nu: (i, 0, 0)),
        scratch_shapes=[
            pltpu.VMEM((half, sub, dl), F32),
            pltpu.VMEM((half, sub, dl), F32),
            pltpu.VMEM((half, sub, dl), F32),
            pltpu.VMEM((half, sub, dl), F32),
            pltpu.SemaphoreType.DMA((1,)),
            pltpu.SemaphoreType.DMA((1,)),
            pltpu.SemaphoreType.DMA((1,)),
            pltpu.SemaphoreType.DMA((1,)),
            pltpu.VMEM((half, d), BF16),
            pltpu.VMEM((half, dff), BF16),
            pltpu.VMEM((d, dff2), BF16),
            pltpu.VMEM((dff, d), BF16),
        ],
    )
    return pl.pallas_call(
        _expert_kernel,
        grid_spec=grid_spec,
        out_shape=jax.ShapeDtypeStruct((nb * bm, sub, dl), F32),
        compiler_params=_cparams(("arbitrary",), no_bounds_checks=True),
        name="moe_experts",
    )(block_e, n_used, tok3, tok3, hx, w_gu, b_gu.reshape(depth, ne, 1, dff2), w_dn,
      b_dn.reshape(depth, ne, 1, d))


def _combine_kernel(dst_ref, dstn_ref, ys_hbm, gw_ref, x_ref, mod_ref, o_ref, buf, sem):
    i = pl.program_id(0)
    nsteps = pl.num_programs(0)
    slot = i % 2
    n_copy = TOP_K * TILE

    def row_copy(dst_smem, e, sl):
        src = dst_smem[0, 0, e]
        kk = e // TILE
        r = e % TILE
        return pltpu.make_async_copy(ys_hbm.at[src], buf.at[sl, kk, r], sem.at[sl])

    @pl.when(i == 0)
    def _():
        def body(e, carry):
            row_copy(dst_ref, e, 0).start()
            return carry
        lax.fori_loop(0, n_copy, body, 0)

    @pl.when(i + 1 < nsteps)
    def _():
        for e in range(n_copy):
            row_copy(dstn_ref, e, 1 - slot).start(priority=e % 2)

    for kk in range(TOP_K):
        pltpu.make_async_copy(ys_hbm.at[pl.ds(0, TILE)], buf.at[slot, kk], sem.at[slot]).wait()

    gw = gw_ref[...]
    y = gw[:, 0:1] * buf[slot, 0].reshape(x_ref.shape)
    for kk in range(1, TOP_K):
        y = y + gw[:, kk:kk + 1] * buf[slot, kk].reshape(x_ref.shape)
    o_ref[...] = x_ref[...] + mod_ref[5:6, :] * y


def _combine(ys, dest_tiles, gate_w, xt, modtab, ntb, mod_base):
    t, d = xt.shape
    nt = t // TILE
    row = _mod_row(ntb, mod_base)
    return pl.pallas_call(
        _combine_kernel,
        grid=(nt,),
        in_specs=[
            pl.BlockSpec((1, 1, TOP_K * TILE), lambda i: (i, 0, 0), memory_space=pltpu.SMEM),
            pl.BlockSpec((1, 1, TOP_K * TILE), lambda i: (jnp.minimum(i + 1, nt - 1), 0, 0),
                         memory_space=pltpu.SMEM),
            pl.BlockSpec(memory_space=pl.ANY),
            pl.BlockSpec((TILE, TOP_K), lambda i: (i, 0)),
            pl.BlockSpec((TILE, d), lambda i: (i, 0)),
            pl.BlockSpec((None, 6, d), lambda i: (row(i), 0, 0)),
        ],
        out_specs=pl.BlockSpec((TILE, d), lambda i: (i, 0)),
        out_shape=jax.ShapeDtypeStruct((t, d), F32),
        scratch_shapes=[
            pltpu.VMEM((2, TOP_K, TILE) + ys.shape[1:], F32),
            pltpu.SemaphoreType.DMA((2,)),
        ],
        compiler_params=_cparams(("arbitrary",), no_bounds_checks=True),
        name="moe_combine",
    )(dest_tiles, dest_tiles, ys, gate_w, xt, modtab)


IN_CHUNKS = 6
IN_DMA_CHUNKS = 4


def _comb_in_kernel(dst_ref, dstn_ref, ys_hbm, gw_ref, x_ref, modp_ref, g_ref, mod_ref, w_ref, xo_ref, o_ref,
                    buf_a, buf_b, sem_a, sem_b, fsem, hb_s):
    i = pl.program_id(0)
    nsteps = pl.num_programs(0)
    n_copy = TOP_K * TILE
    n = w_ref.shape[1]
    cw = n // IN_CHUNKS
    per = n_copy // IN_DMA_CHUNKS

    def row_copy(dst_smem, e, buf, sem):
        src = dst_smem[0, 0, e]
        return pltpu.make_async_copy(ys_hbm.at[src], buf.at[e // TILE, e % TILE], sem.at[0])

    def wait_rows(buf, sem):
        for kk in range(TOP_K):
            pltpu.make_async_copy(ys_hbm.at[pl.ds(0, TILE)], buf.at[kk], sem.at[0]).wait()

    @pl.when(i == 0)
    def _():
        def body(e, carry):
            row_copy(dst_ref, e, buf_a, sem_a).start()
            return carry
        lax.fori_loop(0, n_copy, body, 0)

    def tile(buf_cur, sem_cur, buf_nxt, sem_nxt):
        wait_rows(buf_cur, sem_cur)
        gw = gw_ref[...]
        y = gw[:, 0:1] * buf_cur[0].reshape(x_ref.shape)
        for kk in range(1, TOP_K):
            y = y + gw[:, kk:kk + 1] * buf_cur[kk].reshape(x_ref.shape)
        x = x_ref[...] + modp_ref[5:6, :] * y
        xo_ref[...] = x
        ms = jnp.mean(x * x, axis=-1, keepdims=True)
        h = x * lax.rsqrt(ms + RMS_EPS) * g_ref[...]
        h = h * (1.0 + mod_ref[1:2, :]) + mod_ref[0:1, :]
        hb_s[...] = h.astype(BF16)
        for c in range(IN_CHUNKS):
            if c < IN_DMA_CHUNKS:
                for e in range(c * per, (c + 1) * per):
                    row_copy(dstn_ref, e, buf_nxt, sem_nxt).start(priority=e % 2)
            if c == IN_DMA_CHUNKS:
                pl.semaphore_signal(fsem, 1)
                pl.semaphore_wait(fsem, 1)
            cs = slice(c * cw, (c + 1) * cw)
            o_ref[:, cs] = jnp.dot(hb_s[...], w_ref[:, cs], preferred_element_type=F32).astype(o_ref.dtype)

    even = i % 2 == 0

    @pl.when(even)
    def _():
        tile(buf_a, sem_a, buf_b, sem_b)

    @pl.when(jnp.logical_not(even))
    def _():
        tile(buf_b, sem_b, buf_a, sem_a)

    @pl.when(jnp.logical_and(i == nsteps - 1, even))
    def _():
        wait_rows(buf_b, sem_b)

    @pl.when(jnp.logical_and(i == nsteps - 1, jnp.logical_not(even)))
    def _():
        wait_rows(buf_a, sem_a)


def _comb_in_proj(ys, dest_tiles, gate_w, xt, g1, modtab, w_in_bf, ntb, mod_base_prev, mod_base):
    t, d = xt.shape
    n = w_in_bf.shape[1]
    nt = t // TILE
    assert n % (IN_CHUNKS * 256) == 0 and (TOP_K * TILE) % IN_DMA_CHUNKS == 0
    row_prev = _mod_row(ntb, mod_base_prev)
    row = _mod_row(ntb, mod_base)
    return pl.pallas_call(
        _comb_in_kernel,
        grid=(nt,),
        in_specs=[
            pl.BlockSpec((1, 1, TOP_K * TILE), lambda i: (i, 0, 0), memory_space=pltpu.SMEM),
            pl.BlockSpec((1, 1, TOP_K * TILE), lambda i: (jnp.minimum(i + 1, nt - 1), 0, 0),
                         memory_space=pltpu.SMEM),
            pl.BlockSpec(memory_space=pl.ANY),
            pl.BlockSpec((TILE, TOP_K), lambda i: (i, 0)),
            pl.BlockSpec((TILE, d), lambda i: (i, 0)),
            pl.BlockSpec((None, 6, d), lambda i: (row_prev(i), 0, 0)),
            pl.BlockSpec((1, d), lambda i: (0, 0)),
            pl.BlockSpec((None, 6, d), lambda i: (row(i), 0, 0)),
            pl.BlockSpec((d, n), lambda i: (0, 0), pipeline_mode=pl.Buffered(1)),
        ],
        out_specs=[
            pl.BlockSpec((TILE, d), lambda i: (i, 0)),
            pl.BlockSpec((TILE, n), lambda i: (i, 0)),
        ],
        out_shape=[
            jax.ShapeDtypeStruct((t, d), F32),
            jax.ShapeDtypeStruct((t, n), BF16),
        ],
        scratch_shapes=[
            pltpu.VMEM((TOP_K, TILE) + ys.shape[1:], F32),
            pltpu.VMEM((TOP_K, TILE) + ys.shape[1:], F32),
            pltpu.SemaphoreType.DMA((1,)),
            pltpu.SemaphoreType.DMA((1,)),
            pltpu.SemaphoreType.REGULAR,
            pltpu.VMEM((TILE, d), BF16),
        ],
        compiler_params=_cparams(("arbitrary",), no_bounds_checks=True),
        name="comb_in_proj",
    )(dest_tiles, dest_tiles, ys, gate_w, xt, modtab, g1, modtab, w_in_bf)


def _final_kernel(x_ref, g_ref, o_ref):
    x = x_ref[...]
    ms = jnp.mean(x * x, axis=-1, keepdims=True)
    o_ref[...] = x * lax.rsqrt(ms + RMS_EPS) * g_ref[...]


def _final_norm(xt, final_g, bsz, ntb):
    t, d = xt.shape
    nlat = ntb - 1
    return pl.pallas_call(
        _final_kernel,
        grid=(bsz, nlat),
        in_specs=[
            pl.BlockSpec((TILE, d), lambda b, j: (b * ntb + 1 + j, 0)),
            pl.BlockSpec((1, d), lambda b, j: (0, 0)),
        ],
        out_specs=pl.BlockSpec((TILE, d), lambda b, j: (b * nlat + j, 0)),
        out_shape=jax.ShapeDtypeStruct((bsz * nlat * TILE, d), F32),
        compiler_params=_cparams(("arbitrary", "arbitrary")),
        name="final_norm",
    )(xt, final_g)


def _route_kernel(lg_ref, te_ref, gw_ref, pos_ref, cnt_ref, cnt_s):
    i = pl.program_id(0)

    @pl.when(i == 0)
    def _():
        cnt_s[...] = jnp.zeros(cnt_s.shape, F32)

    lg = lg_ref[...]
    ne = lg.shape[0]
    sub = lax.broadcasted_iota(jnp.int32, lg.shape, 0)
    work = lg
    vals, idxs, hots = [], [], []
    for _ in range(TOP_K):
        m = jnp.max(work, axis=0, keepdims=True)
        idx = jnp.min(jnp.where(work == m, sub, ne), axis=0, keepdims=True)
        hot = sub == idx
        vals.append(m)
        idxs.append(idx)
        hots.append(jnp.where(hot, 1.0, 0.0))
        work = jnp.where(hot, -jnp.inf, work)
    exps = [jnp.exp(v - vals[0]) for v in vals]
    ssum = exps[0]
    for e in exps[1:]:
        ssum = ssum + e
    hot_all = hots[0]
    for h in hots[1:]:
        hot_all = hot_all + h
    r2 = lax.broadcasted_iota(jnp.int32, (TILE, TILE), 0)
    c2 = lax.broadcasted_iota(jnp.int32, (TILE, TILE), 1)
    earlier = jnp.where(r2 < c2, 1.0, 0.0).astype(BF16)
    before = jnp.dot(hot_all.astype(BF16), earlier, preferred_element_type=F32) + cnt_s[...]
    for kk in range(TOP_K):
        te_ref[kk:kk + 1, :] = idxs[kk]
        gw_ref[kk:kk + 1, :] = exps[kk] / ssum
        pos_ref[kk:kk + 1, :] = jnp.sum(hots[kk] * before, axis=0, keepdims=True).astype(jnp.int32)
    cnt_s[...] = cnt_s[...] + jnp.sum(hot_all, axis=1, keepdims=True)
    cnt_ref[...] = cnt_s[...]


def _route(logits_t):
    ne, t = logits_t.shape
    return pl.pallas_call(
        _route_kernel,
        grid=(t // TILE,),
        in_specs=[pl.BlockSpec((ne, TILE), lambda i: (0, i))],
        out_specs=[
            pl.BlockSpec((TOP_K, TILE), lambda i: (0, i)),
            pl.BlockSpec((TOP_K, TILE), lambda i: (0, i)),
            pl.BlockSpec((TOP_K, TILE), lambda i: (0, i)),
            pl.BlockSpec((ne, 1), lambda i: (0, 0)),
        ],
        out_shape=[
            jax.ShapeDtypeStruct((TOP_K, t), jnp.int32),
            jax.ShapeDtypeStruct((TOP_K, t), F32),
            jax.ShapeDtypeStruct((TOP_K, t), jnp.int32),
            jax.ShapeDtypeStruct((ne, 1), F32),
        ],
        scratch_shapes=[pltpu.VMEM((ne, 1), F32)],
        compiler_params=_cparams(("arbitrary",)),
        name="route",
    )(logits_t)


def _routing(logits_t, n_tiles):
    t = logits_t.shape[1]
    top_e, gate, pos, cnt = _route(logits_t)
    n_assign = t * TOP_K
    experts = jnp.arange(N_EXPERTS, dtype=jnp.int32)
    key = (top_e * t + jnp.arange(t, dtype=jnp.int32)[None, :]).reshape(-1)
    order = jnp.argsort(key).astype(jnp.int32)
    counts = cnt.reshape(N_EXPERTS).astype(jnp.int32)
    grp_start = jnp.cumsum(counts) - counts
    padded = (counts + MOE_BM - 1) // MOE_BM * MOE_BM
    pad_end = jnp.cumsum(padded)
    pad_start = pad_end - padded
    dest = pos + jnp.sum(jnp.where(top_e[:, :, None] == experts[None, None, :], pad_start[None, None, :], 0), axis=2)
    n_blocks = n_assign // MOE_BM + N_EXPERTS
    blk = jnp.arange(n_blocks, dtype=jnp.int32)
    block_e = jnp.minimum(jnp.sum((pad_end[None, :] <= (blk * MOE_BM)[:, None]).astype(jnp.int32), axis=1),
                          N_EXPERTS - 1)
    src0 = grp_start[block_e] + blk * MOE_BM - pad_start[block_e]
    lim = grp_start[block_e] + counts[block_e]
    src = src0[:, None] + jnp.arange(MOE_BM, dtype=jnp.int32)[None, :]
    valid = src < lim[:, None]
    row_tok = jnp.where(valid, order[jnp.clip(src, 0, n_assign - 1)] % t, 0).astype(jnp.int32).reshape(-1)
    n_used = (pad_end[-1] // MOE_BM).astype(jnp.int32).reshape(1)
    dest_tiles = dest.reshape(TOP_K, n_tiles, TILE).transpose(1, 0, 2).reshape(n_tiles, 1, TOP_K * TILE)
    return gate.T, row_tok, block_e, n_used, dest_tiles


def _block_diag(w, heads):
    nb, c, _ = w.shape
    per = nb // heads
    w = w.reshape(heads, per, c, c)
    eye = jnp.eye(per, dtype=w.dtype)
    full = eye[None, :, None, :, None] * w[:, :, :, None, :]
    return full.reshape(heads, per * c, per * c)


def kernel(x, c, ctx, c_ctx, w_ada, b_ada, norm1_g, norm2_g, w_in, conv_dw_w, conv_dw_b, conv_ln_g, conv_ln_b, w_conv_out, m_sc_w, m_sc_b, m_wq, m_wk, m_w_if, m_b_if, m_norm_g, m_skip, w_m_out, pool_w, pool_scale, w_pool_out, w_out, router_w, router_b, w_gu, b_gu, w_dn, b_dn, final_g):
    bsz, seq, d = x.shape
    ctx_len = ctx.shape[1]
    depth = w_ada.shape[0]
    assert ctx_len == TILE and seq % TILE == 0 and d % (128 * HEADS) == 0
    nt = ctx_len + seq
    ntb = nt // TILE
    t = bsz * nt
    n_tiles = t // TILE
    d_conv = w_conv_out.shape[1]
    d_pool = w_pool_out.shape[1]
    d_m = w_m_out.shape[1]
    assert d_m == d and 2 * d_conv == d and 2 * d_pool == d
    c_conv = 3 * d
    c_pool = c_conv + 2 * d_conv
    c_mz = c_pool + d_pool
    c_mqk = c_mz + d_m
    c_mv = c_mqk + d_m
    d_in = c_mv + d_m
    col_conv, col_z, col_u, col_v = 3, 4, 5, 6
    col_pool = (7 * d) // d_pool

    xt = jnp.concatenate([ctx, x], axis=1).reshape(t, d)

    mp = -(-(bsz + 1) // 8) * 8
    c_all = jnp.zeros((mp, d), F32).at[:bsz].set(c).at[bsz].set(c_ctx)
    mod = _ada(c_all, w_ada, b_ada)
    mod_x = mod[:, :bsz].reshape(depth, bsz, 1, 6, d)
    mod_c = jnp.broadcast_to(mod[:, bsz].reshape(depth, 1, 1, 6, d), (depth, bsz, 1, 6, d))
    modtab = jnp.concatenate([mod_c, mod_x], axis=2).reshape(depth * bsz * 2, 6, d)

    gh = 2 * HEADS
    for l in range(depth):
        mod_base = l * bsz * 2
        w_in_bf = jnp.concatenate([w_in[l, :, 0:c_pool], w_in[l, :, c_mz:d_in], w_in[l, :, c_pool:c_mz]],
                                  axis=1).astype(BF16)
        if l == 0:
            px = _in_proj(xt, norm1_g[l].reshape(1, d), modtab, w_in_bf, ntb, mod_base)
        else:
            xt, px = _comb_in_proj(ys, dest_tiles, gate_w, xt, norm1_g[l].reshape(1, d), modtab, w_in_bf, ntb,
                                   mod_base - bsz * 2, mod_base)

        wq_bd = _block_diag(m_wq[l], HEADS).astype(BF16)
        wk_bd = _block_diag(m_wk[l], HEADS).astype(BF16)
        wif = jnp.concatenate([m_w_if[l, 0], m_w_if[l, 1]], axis=1)
        bif = jnp.concatenate([m_b_if[l, 0], m_b_if[l, 1]], axis=0)
        a, q, k, gates, gates_t = _mlstm_pre(
            px, col_u, col_v, m_sc_w[l], m_sc_b[l].reshape(1, d), wq_bd, wk_bd,
            wif.astype(BF16), wif.T.astype(BF16), bif.reshape(1, 2 * gh), bif.reshape(2 * gh, 1), bsz, ntb)
        h2 = _mlstm_scan(q, k, px, col_v, gates, gates_t, bsz, ntb)

        y_conv = _conv_module(px, col_conv, conv_dw_w[l], conv_dw_b[l].reshape(1, d_conv),
                              conv_ln_g[l].reshape(1, d_conv), conv_ln_b[l].reshape(1, d_conv),
                              w_conv_out[l].astype(BF16), bsz, ntb)
        y_pool = _pool_module(px, col_pool, pool_w[l].astype(BF16), pool_scale[l].reshape(1, d_pool),
                              w_pool_out[l].astype(BF16), bsz, nt, ctx_len)

        xt, hx, logits = _mix_out(
            xt, px, col_z, a, h2, y_conv, y_pool, modtab, m_norm_g[l].reshape(1, d), m_skip[l].reshape(1, d),
            w_m_out[l].astype(BF16), w_out[l].astype(BF16), norm2_g[l].reshape(1, d), router_w[l].T.astype(BF16),
            router_b[l].reshape(N_EXPERTS, 1), ntb, mod_base)

        gate_w, row_tok, block_e, n_used, dest_tiles = _routing(logits, n_tiles)
        ys = _experts(l, hx, row_tok, block_e, n_used, w_gu, b_gu, w_dn, b_dn)

    xt = _combine(ys, dest_tiles, gate_w, xt, modtab, ntb, (depth - 1) * bsz * 2)
    out = _final_norm(xt, final_g.reshape(1, d), bsz, ntb)
    return out.reshape(bsz, seq, d)
```

```python
import functools

import jax
import jax.numpy as jnp
from jax import lax
from jax.experimental import pallas as pl
from jax.experimental.pallas import tpu as pltpu

F32 = jnp.float32
BF16 = jnp.bfloat16
HIGHEST = lax.Precision.HIGHEST

TILE = 256
GRID_W = 64
CONV_WIDTH = 31
SUBLANES = 8
CONV_HALO = 16
SHORT_CONV = 4
HEADS = 4
POOL_WINDOWS = (2, 4, 8, 16)
N_EXPERTS = 32
TOP_K = 4
MOE_BM = 512
MOE_CHUNK = 256
SWIGLU_LIMIT = 7.0
SWIGLU_ALPHA = 1.702
RMS_EPS = 1e-6
LN_EPS = 1e-5
NEG = -1e30
VMEM_LIMIT = 56 * 1024 * 1024


def _cparams(sem, no_bounds_checks=False):
    return pltpu.CompilerParams(dimension_semantics=sem, vmem_limit_bytes=VMEM_LIMIT,
                                disable_bounds_checks=no_bounds_checks)


def _sigmoid(x):
    return jax.nn.sigmoid(x)


def _log_sigmoid(x):
    return jnp.minimum(x, 0.0) - jnp.log1p(jnp.exp(-jnp.abs(x)))


def _ada_kernel(c_ref, w_ref, b_ref, o_ref):
    s = c_ref[...]
    s = s * _sigmoid(s)
    o_ref[...] = jnp.dot(s, w_ref[...], precision=HIGHEST, preferred_element_type=F32) + b_ref[...]


def _ada(c_all, w_ada, b_ada):
    depth, d, n = w_ada.shape
    mp = c_all.shape[0]
    tn = 1536
    return pl.pallas_call(
        _ada_kernel,
        grid=(depth, n // tn),
        in_specs=[
            pl.BlockSpec((mp, d), lambda l, j: (0, 0)),
            pl.BlockSpec((None, d, tn), lambda l, j: (l, 0, j)),
            pl.BlockSpec((None, 1, tn), lambda l, j: (l, 0, j)),
        ],
        out_specs=pl.BlockSpec((None, mp, tn), lambda l, j: (l, 0, j)),
        out_shape=jax.ShapeDtypeStruct((depth, mp, n), F32),
        compiler_params=_cparams(("arbitrary", "arbitrary")),
        name="ada",
    )(c_all, w_ada, b_ada.reshape(depth, 1, n))


def _in_kernel(x_ref, g_ref, mod_ref, w_ref, o_ref):
    x = x_ref[...]
    ms = jnp.mean(x * x, axis=-1, keepdims=True)
    y = x * lax.rsqrt(ms + RMS_EPS) * g_ref[...]
    h = y * (1.0 + mod_ref[1:2, :]) + mod_ref[0:1, :]
    o_ref[...] = jnp.dot(h.astype(BF16), w_ref[...], preferred_element_type=F32).astype(BF16)


def _mod_row(ntb, base):
    def f(i):
        return base + 2 * (i // ntb) + jnp.minimum(i % ntb, 1)
    return f


def _in_proj(xt, g1, modtab, w_in_bf, ntb, mod_base):
    t, d = xt.shape
    n = w_in_bf.shape[1]
    row = _mod_row(ntb, mod_base)
    return pl.pallas_call(
        _in_kernel,
        grid=(t // TILE,),
        in_specs=[
            pl.BlockSpec((TILE, d), lambda i: (i, 0)),
            pl.BlockSpec((1, d), lambda i: (0, 0)),
            pl.BlockSpec((None, 6, d), lambda i: (row(i), 0, 0)),
            pl.BlockSpec((d, n), lambda i: (0, 0), pipeline_mode=pl.Buffered(1)),
        ],
        out_specs=pl.BlockSpec((TILE, n), lambda i: (i, 0)),
        out_shape=jax.ShapeDtypeStruct((t, n), BF16),
        compiler_params=_cparams(("arbitrary",)),
        name="in_proj",
    )(xt, g1, modtab, w_in_bf)


def _mpre_kernel(ntb, u_ref, up_ref, un_ref, v_ref, scw_ref, scb_ref, wq_ref, wk_ref, wif_ref, wift_ref,
                 bif_ref, bift_ref, a_ref, q_ref, k_ref, g_ref, gt_ref):
    i = pl.program_id(1)
    d = u_ref.shape[1]
    dh = d // HEADS
    u = u_ref[...].astype(F32)
    up = jnp.where(i >= 2, up_ref[...].astype(F32)[8:16, :], 0.0)
    un = jnp.where(jnp.logical_and(i >= 1, i <= ntb - 2), un_ref[...].astype(F32)[0:8, :], 0.0)
    ext = jnp.concatenate([up, u, un], axis=0)
    n_ext = TILE + 16
    conv = scb_ref[...]
    for j in range(SHORT_CONV):
        sh = (SHORT_CONV // 2 - j) % n_ext
        shifted = ext if sh == 0 else pltpu.roll(ext, sh, 0)
        conv = conv + scw_ref[j:j + 1, :] * shifted[8:8 + TILE, :]
    a = conv * _sigmoid(conv)
    ab = a.astype(BF16)
    a_ref[...] = ab
    qs, ks = [], []
    for hb in range(HEADS):
        blk = ab[:, hb * dh:(hb + 1) * dh]
        qs.append(jnp.dot(blk, wq_ref[hb], preferred_element_type=F32))
        ks.append(jnp.dot(blk, wk_ref[hb], preferred_element_type=F32))
    q = jnp.concatenate(qs, axis=1)
    k = jnp.concatenate(ks, axis=1)
    qb = q.astype(BF16)
    kb = k.astype(BF16)
    vb = v_ref[...]
    q_ref[...] = (q * (dh ** -0.5)).astype(BF16)
    k_ref[...] = kb
    pre = bif_ref[...]
    pret = bift_ref[...]
    nt_dims = (((1,), (1,)), ((), ()))
    for part, f in enumerate((qb, kb, vb)):
        pre = pre + jnp.dot(f, wif_ref[part * d:(part + 1) * d, :], preferred_element_type=F32)
        pret = pret + lax.dot_general(wift_ref[:, part * d:(part + 1) * d], f, nt_dims,
                                      preferred_element_type=F32)
    col = lax.broadcasted_iota(jnp.int32, pre.shape, 1)
    pre = jnp.where(col % (2 * HEADS) >= HEADS, _log_sigmoid(pre), pre)
    rowi = lax.broadcasted_iota(jnp.int32, pret.shape, 0)
    pret = jnp.where(rowi % (2 * HEADS) >= HEADS, _log_sigmoid(pret), pret)
    g_ref[0] = pre[:, 0:2 * HEADS]
    g_ref[1] = pre[:, 2 * HEADS:4 * HEADS]
    gt_ref[0] = pret[0:2 * HEADS, :]
    gt_ref[1] = pret[2 * HEADS:4 * HEADS, :]


def _mlstm_pre(px, col_u, col_v, sc_w, sc_b, wq_bd, wk_bd, wif, wift, bif, bift, bsz, ntb):
    t = px.shape[0]
    d = sc_w.shape[1]
    hrows = 16
    nh = t // hrows
    per = TILE // hrows

    def tile(b, i):
        return b * ntb + i

    gh = 2 * HEADS
    return pl.pallas_call(
        functools.partial(_mpre_kernel, ntb),
        grid=(bsz, ntb),
        in_specs=[
            pl.BlockSpec((TILE, d), lambda b, i: (tile(b, i), col_u)),
            pl.BlockSpec((hrows, d), lambda b, i: (jnp.maximum(tile(b, i) * per - 1, 0), col_u)),
            pl.BlockSpec((hrows, d), lambda b, i: (jnp.minimum((tile(b, i) + 1) * per, nh - 1), col_u)),
            pl.BlockSpec((TILE, d), lambda b, i: (tile(b, i), col_v)),
            pl.BlockSpec((SHORT_CONV, d), lambda b, i: (0, 0)),
            pl.BlockSpec((1, d), lambda b, i: (0, 0)),
            pl.BlockSpec((HEADS, d // HEADS, d // HEADS), lambda b, i: (0, 0, 0)),
            pl.BlockSpec((HEADS, d // HEADS, d // HEADS), lambda b, i: (0, 0, 0)),
            pl.BlockSpec((3 * d, 2 * gh), lambda b, i: (0, 0)),
            pl.BlockSpec((2 * gh, 3 * d), lambda b, i: (0, 0)),
            pl.BlockSpec((1, 2 * gh), lambda b, i: (0, 0)),
            pl.BlockSpec((2 * gh, 1), lambda b, i: (0, 0)),
        ],
        out_specs=[
            pl.BlockSpec((TILE, d), lambda b, i: (tile(b, i), 0)),
            pl.BlockSpec((TILE, d), lambda b, i: (tile(b, i), 0)),
            pl.BlockSpec((TILE, d), lambda b, i: (tile(b, i), 0)),
            pl.BlockSpec((2, TILE, gh), lambda b, i: (0, tile(b, i), 0)),
            pl.BlockSpec((2, gh, TILE), lambda b, i: (0, 0, tile(b, i))),
        ],
        out_shape=[
            jax.ShapeDtypeStruct((t, d), BF16),
            jax.ShapeDtypeStruct((t, d), BF16),
            jax.ShapeDtypeStruct((t, d), BF16),
            jax.ShapeDtypeStruct((2, t, gh), F32),
            jax.ShapeDtypeStruct((2, gh, t), F32),
        ],
        compiler_params=_cparams(("arbitrary", "arbitrary")),
        name="mlstm_pre",
    )(px, px, px, px, sc_w, sc_b, wq_bd, wk_bd, wif, wift, bif, bift)


AUG = 128


def _scan_direction(fwd, q_ref, k_ref, v_ref, g_ref, gt_ref, h_ref, cta_s, m_s):
    dh = q_ref.shape[1] // HEADS
    x = g_ref[...]
    xt = gt_ref[...]
    rid = lax.broadcasted_iota(jnp.int32, x.shape, 0)
    cid = lax.broadcasted_iota(jnp.int32, xt.shape, 1)
    p = x
    pt = xt
    s = 1
    while s < TILE:
        p = p + jnp.where(rid >= s, pltpu.roll(p, s, 0), 0.0)
        pt = pt + jnp.where(cid >= s, pltpu.roll(pt, s, 1), 0.0)
        s *= 2
    tot = p[TILE - 1:TILE, :]
    tott = pt[:, TILE - 1:TILE]
    bc = p if fwd else tot - p + x
    br = pt if fwd else tott - pt + xt
    cmax = x[:, 0:HEADS] - bc[:, HEADS:2 * HEADS]
    rida = lax.broadcasted_iota(jnp.int32, cmax.shape, 0)
    s = 1
    while s < TILE:
        if fwd:
            cmax = jnp.maximum(cmax, jnp.where(rida >= s, pltpu.roll(cmax, s, 0), NEG))
        else:
            cmax = jnp.maximum(cmax, jnp.where(rida < TILE - s, pltpu.roll(cmax, TILE - s, 0), NEG))
        s *= 2
    r2 = lax.broadcasted_iota(jnp.int32, (TILE, TILE), 0)
    c2 = lax.broadcasted_iota(jnp.int32, (TILE, TILE), 1)
    tri = r2 >= c2 if fwd else r2 <= c2
    lane = lax.broadcasted_iota(jnp.int32, (TILE, AUG), 1)
    one_col = jnp.where(lane == 0, 1.0, 0.0)
    nt_dims = (((1,), (1,)), ((), ()))
    tn_dims = (((0,), (0,)), ((), ()))
    b4 = bc[:, HEADS:2 * HEADS]
    bend4 = tot[:, HEADS:2 * HEADS]
    mprev4 = m_s[0:1, 0:HEADS]
    colv4 = -jnp.maximum(mprev4, cmax)
    wint4 = jnp.exp(mprev4 + colv4)
    emt4 = jnp.exp(colv4 - b4)
    g4 = bend4 - b4 + x[:, 0:HEADS]
    mnew4 = jnp.maximum(bend4 + mprev4, jnp.max(g4, axis=0, keepdims=True))
    wj4 = jnp.exp(g4 - mnew4)
    decay4 = jnp.exp(bend4 + mprev4 - mnew4)
    m_s[0:1, 0:HEADS] = mnew4
    for hh in range(HEADS):
        sl = slice(hh * dh, (hh + 1) * dh)
        q = q_ref[:, sl]
        k = k_ref[:, sl]
        v = v_ref[:, sl]
        row_vec = xt[hh:hh + 1, :] - br[HEADS + hh:HEADS + hh + 1, :]
        dmat = jnp.exp(jnp.where(tri, row_vec + colv4[:, hh:hh + 1], NEG))
        sc = lax.dot_general(q, k, nt_dims, preferred_element_type=F32) * dmat
        cta = cta_s[hh]
        v_aug = jnp.concatenate([v, one_col.astype(BF16)], axis=1)
        nd = (jnp.dot(sc.astype(BF16), v_aug, preferred_element_type=F32)
              + wint4[:, hh:hh + 1] * jnp.dot(q, cta.astype(BF16), preferred_element_type=F32))
        den = nd[:, dh:dh + 1]
        h = nd[:, 0:dh] / jnp.maximum(jnp.abs(den), emt4[:, hh:hh + 1])
        h_ref[:, sl] = h.astype(h_ref.dtype)
        wj = wj4[:, hh:hh + 1]
        vw = jnp.concatenate([(v.astype(F32) * wj).astype(BF16), (one_col * wj).astype(BF16)], axis=1)
        cta_s[hh] = (decay4[:, hh:hh + 1] * cta
                     + lax.dot_general(k, vw, tn_dims, preferred_element_type=F32))


def _scan_kernel(qf_ref, kf_ref, vf_ref, gf_ref, gtf_ref, qr_ref, kr_ref, vr_ref, gr_ref, gtr_ref,
                 hf_ref, hr_ref, ctaf_s, mf_s, ctar_s, mr_s):
    @pl.when(pl.program_id(1) == 0)
    def _():
        for ref in (ctaf_s, mf_s, ctar_s, mr_s):
            ref[...] = jnp.zeros(ref.shape, F32)

    _scan_direction(True, qf_ref, kf_ref, vf_ref, gf_ref, gtf_ref, hf_ref, ctaf_s, mf_s)
    _scan_direction(False, qr_ref, kr_ref, vr_ref, gr_ref, gtr_ref, hr_ref, ctar_s, mr_s)


def _mlstm_scan(q, k, px, col_v, gates, gates_t, bsz, ntb):
    t, d = q.shape
    dh = d // HEADS
    gh = 2 * HEADS

    def cf(b, i):
        return b * ntb + i

    def cr(b, i):
        return b * ntb + jnp.where(i == 0, 0, ntb - i)

    def specs(chunk, dr):
        return [
            pl.BlockSpec((TILE, d), lambda b, i: (chunk(b, i), 0)),
            pl.BlockSpec((TILE, d), lambda b, i: (chunk(b, i), 0)),
            pl.BlockSpec((TILE, d), lambda b, i: (chunk(b, i), col_v)),
            pl.BlockSpec((None, TILE, gh), lambda b, i: (dr, chunk(b, i), 0)),
            pl.BlockSpec((None, gh, TILE), lambda b, i: (dr, 0, chunk(b, i))),
        ]

    state = [pltpu.VMEM((HEADS, dh, dh + AUG), F32), pltpu.VMEM((SUBLANES, 128), F32)]
    hf, hr = pl.pallas_call(
        _scan_kernel,
        grid=(bsz, ntb),
        in_specs=specs(cf, 0) + specs(cr, 1),
        out_specs=[
            pl.BlockSpec((TILE, d), lambda b, i: (cf(b, i), 0)),
            pl.BlockSpec((TILE, d), lambda b, i: (cr(b, i), 0)),
        ],
        out_shape=[jax.ShapeDtypeStruct((t, d), BF16), jax.ShapeDtypeStruct((t, d), BF16)],
        scratch_shapes=state + state,
        compiler_params=_cparams(("arbitrary", "arbitrary")),
        name="mlstm_scan",
    )(q, k, px, gates, gates_t, q, k, px, gates, gates_t)
    return hf, hr


def _conv_kernel(u_ref, dw_ref, db_ref, lg_ref, lb_ref, wo_ref, o_ref, pad_s):
    i = pl.program_id(1)
    dc = dw_ref.shape[1]
    u = u_ref[...].astype(F32)
    a = u[:, :dc] * _sigmoid(u[:, dc:])
    zeros = jnp.zeros((CONV_HALO, dc), F32)

    def depthwise(seg_len, n_seg):
        stride = seg_len + 2 * CONV_HALO
        used = n_seg * stride
        for s in range(n_seg):
            base = s * stride
            pad_s[0, base:base + CONV_HALO, :] = zeros
            pad_s[0, base + CONV_HALO:base + CONV_HALO + seg_len, :] = a[s * seg_len:(s + 1) * seg_len, :]
            pad_s[0, base + CONV_HALO + seg_len:base + stride, :] = zeros
        for rho in range(1, SUBLANES):
            pad_s[rho, 0:used - SUBLANES, :] = pad_s[0, rho:used - SUBLANES + rho, :]
        outs = []
        for s in range(n_seg):
            base = s * stride
            acc = jnp.zeros((seg_len, dc), F32)
            for j in range(CONV_WIDTH):
                off = base + CONV_HALO - CONV_WIDTH // 2 + j
                rho = off % SUBLANES
                acc = acc + dw_ref[j:j + 1, :] * pad_s[rho, off - rho:off - rho + seg_len, :]
            outs.append(acc)
        return outs[0] if n_seg == 1 else jnp.concatenate(outs, axis=0)

    def finish(acc):
        acc = acc + db_ref[...]
        mu = jnp.mean(acc, axis=-1, keepdims=True)
        var = jnp.mean(jnp.square(acc - mu), axis=-1, keepdims=True)
        y = (acc - mu) * lax.rsqrt(var + LN_EPS) * lg_ref[...] + lb_ref[...]
        y = y * _sigmoid(y)
        o_ref[...] = jnp.dot(y.astype(BF16), wo_ref[...], preferred_element_type=F32).astype(o_ref.dtype)

    @pl.when(i == 0)
    def _():
        finish(depthwise(TILE, 1))

    @pl.when(i > 0)
    def _():
        finish(depthwise(GRID_W, TILE // GRID_W))


def _conv_module(px, col, dw_w, dw_b, ln_g, ln_b, w_o_bf, bsz, ntb):
    t = px.shape[0]
    dc, d = w_o_bf.shape
    return pl.pallas_call(
        _conv_kernel,
        grid=(bsz, ntb),
        in_specs=[
            pl.BlockSpec((TILE, 2 * dc), lambda b, i: (b * ntb + i, col)),
            pl.BlockSpec((CONV_WIDTH, dc), lambda b, i: (0, 0)),
            pl.BlockSpec((1, dc), lambda b, i: (0, 0)),
            pl.BlockSpec((1, dc), lambda b, i: (0, 0)),
            pl.BlockSpec((1, dc), lambda b, i: (0, 0)),
            pl.BlockSpec((dc, d), lambda b, i: (0, 0)),
        ],
        out_specs=pl.BlockSpec((TILE, d), lambda b, i: (b * ntb + i, 0)),
        out_shape=jax.ShapeDtypeStruct((t, d), BF16),
        scratch_shapes=[pltpu.VMEM((SUBLANES, (TILE // GRID_W) * (GRID_W + 2 * CONV_HALO), dc), F32)],
        compiler_params=_cparams(("arbitrary", "arbitrary")),
        name="conv_module",
    )(px, dw_w, dw_b, ln_g, ln_b, w_o_bf)


def _pool_kernel(ctx_len, u_ref, pw_ref, ps_ref, wo_ref, o_ref, d_s):
    nt, dp = u_ref.shape
    pg = dp // len(POOL_WINDOWS)
    rows = (nt - ctx_len) // GRID_W
    uf = u_ref[...].astype(F32)
    tq = lax.broadcasted_iota(jnp.int32, (ctx_len, ctx_len), 0)
    tk = lax.broadcasted_iota(jnp.int32, (ctx_len, ctx_len), 1)
    for g, w in enumerate(POOL_WINDOWS):
        cs = slice(g * pg, (g + 1) * pg)
        lo = jnp.maximum(tq - w // 2, 0)
        hi = jnp.minimum(tq + w // 2 - 1, ctx_len - 1)
        band = jnp.where(tk >= lo, jnp.where(tk <= hi, 1.0, 0.0), 0.0).astype(BF16)
        ssum = jnp.dot(band, u_ref[0:ctx_len, cs], preferred_element_type=F32)
        cnt = (hi - lo + 1)[:, 0:1].astype(F32)
        dd = ssum / cnt - uf[0:ctx_len, cs]
        y = jnp.dot(dd.astype(BF16), pw_ref[g], preferred_element_type=F32)
        d_s[0:ctx_len, cs] = (y * ps_ref[:, cs]).astype(BF16)
    ridx = lax.broadcasted_iota(jnp.int32, (rows, GRID_W, pg), 0)
    for g, w in enumerate(POOL_WINDOWS):
        cs = slice(g * pg, (g + 1) * pg)
        xg = uf[ctx_len:, cs].reshape(rows, GRID_W, pg)
        acc = xg
        for dlt in range(-(w // 2), w // 2):
            if dlt == 0:
                continue
            z = jnp.zeros((abs(dlt), GRID_W, pg), F32)
            if dlt > 0:
                acc = acc + jnp.concatenate([xg[dlt:], z], axis=0)
            else:
                acc = acc + jnp.concatenate([z, xg[:dlt]], axis=0)
        cnt = (jnp.minimum(ridx + w // 2 - 1, rows - 1) - jnp.maximum(ridx - w // 2, 0) + 1).astype(F32)
        dd = (acc / cnt - xg).reshape(rows * GRID_W, pg)
        y = jnp.dot(dd.astype(BF16), pw_ref[g], preferred_element_type=F32)
        d_s[ctx_len:, cs] = (y * ps_ref[:, cs]).astype(BF16)
    o_ref[...] = jnp.dot(d_s[...], wo_ref[...], preferred_element_type=F32).astype(o_ref.dtype)


def _pool_module(px, col, pool_w_bf, pool_scale, w_o_bf, bsz, nt, ctx_len):
    t = px.shape[0]
    dp, d = w_o_bf.shape
    ng = len(POOL_WINDOWS)
    return pl.pallas_call(
        functools.partial(_pool_kernel, ctx_len),
        grid=(bsz,),
        in_specs=[
            pl.BlockSpec((nt, dp), lambda b: (b, col)),
            pl.BlockSpec((ng, dp // ng, dp // ng), lambda b: (0, 0, 0)),
            pl.BlockSpec((1, dp), lambda b: (0, 0)),
            pl.BlockSpec((dp, d), lambda b: (0, 0)),
        ],
        out_specs=pl.BlockSpec((nt, d), lambda b: (b, 0)),
        out_shape=jax.ShapeDtypeStruct((t, d), BF16),
        scratch_shapes=[pltpu.VMEM((nt, dp), BF16)],
        compiler_params=_cparams(("arbitrary",)),
        name="pool_module",
    )(px, pool_w_bf, pool_scale, w_o_bf)


def _mix_out_kernel(x_ref, gate_ref, z_ref, a_ref, hf_ref, hr_ref, yc_ref, yp_ref, mod_ref, ng_ref, skip_ref, wm_ref,
                    wo_ref, g2_ref, rw_ref, rb_ref, xo_ref, hx_ref, lg_ref):
    d = x_ref.shape[1]
    dh = d // HEADS
    hsum = hf_ref[...].astype(F32) + hr_ref[...].astype(F32)
    parts = []
    for hh in range(HEADS):
        hb = hsum[:, hh * dh:(hh + 1) * dh]
        mu = jnp.mean(hb, axis=-1, keepdims=True)
        var = jnp.mean(jnp.square(hb - mu), axis=-1, keepdims=True)
        parts.append((hb - mu) * lax.rsqrt(var + LN_EPS))
    hn = jnp.concatenate(parts, axis=1)
    z = z_ref[...].astype(F32)
    tm = (hn * ng_ref[...] + skip_ref[...] * a_ref[...].astype(F32)) * (z * _sigmoid(z))
    y_m = jnp.dot(tm.astype(BF16), wm_ref[...], preferred_element_type=F32)
    merged = (_sigmoid(gate_ref[:, 0:d].astype(F32)) * yc_ref[...].astype(F32)
              + _sigmoid(gate_ref[:, d:2 * d].astype(F32)) * y_m
              + _sigmoid(gate_ref[:, 2 * d:3 * d].astype(F32)) * yp_ref[...].astype(F32))
    y = jnp.dot(merged.astype(BF16), wo_ref[...], preferred_element_type=F32)
    x = x_ref[...] + mod_ref[2:3, :] * y
    xo_ref[...] = x
    ms = jnp.mean(x * x, axis=-1, keepdims=True)
    hx = x * lax.rsqrt(ms + RMS_EPS) * g2_ref[...]
    hx = hx * (1.0 + mod_ref[4:5, :]) + mod_ref[3:4, :]
    hx_ref[...] = hx.reshape(hx_ref.shape)
    lg_ref[...] = lax.dot_general(rw_ref[...], hx.astype(BF16), (((1,), (1,)), ((), ())),
                                  preferred_element_type=F32) + rb_ref[...]


def _mix_out(xt, px, col_z, a, hf, hr, y_conv, y_pool, modtab, norm_g, skip, w_m_bf, w_out_bf, g2, router_w, router_b,
             ntb, mod_base):
    t, d = xt.shape
    ne = router_w.shape[0]
    row = _mod_row(ntb, mod_base)
    vec = lambda: pl.BlockSpec((1, d), lambda i: (0, 0))
    return pl.pallas_call(
        _mix_out_kernel,
        grid=(t // TILE,),
        in_specs=[
            pl.BlockSpec((TILE, d), lambda i: (i, 0)),
            pl.BlockSpec((TILE, 3 * d), lambda i: (i, 0)),
            pl.BlockSpec((TILE, d), lambda i: (i, col_z)),
            pl.BlockSpec((TILE, d), lambda i: (i, 0)),
            pl.BlockSpec((TILE, d), lambda i: (i, 0)),
            pl.BlockSpec((TILE, d), lambda i: (i, 0)),
            pl.BlockSpec((TILE, d), lambda i: (i, 0)),
            pl.BlockSpec((TILE, d), lambda i: (i, 0)),
            pl.BlockSpec((None, 6, d), lambda i: (row(i), 0, 0)),
            vec(), vec(),
            pl.BlockSpec((d, d), lambda i: (0, 0)),
            pl.BlockSpec((d, d), lambda i: (0, 0)),
            vec(),
            pl.BlockSpec((ne, d), lambda i: (0, 0)),
            pl.BlockSpec((ne, 1), lambda i: (0, 0)),
        ],
        out_specs=[
            pl.BlockSpec((TILE, d), lambda i: (i, 0)),
            pl.BlockSpec((TILE, SUBLANES, d // SUBLANES), lambda i: (i, 0, 0)),
            pl.BlockSpec((ne, TILE), lambda i: (0, i)),
        ],
        out_shape=[
            jax.ShapeDtypeStruct((t, d), F32),
            jax.ShapeDtypeStruct((t, SUBLANES, d // SUBLANES), F32),
            jax.ShapeDtypeStruct((ne, t), F32),
        ],
        compiler_params=_cparams(("arbitrary",)),
        name="mix_out",
    )(xt, px, px, a, hf, hr, y_conv, y_pool, modtab, norm_g, skip, w_m_bf, w_out_bf, g2, router_w, router_b)


def _expert_kernel(be_ref, nu_ref, tok_ref, tokn_ref, hx_hbm, wgu_ref, bgu_ref, wdn_ref, bdn_ref, y_ref,
                   xbuf_a, xbuf_b, xbuf_c, xbuf_d, sem_a, sem_b, sem_c, sem_d, xb_s, act_s, wgu_s, wdn_s):
    i = pl.program_id(0)
    n_used = nu_ref[0]
    half = xbuf_a.shape[0]
    dff = wdn_s.shape[0]
    n_chunk = dff // MOE_CHUNK

    def row_copy(tok_smem, r, xbuf, sem):
        t = tok_smem[0, 0, r]
        return pltpu.make_async_copy(hx_hbm.at[t], xbuf.at[r % half], sem.at[0])

    def wait_rows(xbuf, sem):
        pltpu.make_async_copy(hx_hbm.at[pl.ds(0, half)], xbuf, sem.at[0]).wait()

    @pl.when(i == 0)
    def _():
        def body(r, carry):
            row_copy(tok_ref, r, xbuf_a, sem_a).start()
            row_copy(tok_ref, half + r, xbuf_b, sem_b).start()
            return carry
        lax.fori_loop(0, half, body, 0)

    e_now = be_ref[i]
    e_prev = be_ref[jnp.maximum(i - 1, 0)]

    @pl.when(jnp.logical_or(i == 0, e_now != e_prev))
    def _():
        wgu_s[...] = wgu_ref[...].astype(BF16)
        wdn_s[...] = wdn_ref[...].astype(BF16)

    def half_block(x_cur, sem_cur, out_rows, prefetch):
        wait_rows(x_cur, sem_cur)
        xb_s[...] = x_cur[...].reshape(xb_s.shape).astype(BF16)
        for c in range(n_chunk):
            for r, (xbuf, sem) in (prefetch[c] if prefetch else ()):
                row_copy(tokn_ref, r, xbuf, sem).start(priority=r % 2)
            cg = slice(c * MOE_CHUNK, (c + 1) * MOE_CHUNK)
            cu = slice(dff + c * MOE_CHUNK, dff + (c + 1) * MOE_CHUNK)
            g = jnp.dot(xb_s[...], wgu_s[:, cg], preferred_element_type=F32) + bgu_ref[:, cg]
            u = jnp.dot(xb_s[...], wgu_s[:, cu], preferred_element_type=F32) + bgu_ref[:, cu]
            g = jnp.minimum(g, SWIGLU_LIMIT)
            u = jnp.clip(u, -SWIGLU_LIMIT, SWIGLU_LIMIT)
            act_s[:, cg] = ((u + 1.0) * (g * _sigmoid(SWIGLU_ALPHA * g))).astype(BF16)
        y = jnp.dot(act_s[...], wdn_s[...], preferred_element_type=F32)
        y_ref[out_rows] = (y + bdn_ref[...]).reshape((half,) + y_ref.shape[1:])

    used = i < n_used
    even = i % 2 == 0

    def block(cur, nxt):
        (xa, sa), (xb, sb) = cur
        per = 2 * half // n_chunk
        prefetch = [[(r, nxt[r // half]) for r in range(c * per, (c + 1) * per)] for c in range(n_chunk)]
        half_block(xa, sa, slice(0, half), prefetch)
        half_block(xb, sb, slice(half, 2 * half), None)

    bufs_even = ((xbuf_a, sem_a), (xbuf_b, sem_b))
    bufs_odd = ((xbuf_c, sem_c), (xbuf_d, sem_d))

    @pl.when(jnp.logical_and(used, even))
    def _():
        block(bufs_even, bufs_odd)

    @pl.when(jnp.logical_and(used, jnp.logical_not(even)))
    def _():
        block(bufs_odd, bufs_even)

    @pl.when(jnp.logical_not(used))
    def _():
        @pl.when(jnp.logical_and(i == n_used, even))
        def _():
            wait_rows(xbuf_a, sem_a)
            wait_rows(xbuf_b, sem_b)

        @pl.when(jnp.logical_and(i == n_used, jnp.logical_not(even)))
        def _():
            wait_rows(xbuf_c, sem_c)
            wait_rows(xbuf_d, sem_d)

        y_ref[...] = jnp.zeros(y_ref.shape, y_ref.dtype)


def _experts(layer, hx, row_tok, block_e, n_used, w_gu, b_gu, w_dn, b_dn):
    t, sub, dl = hx.shape
    d = sub * dl
    depth, ne, _, dff2 = w_gu.shape
    dff = dff2 // 2
    nb = block_e.shape[0]
    bm = MOE_BM
    half = bm // 2
    tok3 = row_tok.reshape(nb, 1, bm)
    grid_spec = pltpu.PrefetchScalarGridSpec(
        num_scalar_prefetch=2,
        grid=(nb,),
        in_specs=[
            pl.BlockSpec((1, 1, bm), lambda i, be, nu: (i, 0, 0), memory_space=pltpu.SMEM),
            pl.BlockSpec((1, 1, bm), lambda i, be, nu: (jnp.minimum(i + 1, nb - 1), 0, 0),
                         memory_space=pltpu.SMEM),
            pl.BlockSpec(memory_space=pl.ANY),
            pl.BlockSpec((None, None, d, dff2), lambda i, be, nu: (layer, be[i], 0, 0)),
            pl.BlockSpec((None, None, 1, dff2), lambda i, be, nu: (layer, be[i], 0, 0)),
            pl.BlockSpec((None, None, dff, d), lambda i, be, nu: (layer, be[i], 0, 0)),
            pl.BlockSpec((None, None, 1, d), lambda i, be, nu: (layer, be[i], 0, 0)),
        ],
        out_specs=pl.BlockSpec((bm, sub, dl), lambda i, be, nu: (i, 0, 0)),
        scratch_shapes=[
            pltpu.VMEM((half, sub, dl), F32),
            pltpu.VMEM((half, sub, dl), F32),
            pltpu.VMEM((half, sub, dl), F32),
            pltpu.VMEM((half, sub, dl), F32),
            pltpu.SemaphoreType.DMA((1,)),
            pltpu.SemaphoreType.DMA((1,)),
            pltpu.SemaphoreType.DMA((1,)),
            pltpu.SemaphoreType.DMA((1,)),
            pltpu.VMEM((half, d), BF16),
            pltpu.VMEM((half, dff), BF16),
            pltpu.VMEM((d, dff2), BF16),
            pltpu.VMEM((dff, d), BF16),
        ],
    )
    return pl.pallas_call(
        _expert_kernel,
        grid_spec=grid_spec,
        out_shape=jax.ShapeDtypeStruct((nb * bm, sub, dl), F32),
        compiler_params=_cparams(("arbitrary",), no_bounds_checks=True),
        name="moe_experts",
    )(block_e, n_used, tok3, tok3, hx, w_gu, b_gu.reshape(depth, ne, 1, dff2), w_dn,
      b_dn.reshape(depth, ne, 1, d))


def _combine_kernel(dst_ref, dstn_ref, ys_hbm, gw_ref, x_ref, mod_ref, o_ref, buf, sem):
    i = pl.program_id(0)
    nsteps = pl.num_programs(0)
    slot = i % 2
    n_copy = TOP_K * TILE

    def row_copy(dst_smem, e, sl):
        src = dst_smem[0, 0, e]
        kk = e // TILE
        r = e % TILE
        return pltpu.make_async_copy(ys_hbm.at[src], buf.at[sl, kk, r], sem.at[sl])

    @pl.when(i == 0)
    def _():
        def body(e, carry):
            row_copy(dst_ref, e, 0).start()
            return carry
        lax.fori_loop(0, n_copy, body, 0)

    @pl.when(i + 1 < nsteps)
    def _():
        for e in range(n_copy):
            row_copy(dstn_ref, e, 1 - slot).start(priority=e % 2)

    for kk in range(TOP_K):
        pltpu.make_async_copy(ys_hbm.at[pl.ds(0, TILE)], buf.at[slot, kk], sem.at[slot]).wait()

    gw = gw_ref[...]
    y = gw[:, 0:1] * buf[slot, 0].reshape(x_ref.shape)
    for kk in range(1, TOP_K):
        y = y + gw[:, kk:kk + 1] * buf[slot, kk].reshape(x_ref.shape)
    o_ref[...] = x_ref[...] + mod_ref[5:6, :] * y


def _combine(ys, dest_tiles, gate_w, xt, modtab, ntb, mod_base):
    t, d = xt.shape
    nt = t // TILE
    row = _mod_row(ntb, mod_base)
    return pl.pallas_call(
        _combine_kernel,
        grid=(nt,),
        in_specs=[
            pl.BlockSpec((1, 1, TOP_K * TILE), lambda i: (i, 0, 0), memory_space=pltpu.SMEM),
            pl.BlockSpec((1, 1, TOP_K * TILE), lambda i: (jnp.minimum(i + 1, nt - 1), 0, 0),
                         memory_space=pltpu.SMEM),
            pl.BlockSpec(memory_space=pl.ANY),
            pl.BlockSpec((TILE, TOP_K), lambda i: (i, 0)),
            pl.BlockSpec((TILE, d), lambda i: (i, 0)),
            pl.BlockSpec((None, 6, d), lambda i: (row(i), 0, 0)),
        ],
        out_specs=pl.BlockSpec((TILE, d), lambda i: (i, 0)),
        out_shape=jax.ShapeDtypeStruct((t, d), F32),
        scratch_shapes=[
            pltpu.VMEM((2, TOP_K, TILE) + ys.shape[1:], F32),
            pltpu.SemaphoreType.DMA((2,)),
        ],
        compiler_params=_cparams(("arbitrary",), no_bounds_checks=True),
        name="moe_combine",
    )(dest_tiles, dest_tiles, ys, gate_w, xt, modtab)


IN_CHUNKS = 6
IN_DMA_CHUNKS = 4


def _comb_in_kernel(dst_ref, dstn_ref, ys_hbm, gw_ref, x_ref, modp_ref, g_ref, mod_ref, w_ref, xo_ref, o_ref,
                    buf_a, buf_b, sem_a, sem_b, fsem, hb_s):
    i = pl.program_id(0)
    nsteps = pl.num_programs(0)
    n_copy = TOP_K * TILE
    n = w_ref.shape[1]
    cw = n // IN_CHUNKS
    per = n_copy // IN_DMA_CHUNKS

    def row_copy(dst_smem, e, buf, sem):
        src = dst_smem[0, 0, e]
        return pltpu.make_async_copy(ys_hbm.at[src], buf.at[e // TILE, e % TILE], sem.at[0])

    def wait_rows(buf, sem):
        for kk in range(TOP_K):
            pltpu.make_async_copy(ys_hbm.at[pl.ds(0, TILE)], buf.at[kk], sem.at[0]).wait()

    @pl.when(i == 0)
    def _():
        def body(e, carry):
            row_copy(dst_ref, e, buf_a, sem_a).start()
            return carry
        lax.fori_loop(0, n_copy, body, 0)

    def tile(buf_cur, sem_cur, buf_nxt, sem_nxt):
        wait_rows(buf_cur, sem_cur)
        gw = gw_ref[...]
        y = gw[:, 0:1] * buf_cur[0].reshape(x_ref.shape)
        for kk in range(1, TOP_K):
            y = y + gw[:, kk:kk + 1] * buf_cur[kk].reshape(x_ref.shape)
        x = x_ref[...] + modp_ref[5:6, :] * y
        xo_ref[...] = x
        ms = jnp.mean(x * x, axis=-1, keepdims=True)
        h = x * lax.rsqrt(ms + RMS_EPS) * g_ref[...]
        h = h * (1.0 + mod_ref[1:2, :]) + mod_ref[0:1, :]
        hb_s[...] = h.astype(BF16)
        for c in range(IN_CHUNKS):
            if c < IN_DMA_CHUNKS:
                for e in range(c * per, (c + 1) * per):
                    row_copy(dstn_ref, e, buf_nxt, sem_nxt).start(priority=e % 2)
            if c == IN_DMA_CHUNKS:
                pl.semaphore_signal(fsem, 1)
                pl.semaphore_wait(fsem, 1)
            cs = slice(c * cw, (c + 1) * cw)
            o_ref[:, cs] = jnp.dot(hb_s[...], w_ref[:, cs], preferred_element_type=F32).astype(o_ref.dtype)

    even = i % 2 == 0

    @pl.when(even)
    def _():
        tile(buf_a, sem_a, buf_b, sem_b)

    @pl.when(jnp.logical_not(even))
    def _():
        tile(buf_b, sem_b, buf_a, sem_a)

    @pl.when(jnp.logical_and(i == nsteps - 1, even))
    def _():
        wait_rows(buf_b, sem_b)

    @pl.when(jnp.logical_and(i == nsteps - 1, jnp.logical_not(even)))
    def _():
        wait_rows(buf_a, sem_a)


def _comb_in_proj(ys, dest_tiles, gate_w, xt, g1, modtab, w_in_bf, ntb, mod_base_prev, mod_base):
    t, d = xt.shape
    n = w_in_bf.shape[1]
    nt = t // TILE
    assert n % (IN_CHUNKS * 256) == 0 and (TOP_K * TILE) % IN_DMA_CHUNKS == 0
    row_prev = _mod_row(ntb, mod_base_prev)
    row = _mod_row(ntb, mod_base)
    return pl.pallas_call(
        _comb_in_kernel,
        grid=(nt,),
        in_specs=[
            pl.BlockSpec((1, 1, TOP_K * TILE), lambda i: (i, 0, 0), memory_space=pltpu.SMEM),
            pl.BlockSpec((1, 1, TOP_K * TILE), lambda i: (jnp.minimum(i + 1, nt - 1), 0, 0),
                         memory_space=pltpu.SMEM),
            pl.BlockSpec(memory_space=pl.ANY),
            pl.BlockSpec((TILE, TOP_K), lambda i: (i, 0)),
            pl.BlockSpec((TILE, d), lambda i: (i, 0)),
            pl.BlockSpec((None, 6, d), lambda i: (row_prev(i), 0, 0)),
            pl.BlockSpec((1, d), lambda i: (0, 0)),
            pl.BlockSpec((None, 6, d), lambda i: (row(i), 0, 0)),
            pl.BlockSpec((d, n), lambda i: (0, 0), pipeline_mode=pl.Buffered(1)),
        ],
        out_specs=[
            pl.BlockSpec((TILE, d), lambda i: (i, 0)),
            pl.BlockSpec((TILE, n), lambda i: (i, 0)),
        ],
        out_shape=[
            jax.ShapeDtypeStruct((t, d), F32),
            jax.ShapeDtypeStruct((t, n), BF16),
        ],
        scratch_shapes=[
            pltpu.VMEM((TOP_K, TILE) + ys.shape[1:], F32),
            pltpu.VMEM((TOP_K, TILE) + ys.shape[1:], F32),
            pltpu.SemaphoreType.DMA((1,)),
            pltpu.SemaphoreType.DMA((1,)),
            pltpu.SemaphoreType.REGULAR,
            pltpu.VMEM((TILE, d), BF16),
        ],
        compiler_params=_cparams(("arbitrary",), no_bounds_checks=True),
        name="comb_in_proj",
    )(dest_tiles, dest_tiles, ys, gate_w, xt, modtab, g1, modtab, w_in_bf)


def _final_kernel(x_ref, g_ref, o_ref):
    x = x_ref[...]
    ms = jnp.mean(x * x, axis=-1, keepdims=True)
    o_ref[...] = x * lax.rsqrt(ms + RMS_EPS) * g_ref[...]


def _final_norm(xt, final_g, bsz, ntb):
    t, d = xt.shape
    nlat = ntb - 1
    return pl.pallas_call(
        _final_kernel,
        grid=(bsz, nlat),
        in_specs=[
            pl.BlockSpec((TILE, d), lambda b, j: (b * ntb + 1 + j, 0)),
            pl.BlockSpec((1, d), lambda b, j: (0, 0)),
        ],
        out_specs=pl.BlockSpec((TILE, d), lambda b, j: (b * nlat + j, 0)),
        out_shape=jax.ShapeDtypeStruct((bsz * nlat * TILE, d), F32),
        compiler_params=_cparams(("arbitrary", "arbitrary")),
        name="final_norm",
    )(xt, final_g)


def _route_kernel(lg_ref, te_ref, gw_ref, pos_ref, cnt_ref, cnt_s):
    i = pl.program_id(0)

    @pl.when(i == 0)
    def _():
        cnt_s[...] = jnp.zeros(cnt_s.shape, F32)

    lg = lg_ref[...]
    ne = lg.shape[0]
    sub = lax.broadcasted_iota(jnp.int32, lg.shape, 0)
    work = lg
    vals, idxs, hots = [], [], []
    for _ in range(TOP_K):
        m = jnp.max(work, axis=0, keepdims=True)
        idx = jnp.min(jnp.where(work == m, sub, ne), axis=0, keepdims=True)
        hot = sub == idx
        vals.append(m)
        idxs.append(idx)
        hots.append(jnp.where(hot, 1.0, 0.0))
        work = jnp.where(hot, -jnp.inf, work)
    exps = [jnp.exp(v - vals[0]) for v in vals]
    ssum = exps[0]
    for e in exps[1:]:
        ssum = ssum + e
    hot_all = hots[0]
    for h in hots[1:]:
        hot_all = hot_all + h
    r2 = lax.broadcasted_iota(jnp.int32, (TILE, TILE), 0)
    c2 = lax.broadcasted_iota(jnp.int32, (TILE, TILE), 1)
    earlier = jnp.where(r2 < c2, 1.0, 0.0).astype(BF16)
    before = jnp.dot(hot_all.astype(BF16), earlier, preferred_element_type=F32) + cnt_s[...]
    for kk in range(TOP_K):
        te_ref[kk:kk + 1, :] = idxs[kk]
        gw_ref[kk:kk + 1, :] = exps[kk] / ssum
        pos_ref[kk:kk + 1, :] = jnp.sum(hots[kk] * before, axis=0, keepdims=True).astype(jnp.int32)
    cnt_s[...] = cnt_s[...] + jnp.sum(hot_all, axis=1, keepdims=True)
    cnt_ref[...] = cnt_s[...]


def _route(logits_t):
    ne, t = logits_t.shape
    return pl.pallas_call(
        _route_kernel,
        grid=(t // TILE,),
        in_specs=[pl.BlockSpec((ne, TILE), lambda i: (0, i))],
        out_specs=[
            pl.BlockSpec((TOP_K, TILE), lambda i: (0, i)),
            pl.BlockSpec((TOP_K, TILE), lambda i: (0, i)),
            pl.BlockSpec((TOP_K, TILE), lambda i: (0, i)),
            pl.BlockSpec((ne, 1), lambda i: (0, 0)),
        ],
        out_shape=[
            jax.ShapeDtypeStruct((TOP_K, t), jnp.int32),
            jax.ShapeDtypeStruct((TOP_K, t), F32),
            jax.ShapeDtypeStruct((TOP_K, t), jnp.int32),
            jax.ShapeDtypeStruct((ne, 1), F32),
        ],
        scratch_shapes=[pltpu.VMEM((ne, 1), F32)],
        compiler_params=_cparams(("arbitrary",)),
        name="route",
    )(logits_t)


def _routing(logits_t, n_tiles):
    t = logits_t.shape[1]
    top_e, gate, pos, cnt = _route(logits_t)
    n_assign = t * TOP_K
    experts = jnp.arange(N_EXPERTS, dtype=jnp.int32)
    key = (top_e * t + jnp.arange(t, dtype=jnp.int32)[None, :]).reshape(-1)
    order = jnp.argsort(key).astype(jnp.int32)
    counts = cnt.reshape(N_EXPERTS).astype(jnp.int32)
    grp_start = jnp.cumsum(counts) - counts
    padded = (counts + MOE_BM - 1) // MOE_BM * MOE_BM
    pad_end = jnp.cumsum(padded)
    pad_start = pad_end - padded
    dest = pos + jnp.sum(jnp.where(top_e[:, :, None] == experts[None, None, :], pad_start[None, None, :], 0), axis=2)
    n_blocks = n_assign // MOE_BM + N_EXPERTS
    blk = jnp.arange(n_blocks, dtype=jnp.int32)
    block_e = jnp.minimum(jnp.sum((pad_end[None, :] <= (blk * MOE_BM)[:, None]).astype(jnp.int32), axis=1),
                          N_EXPERTS - 1)
    src0 = grp_start[block_e] + blk * MOE_BM - pad_start[block_e]
    lim = grp_start[block_e] + counts[block_e]
    src = src0[:, None] + jnp.arange(MOE_BM, dtype=jnp.int32)[None, :]
    valid = src < lim[:, None]
    row_tok = jnp.where(valid, order[jnp.clip(src, 0, n_assign - 1)] % t, 0).astype(jnp.int32).reshape(-1)
    n_used = (pad_end[-1] // MOE_BM).astype(jnp.int32).reshape(1)
    dest_tiles = dest.reshape(TOP_K, n_tiles, TILE).transpose(1, 0, 2).reshape(n_tiles, 1, TOP_K * TILE)
    return gate.T, row_tok, block_e, n_used, dest_tiles


def _block_diag(w, heads):
    nb, c, _ = w.shape
    per = nb // heads
    w = w.reshape(heads, per, c, c)
    eye = jnp.eye(per, dtype=w.dtype)
    full = eye[None, :, None, :, None] * w[:, :, :, None, :]
    return full.reshape(heads, per * c, per * c)


def kernel(x, c, ctx, c_ctx, w_ada, b_ada, norm1_g, norm2_g, w_in, conv_dw_w, conv_dw_b, conv_ln_g, conv_ln_b, w_conv_out, m_sc_w, m_sc_b, m_wq, m_wk, m_w_if, m_b_if, m_norm_g, m_skip, w_m_out, pool_w, pool_scale, w_pool_out, w_out, router_w, router_b, w_gu, b_gu, w_dn, b_dn, final_g):
    bsz, seq, d = x.shape
    ctx_len = ctx.shape[1]
    depth = w_ada.shape[0]
    assert ctx_len == TILE and seq % TILE == 0 and d % (128 * HEADS) == 0
    nt = ctx_len + seq
    ntb = nt // TILE
    t = bsz * nt
    n_tiles = t // TILE
    d_conv = w_conv_out.shape[1]
    d_pool = w_pool_out.shape[1]
    d_m = w_m_out.shape[1]
    assert d_m == d and 2 * d_conv == d and 2 * d_pool == d
    c_conv = 3 * d
    c_pool = c_conv + 2 * d_conv
    c_mz = c_pool + d_pool
    c_mqk = c_mz + d_m
    c_mv = c_mqk + d_m
    d_in = c_mv + d_m
    col_conv, col_z, col_u, col_v = 3, 4, 5, 6
    col_pool = (7 * d) // d_pool

    xt = jnp.concatenate([ctx, x], axis=1).reshape(t, d)

    mp = -(-(bsz + 1) // 8) * 8
    c_all = jnp.zeros((mp, d), F32).at[:bsz].set(c).at[bsz].set(c_ctx)
    mod = _ada(c_all, w_ada, b_ada)
    mod_x = mod[:, :bsz].reshape(depth, bsz, 1, 6, d)
    mod_c = jnp.broadcast_to(mod[:, bsz].reshape(depth, 1, 1, 6, d), (depth, bsz, 1, 6, d))
    modtab = jnp.concatenate([mod_c, mod_x], axis=2).reshape(depth * bsz * 2, 6, d)

    gh = 2 * HEADS
    for l in range(depth):
        mod_base = l * bsz * 2
        w_in_bf = jnp.concatenate([w_in[l, :, 0:c_pool], w_in[l, :, c_mz:d_in], w_in[l, :, c_pool:c_mz]],
                                  axis=1).astype(BF16)
        if l == 0:
            px = _in_proj(xt, norm1_g[l].reshape(1, d), modtab, w_in_bf, ntb, mod_base)
        else:
            xt, px = _comb_in_proj(ys, dest_tiles, gate_w, xt, norm1_g[l].reshape(1, d), modtab, w_in_bf, ntb,
                                   mod_base - bsz * 2, mod_base)

        wq_bd = _block_diag(m_wq[l], HEADS).astype(BF16)
        wk_bd = _block_diag(m_wk[l], HEADS).astype(BF16)
        wif = jnp.concatenate([m_w_if[l, 0], m_w_if[l, 1]], axis=1)
        bif = jnp.concatenate([m_b_if[l, 0], m_b_if[l, 1]], axis=0)
        a, q, k, gates, gates_t = _mlstm_pre(
            px, col_u, col_v, m_sc_w[l], m_sc_b[l].reshape(1, d), wq_bd, wk_bd,
            wif.astype(BF16), wif.T.astype(BF16), bif.reshape(1, 2 * gh), bif.reshape(2 * gh, 1), bsz, ntb)
        hf, hr = _mlstm_scan(q, k, px, col_v, gates, gates_t, bsz, ntb)

        y_conv = _conv_module(px, col_conv, conv_dw_w[l], conv_dw_b[l].reshape(1, d_conv),
                              conv_ln_g[l].reshape(1, d_conv), conv_ln_b[l].reshape(1, d_conv),
                              w_conv_out[l].astype(BF16), bsz, ntb)
        y_pool = _pool_module(px, col_pool, pool_w[l].astype(BF16), pool_scale[l].reshape(1, d_pool),
                              w_pool_out[l].astype(BF16), bsz, nt, ctx_len)

        xt, hx, logits = _mix_out(
            xt, px, col_z, a, hf, hr, y_conv, y_pool, modtab, m_norm_g[l].reshape(1, d), m_skip[l].reshape(1, d),
            w_m_out[l].astype(BF16), w_out[l].astype(BF16), norm2_g[l].reshape(1, d), router_w[l].T.astype(BF16),
            router_b[l].reshape(N_EXPERTS, 1), ntb, mod_base)

        gate_w, row_tok, block_e, n_used, dest_tiles = _routing(logits, n_tiles)
        ys = _experts(l, hx, row_tok, block_e, n_used, w_gu, b_gu, w_dn, b_dn)

    xt = _combine(ys, dest_tiles, gate_w, xt, modtab, ntb, (depth - 1) * bsz * 2)
    out = _final_norm(xt, final_g.reshape(1, d), bsz, ntb)
    return out.reshape(bsz, seq, d)
```

```python
import functools

import jax
import jax.numpy as jnp
from jax import lax
from jax.experimental import pallas as pl
from jax.experimental.pallas import tpu as pltpu

F32 = jnp.float32
BF16 = jnp.bfloat16
HIGHEST = lax.Precision.HIGHEST

TILE = 256
GRID_W = 64
CONV_WIDTH = 31
SUBLANES = 8
CONV_HALO = 16
SHORT_CONV = 4
HEADS = 4
POOL_WINDOWS = (2, 4, 8, 16)
N_EXPERTS = 32
TOP_K = 4
MOE_BM = 512
MOE_CHUNK = 256
SWIGLU_LIMIT = 7.0
SWIGLU_ALPHA = 1.702
RMS_EPS = 1e-6
LN_EPS = 1e-5
NEG = -1e30
VMEM_LIMIT = 56 * 1024 * 1024


def _cparams(sem, no_bounds_checks=False):
    return pltpu.CompilerParams(dimension_semantics=sem, vmem_limit_bytes=VMEM_LIMIT,
                                disable_bounds_checks=no_bounds_checks)


def _sigmoid(x):
    return jax.nn.sigmoid(x)


def _log_sigmoid(x):
    return jnp.minimum(x, 0.0) - jnp.log1p(jnp.exp(-jnp.abs(x)))


def _ada_kernel(c_ref, w_ref, b_ref, o_ref):
    s = c_ref[...]
    s = s * _sigmoid(s)
    o_ref[...] = jnp.dot(s, w_ref[...], precision=HIGHEST, preferred_element_type=F32) + b_ref[...]


def _ada(c_all, w_ada, b_ada):
    depth, d, n = w_ada.shape
    mp = c_all.shape[0]
    tn = 1536
    return pl.pallas_call(
        _ada_kernel,
        grid=(depth, n // tn),
        in_specs=[
            pl.BlockSpec((mp, d), lambda l, j: (0, 0)),
            pl.BlockSpec((None, d, tn), lambda l, j: (l, 0, j)),
            pl.BlockSpec((None, 1, tn), lambda l, j: (l, 0, j)),
        ],
        out_specs=pl.BlockSpec((None, mp, tn), lambda l, j: (l, 0, j)),
        out_shape=jax.ShapeDtypeStruct((depth, mp, n), F32),
        compiler_params=_cparams(("arbitrary", "arbitrary")),
        name="ada",
    )(c_all, w_ada, b_ada.reshape(depth, 1, n))


def _in_kernel(x_ref, g_ref, mod_ref, w_ref, o_ref):
    x = x_ref[...]
    ms = jnp.mean(x * x, axis=-1, keepdims=True)
    y = x * lax.rsqrt(ms + RMS_EPS) * g_ref[...]
    h = y * (1.0 + mod_ref[1:2, :]) + mod_ref[0:1, :]
    o_ref[...] = jnp.dot(h.astype(BF16), w_ref[...], preferred_element_type=F32).astype(BF16)


def _mod_row(ntb, base):
    def f(i):
        return base + 2 * (i // ntb) + jnp.minimum(i % ntb, 1)
    return f


def _in_proj(xt, g1, modtab, w_in_bf, ntb, mod_base):
    t, d = xt.shape
    n = w_in_bf.shape[1]
    row = _mod_row(ntb, mod_base)
    return pl.pallas_call(
        _in_kernel,
        grid=(t // TILE,),
        in_specs=[
            pl.BlockSpec((TILE, d), lambda i: (i, 0)),
            pl.BlockSpec((1, d), lambda i: (0, 0)),
            pl.BlockSpec((None, 6, d), lambda i: (row(i), 0, 0)),
            pl.BlockSpec((d, n), lambda i: (0, 0), pipeline_mode=pl.Buffered(1)),
        ],
        out_specs=pl.BlockSpec((TILE, n), lambda i: (i, 0)),
        out_shape=jax.ShapeDtypeStruct((t, n), BF16),
        compiler_params=_cparams(("arbitrary",)),
        name="in_proj",
    )(xt, g1, modtab, w_in_bf)


def _mpre_kernel(ntb, u_ref, up_ref, un_ref, v_ref, scw_ref, scb_ref, wq_ref, wk_ref, wif_ref, wift_ref,
                 bif_ref, bift_ref, a_ref, q_ref, k_ref, g_ref, gt_ref):
    i = pl.program_id(1)
    d = u_ref.shape[1]
    dh = d // HEADS
    u = u_ref[...].astype(F32)
    up = jnp.where(i >= 2, up_ref[...].astype(F32)[8:16, :], 0.0)
    un = jnp.where(jnp.logical_and(i >= 1, i <= ntb - 2), un_ref[...].astype(F32)[0:8, :], 0.0)
    ext = jnp.concatenate([up, u, un], axis=0)
    n_ext = TILE + 16
    conv = scb_ref[...]
    for j in range(SHORT_CONV):
        sh = (SHORT_CONV // 2 - j) % n_ext
        shifted = ext if sh == 0 else pltpu.roll(ext, sh, 0)
        conv = conv + scw_ref[j:j + 1, :] * shifted[8:8 + TILE, :]
    a = conv * _sigmoid(conv)
    ab = a.astype(BF16)
    a_ref[...] = ab
    qs, ks = [], []
    for hb in range(HEADS):
        blk = ab[:, hb * dh:(hb + 1) * dh]
        qs.append(jnp.dot(blk, wq_ref[hb], preferred_element_type=F32))
        ks.append(jnp.dot(blk, wk_ref[hb], preferred_element_type=F32))
    q = jnp.concatenate(qs, axis=1)
    k = jnp.concatenate(ks, axis=1)
    qb = q.astype(BF16)
    kb = k.astype(BF16)
    vb = v_ref[...]
    q_ref[...] = (q * (dh ** -0.5)).astype(BF16)
    k_ref[...] = kb
    pre = bif_ref[...]
    pret = bift_ref[...]
    nt_dims = (((1,), (1,)), ((), ()))
    for part, f in enumerate((qb, kb, vb)):
        pre = pre + jnp.dot(f, wif_ref[part * d:(part + 1) * d, :], preferred_element_type=F32)
        pret = pret + lax.dot_general(wift_ref[:, part * d:(part + 1) * d], f, nt_dims,
                                      preferred_element_type=F32)
    col = lax.broadcasted_iota(jnp.int32, pre.shape, 1)
    pre = jnp.where(col % (2 * HEADS) >= HEADS, _log_sigmoid(pre), pre)
    rowi = lax.broadcasted_iota(jnp.int32, pret.shape, 0)
    pret = jnp.where(rowi % (2 * HEADS) >= HEADS, _log_sigmoid(pret), pret)
    g_ref[0] = pre[:, 0:2 * HEADS]
    g_ref[1] = pre[:, 2 * HEADS:4 * HEADS]
    gt_ref[0] = pret[0:2 * HEADS, :]
    gt_ref[1] = pret[2 * HEADS:4 * HEADS, :]


def _mlstm_pre(px, col_u, col_v, sc_w, sc_b, wq_bd, wk_bd, wif, wift, bif, bift, bsz, ntb):
    t = px.shape[0]
    d = sc_w.shape[1]
    hrows = 16
    nh = t // hrows
    per = TILE // hrows

    def tile(b, i):
        return b * ntb + i

    gh = 2 * HEADS
    return pl.pallas_call(
        functools.partial(_mpre_kernel, ntb),
        grid=(bsz, ntb),
        in_specs=[
            pl.BlockSpec((TILE, d), lambda b, i: (tile(b, i), col_u)),
            pl.BlockSpec((hrows, d), lambda b, i: (jnp.maximum(tile(b, i) * per - 1, 0), col_u)),
            pl.BlockSpec((hrows, d), lambda b, i: (jnp.minimum((tile(b, i) + 1) * per, nh - 1), col_u)),
            pl.BlockSpec((TILE, d), lambda b, i: (tile(b, i), col_v)),
            pl.BlockSpec((SHORT_CONV, d), lambda b, i: (0, 0)),
            pl.BlockSpec((1, d), lambda b, i: (0, 0)),
            pl.BlockSpec((HEADS, d // HEADS, d // HEADS), lambda b, i: (0, 0, 0)),
            pl.BlockSpec((HEADS, d // HEADS, d // HEADS), lambda b, i: (0, 0, 0)),
            pl.BlockSpec((3 * d, 2 * gh), lambda b, i: (0, 0)),
            pl.BlockSpec((2 * gh, 3 * d), lambda b, i: (0, 0)),
            pl.BlockSpec((1, 2 * gh), lambda b, i: (0, 0)),
            pl.BlockSpec((2 * gh, 1), lambda b, i: (0, 0)),
        ],
        out_specs=[
            pl.BlockSpec((TILE, d), lambda b, i: (tile(b, i), 0)),
            pl.BlockSpec((TILE, d), lambda b, i: (tile(b, i), 0)),
            pl.BlockSpec((TILE, d), lambda b, i: (tile(b, i), 0)),
            pl.BlockSpec((2, TILE, gh), lambda b, i: (0, tile(b, i), 0)),
            pl.BlockSpec((2, gh, TILE), lambda b, i: (0, 0, tile(b, i))),
        ],
        out_shape=[
            jax.ShapeDtypeStruct((t, d), BF16),
            jax.ShapeDtypeStruct((t, d), BF16),
            jax.ShapeDtypeStruct((t, d), BF16),
            jax.ShapeDtypeStruct((2, t, gh), F32),
            jax.ShapeDtypeStruct((2, gh, t), F32),
        ],
        compiler_params=_cparams(("arbitrary", "arbitrary")),
        name="mlstm_pre",
    )(px, px, px, px, sc_w, sc_b, wq_bd, wk_bd, wif, wift, bif, bift)


AUG = 128


def _scan_direction(fwd, q_ref, k_ref, v_ref, g_ref, gt_ref, h_ref, cta_s, m_s):
    dh = q_ref.shape[1] // HEADS
    x = g_ref[...]
    xt = gt_ref[...]
    rid = lax.broadcasted_iota(jnp.int32, x.shape, 0)
    cid = lax.broadcasted_iota(jnp.int32, xt.shape, 1)
    p = x
    pt = xt
    s = 1
    while s < TILE:
        p = p + jnp.where(rid >= s, pltpu.roll(p, s, 0), 0.0)
        pt = pt + jnp.where(cid >= s, pltpu.roll(pt, s, 1), 0.0)
        s *= 2
    tot = p[TILE - 1:TILE, :]
    tott = pt[:, TILE - 1:TILE]
    bc = p if fwd else tot - p + x
    br = pt if fwd else tott - pt + xt
    cmax = x[:, 0:HEADS] - bc[:, HEADS:2 * HEADS]
    rida = lax.broadcasted_iota(jnp.int32, cmax.shape, 0)
    s = 1
    while s < TILE:
        if fwd:
            cmax = jnp.maximum(cmax, jnp.where(rida >= s, pltpu.roll(cmax, s, 0), NEG))
        else:
            cmax = jnp.maximum(cmax, jnp.where(rida < TILE - s, pltpu.roll(cmax, TILE - s, 0), NEG))
        s *= 2
    r2 = lax.broadcasted_iota(jnp.int32, (TILE, TILE), 0)
    c2 = lax.broadcasted_iota(jnp.int32, (TILE, TILE), 1)
    tri = r2 >= c2 if fwd else r2 <= c2
    lane = lax.broadcasted_iota(jnp.int32, (TILE, AUG), 1)
    one_col = jnp.where(lane == 0, 1.0, 0.0)
    nt_dims = (((1,), (1,)), ((), ()))
    tn_dims = (((0,), (0,)), ((), ()))
    b4 = bc[:, HEADS:2 * HEADS]
    bend4 = tot[:, HEADS:2 * HEADS]
    mprev4 = m_s[0:1, 0:HEADS]
    colv4 = -jnp.maximum(mprev4, cmax)
    wint4 = jnp.exp(mprev4 + colv4)
    emt4 = jnp.exp(colv4 - b4)
    g4 = bend4 - b4 + x[:, 0:HEADS]
    mnew4 = jnp.maximum(bend4 + mprev4, jnp.max(g4, axis=0, keepdims=True))
    wj4 = jnp.exp(g4 - mnew4)
    decay4 = jnp.exp(bend4 + mprev4 - mnew4)
    m_s[0:1, 0:HEADS] = mnew4
    for hh in range(HEADS):
        sl = slice(hh * dh, (hh + 1) * dh)
        q = q_ref[:, sl]
        k = k_ref[:, sl]
        v = v_ref[:, sl]
        row_vec = xt[hh:hh + 1, :] - br[HEADS + hh:HEADS + hh + 1, :]
        dmat = jnp.exp(jnp.where(tri, row_vec + colv4[:, hh:hh + 1], NEG))
        sc = lax.dot_general(q, k, nt_dims, preferred_element_type=F32) * dmat
        cta = cta_s[hh]
        v_aug = jnp.concatenate([v, one_col.astype(BF16)], axis=1)
        nd = (jnp.dot(sc.astype(BF16), v_aug, preferred_element_type=F32)
              + wint4[:, hh:hh + 1] * jnp.dot(q, cta.astype(BF16), preferred_element_type=F32))
        den = nd[:, dh:dh + 1]
        h = nd[:, 0:dh] / jnp.maximum(jnp.abs(den), emt4[:, hh:hh + 1])
        h_ref[:, sl] = h.astype(h_ref.dtype)
        wj = wj4[:, hh:hh + 1]
        vw = jnp.concatenate([(v.astype(F32) * wj).astype(BF16), (one_col * wj).astype(BF16)], axis=1)
        cta_s[hh] = (decay4[:, hh:hh + 1] * cta
                     + lax.dot_general(k, vw, tn_dims, preferred_element_type=F32))


def _scan_kernel(qf_ref, kf_ref, vf_ref, gf_ref, gtf_ref, qr_ref, kr_ref, vr_ref, gr_ref, gtr_ref,
                 hf_ref, hr_ref, ctaf_s, mf_s, ctar_s, mr_s):
    @pl.when(pl.program_id(1) == 0)
    def _():
        for ref in (ctaf_s, mf_s, ctar_s, mr_s):
            ref[...] = jnp.zeros(ref.shape, F32)

    _scan_direction(True, qf_ref, kf_ref, vf_ref, gf_ref, gtf_ref, hf_ref, ctaf_s, mf_s)
    _scan_direction(False, qr_ref, kr_ref, vr_ref, gr_ref, gtr_ref, hr_ref, ctar_s, mr_s)


def _mlstm_scan(q, k, px, col_v, gates, gates_t, bsz, ntb):
    t, d = q.shape
    dh = d // HEADS
    gh = 2 * HEADS

    def cf(b, i):
        return b * ntb + i

    def cr(b, i):
        return b * ntb + jnp.where(i == 0, 0, ntb - i)

    def specs(chunk, dr):
        return [
            pl.BlockSpec((TILE, d), lambda b, i: (chunk(b, i), 0)),
            pl.BlockSpec((TILE, d), lambda b, i: (chunk(b, i), 0)),
            pl.BlockSpec((TILE, d), lambda b, i: (chunk(b, i), col_v)),
            pl.BlockSpec((None, TILE, gh), lambda b, i: (dr, chunk(b, i), 0)),
            pl.BlockSpec((None, gh, TILE), lambda b, i: (dr, 0, chunk(b, i))),
        ]

    state = [pltpu.VMEM((HEADS, dh, dh + AUG), F32), pltpu.VMEM((SUBLANES, 128), F32)]
    hf, hr = pl.pallas_call(
        _scan_kernel,
        grid=(bsz, ntb),
        in_specs=specs(cf, 0) + specs(cr, 1),
        out_specs=[
            pl.BlockSpec((TILE, d), lambda b, i: (cf(b, i), 0)),
            pl.BlockSpec((TILE, d), lambda b, i: (cr(b, i), 0)),
        ],
        out_shape=[jax.ShapeDtypeStruct((t, d), BF16), jax.ShapeDtypeStruct((t, d), BF16)],
        scratch_shapes=state + state,
        compiler_params=_cparams(("arbitrary", "arbitrary")),
        name="mlstm_scan",
    )(q, k, px, gates, gates_t, q, k, px, gates, gates_t)
    return hf, hr


def _conv_kernel(u_ref, dw_ref, db_ref, lg_ref, lb_ref, wo_ref, o_ref, pad_s):
    i = pl.program_id(1)
    dc = dw_ref.shape[1]
    u = u_ref[...].astype(F32)
    a = u[:, :dc] * _sigmoid(u[:, dc:])
    zeros = jnp.zeros((CONV_HALO, dc), F32)

    def depthwise(seg_len, n_seg):
        stride = seg_len + 2 * CONV_HALO
        used = n_seg * stride
        for s in range(n_seg):
            base = s * stride
            pad_s[0, base:base + CONV_HALO, :] = zeros
            pad_s[0, base + CONV_HALO:base + CONV_HALO + seg_len, :] = a[s * seg_len:(s + 1) * seg_len, :]
            pad_s[0, base + CONV_HALO + seg_len:base + stride, :] = zeros
        for rho in range(1, SUBLANES):
            pad_s[rho, 0:used - SUBLANES, :] = pad_s[0, rho:used - SUBLANES + rho, :]
        outs = []
        for s in range(n_seg):
            base = s * stride
            acc = jnp.zeros((seg_len, dc), F32)
            for j in range(CONV_WIDTH):
                off = base + CONV_HALO - CONV_WIDTH // 2 + j
                rho = off % SUBLANES
                acc = acc + dw_ref[j:j + 1, :] * pad_s[rho, off - rho:off - rho + seg_len, :]
            outs.append(acc)
        return outs[0] if n_seg == 1 else jnp.concatenate(outs, axis=0)

    def finish(acc):
        acc = acc + db_ref[...]
        mu = jnp.mean(acc, axis=-1, keepdims=True)
        var = jnp.mean(jnp.square(acc - mu), axis=-1, keepdims=True)
        y = (acc - mu) * lax.rsqrt(var + LN_EPS) * lg_ref[...] + lb_ref[...]
        y = y * _sigmoid(y)
        o_ref[...] = jnp.dot(y.astype(BF16), wo_ref[...], preferred_element_type=F32).astype(o_ref.dtype)

    @pl.when(i == 0)
    def _():
        finish(depthwise(TILE, 1))

    @pl.when(i > 0)
    def _():
        finish(depthwise(GRID_W, TILE // GRID_W))


def _conv_module(px, col, dw_w, dw_b, ln_g, ln_b, w_o_bf, bsz, ntb):
    t = px.shape[0]
    dc, d = w_o_bf.shape
    return pl.pallas_call(
        _conv_kernel,
        grid=(bsz, ntb),
        in_specs=[
            pl.BlockSpec((TILE, 2 * dc), lambda b, i: (b * ntb + i, col)),
            pl.BlockSpec((CONV_WIDTH, dc), lambda b, i: (0, 0)),
            pl.BlockSpec((1, dc), lambda b, i: (0, 0)),
            pl.BlockSpec((1, dc), lambda b, i: (0, 0)),
            pl.BlockSpec((1, dc), lambda b, i: (0, 0)),
            pl.BlockSpec((dc, d), lambda b, i: (0, 0)),
        ],
        out_specs=pl.BlockSpec((TILE, d), lambda b, i: (b * ntb + i, 0)),
        out_shape=jax.ShapeDtypeStruct((t, d), BF16),
        scratch_shapes=[pltpu.VMEM((SUBLANES, (TILE // GRID_W) * (GRID_W + 2 * CONV_HALO), dc), F32)],
        compiler_params=_cparams(("arbitrary", "arbitrary")),
        name="conv_module",
    )(px, dw_w, dw_b, ln_g, ln_b, w_o_bf)


def _pool_kernel(ctx_len, u_ref, pw_ref, ps_ref, wo_ref, o_ref, d_s):
    nt, dp = u_ref.shape
    pg = dp // len(POOL_WINDOWS)
    rows = (nt - ctx_len) // GRID_W
    uf = u_ref[...].astype(F32)
    tq = lax.broadcasted_iota(jnp.int32, (ctx_len, ctx_len), 0)
    tk = lax.broadcasted_iota(jnp.int32, (ctx_len, ctx_len), 1)
    for g, w in enumerate(POOL_WINDOWS):
        cs = slice(g * pg, (g + 1) * pg)
        lo = jnp.maximum(tq - w // 2, 0)
        hi = jnp.minimum(tq + w // 2 - 1, ctx_len - 1)
        band = jnp.where(tk >= lo, jnp.where(tk <= hi, 1.0, 0.0), 0.0).astype(BF16)
        ssum = jnp.dot(band, u_ref[0:ctx_len, cs], preferred_element_type=F32)
        cnt = (hi - lo + 1)[:, 0:1].astype(F32)
        dd = ssum / cnt - uf[0:ctx_len, cs]
        y = jnp.dot(dd.astype(BF16), pw_ref[g], preferred_element_type=F32)
        d_s[0:ctx_len, cs] = (y * ps_ref[:, cs]).astype(BF16)
    ridx = lax.broadcasted_iota(jnp.int32, (rows, GRID_W, pg), 0)
    for g, w in enumerate(POOL_WINDOWS):
        cs = slice(g * pg, (g + 1) * pg)
        xg = uf[ctx_len:, cs].reshape(rows, GRID_W, pg)
        acc = xg
        for dlt in range(-(w // 2), w // 2):
            if dlt == 0:
                continue
            z = jnp.zeros((abs(dlt), GRID_W, pg), F32)
            if dlt > 0:
                acc = acc + jnp.concatenate([xg[dlt:], z], axis=0)
            else:
                acc = acc + jnp.concatenate([z, xg[:dlt]], axis=0)
        cnt = (jnp.minimum(ridx + w // 2 - 1, rows - 1) - jnp.maximum(ridx - w // 2, 0) + 1).astype(F32)
        dd = (acc / cnt - xg).reshape(rows * GRID_W, pg)
        y = jnp.dot(dd.astype(BF16), pw_ref[g], preferred_element_type=F32)
        d_s[ctx_len:, cs] = (y * ps_ref[:, cs]).astype(BF16)
    o_ref[...] = jnp.dot(d_s[...], wo_ref[...], preferred_element_type=F32).astype(o_ref.dtype)


def _pool_module(px, col, pool_w_bf, pool_scale, w_o_bf, bsz, nt, ctx_len):
    t = px.shape[0]
    dp, d = w_o_bf.shape
    ng = len(POOL_WINDOWS)
    return pl.pallas_call(
        functools.partial(_pool_kernel, ctx_len),
        grid=(bsz,),
        in_specs=[
            pl.BlockSpec((nt, dp), lambda b: (b, col)),
            pl.BlockSpec((ng, dp // ng, dp // ng), lambda b: (0, 0, 0)),
            pl.BlockSpec((1, dp), lambda b: (0, 0)),
            pl.BlockSpec((dp, d), lambda b: (0, 0)),
        ],
        out_specs=pl.BlockSpec((nt, d), lambda b: (b, 0)),
        out_shape=jax.ShapeDtypeStruct((t, d), BF16),
        scratch_shapes=[pltpu.VMEM((nt, dp), BF16)],
        compiler_params=_cparams(("arbitrary",)),
        name="pool_module",
    )(px, pool_w_bf, pool_scale, w_o_bf)


def _mix_out_kernel(x_ref, gate_ref, z_ref, a_ref, hf_ref, hr_ref, yc_ref, yp_ref, mod_ref, ng_ref, skip_ref, wm_ref,
                    wo_ref, g2_ref, rw_ref, rb_ref, xo_ref, hx_ref, lg_ref):
    d = x_ref.shape[1]
    dh = d // HEADS
    hsum = hf_ref[...].astype(F32) + hr_ref[...].astype(F32)
    parts = []
    for hh in range(HEADS):
        hb = hsum[:, hh * dh:(hh + 1) * dh]
        mu = jnp.mean(hb, axis=-1, keepdims=True)
        var = jnp.mean(jnp.square(hb - mu), axis=-1, keepdims=True)
        parts.append((hb - mu) * lax.rsqrt(var + LN_EPS))
    hn = jnp.concatenate(parts, axis=1)
    z = z_ref[...].astype(F32)
    tm = (hn * ng_ref[...] + skip_ref[...] * a_ref[...].astype(F32)) * (z * _sigmoid(z))
    y_m = jnp.dot(tm.astype(BF16), wm_ref[...], preferred_element_type=F32)
    merged = (_sigmoid(gate_ref[:, 0:d].astype(F32)) * yc_ref[...].astype(F32)
              + _sigmoid(gate_ref[:, d:2 * d].astype(F32)) * y_m
              + _sigmoid(gate_ref[:, 2 * d:3 * d].astype(F32)) * yp_ref[...].astype(F32))
    y = jnp.dot(merged.astype(BF16), wo_ref[...], preferred_element_type=F32)
    x = x_ref[...] + mod_ref[2:3, :] * y
    xo_ref[...] = x
    ms = jnp.mean(x * x, axis=-1, keepdims=True)
    hx = x * lax.rsqrt(ms + RMS_EPS) * g2_ref[...]
    hx = hx * (1.0 + mod_ref[4:5, :]) + mod_ref[3:4, :]
    hx_ref[...] = hx.reshape(hx_ref.shape)
    lg_ref[...] = lax.dot_general(rw_ref[...], hx.astype(BF16), (((1,), (1,)), ((), ())),
                                  preferred_element_type=F32) + rb_ref[...]


def _mix_out(xt, px, col_z, a, hf, hr, y_conv, y_pool, modtab, norm_g, skip, w_m_bf, w_out_bf, g2, router_w, router_b,
             ntb, mod_base):
    t, d = xt.shape
    ne = router_w.shape[0]
    row = _mod_row(ntb, mod_base)
    vec = lambda: pl.BlockSpec((1, d), lambda i: (0, 0))
    return pl.pallas_call(
        _mix_out_kernel,
        grid=(t // TILE,),
        in_specs=[
            pl.BlockSpec((TILE, d), lambda i: (i, 0)),
            pl.BlockSpec((TILE, 3 * d), lambda i: (i, 0)),
            pl.BlockSpec((TILE, d), lambda i: (i, col_z)),
            pl.BlockSpec((TILE, d), lambda i: (i, 0)),
            pl.BlockSpec((TILE, d), lambda i: (i, 0)),
            pl.BlockSpec((TILE, d), lambda i: (i, 0)),
            pl.BlockSpec((TILE, d), lambda i: (i, 0)),
            pl.BlockSpec((TILE, d), lambda i: (i, 0)),
            pl.BlockSpec((None, 6, d), lambda i: (row(i), 0, 0)),
            vec(), vec(),
            pl.BlockSpec((d, d), lambda i: (0, 0)),
            pl.BlockSpec((d, d), lambda i: (0, 0)),
            vec(),
            pl.BlockSpec((ne, d), lambda i: (0, 0)),
            pl.BlockSpec((ne, 1), lambda i: (0, 0)),
        ],
        out_specs=[
            pl.BlockSpec((TILE, d), lambda i: (i, 0)),
            pl.BlockSpec((TILE, SUBLANES, d // SUBLANES), lambda i: (i, 0, 0)),
            pl.BlockSpec((ne, TILE), lambda i: (0, i)),
        ],
        out_shape=[
            jax.ShapeDtypeStruct((t, d), F32),
            jax.ShapeDtypeStruct((t, SUBLANES, d // SUBLANES), F32),
            jax.ShapeDtypeStruct((ne, t), F32),
        ],
        compiler_params=_cparams(("arbitrary",)),
        name="mix_out",
    )(xt, px, px, a, hf, hr, y_conv, y_pool, modtab, norm_g, skip, w_m_bf, w_out_bf, g2, router_w, router_b)


def _expert_kernel(be_ref, nu_ref, tok_ref, tokn_ref, hx_hbm, wgu_ref, bgu_ref, wdn_ref, bdn_ref, y_ref,
                   xbuf_a, xbuf_b, xbuf_c, xbuf_d, sem_a, sem_b, sem_c, sem_d, xb_s, act_s, wgu_s, wdn_s):
    i = pl.program_id(0)
    n_used = nu_ref[0]
    half = xbuf_a.shape[0]
    dff = wdn_s.shape[0]
    n_chunk = dff // MOE_CHUNK

    def row_copy(tok_smem, r, xbuf, sem):
        t = tok_smem[0, 0, r]
        return pltpu.make_async_copy(hx_hbm.at[t], xbuf.at[r % half], sem.at[0])

    def wait_rows(xbuf, sem):
        pltpu.make_async_copy(hx_hbm.at[pl.ds(0, half)], xbuf, sem.at[0]).wait()

    @pl.when(i == 0)
    def _():
        def body(r, carry):
            row_copy(tok_ref, r, xbuf_a, sem_a).start()
            row_copy(tok_ref, half + r, xbuf_b, sem_b).start()
            return carry
        lax.fori_loop(0, half, body, 0)

    e_now = be_ref[i]
    e_prev = be_ref[jnp.maximum(i - 1, 0)]

    @pl.when(jnp.logical_or(i == 0, e_now != e_prev))
    def _():
        wgu_s[...] = wgu_ref[...].astype(BF16)
        wdn_s[...] = wdn_ref[...].astype(BF16)

    def half_block(x_cur, sem_cur, out_rows, prefetch):
        wait_rows(x_cur, sem_cur)
        xb_s[...] = x_cur[...].reshape(xb_s.shape).astype(BF16)
        for c in range(n_chunk):
            for r, (xbuf, sem) in (prefetch[c] if prefetch else ()):
                row_copy(tokn_ref, r, xbuf, sem).start(priority=r % 2)
            cg = slice(c * MOE_CHUNK, (c + 1) * MOE_CHUNK)
            cu = slice(dff + c * MOE_CHUNK, dff + (c + 1) * MOE_CHUNK)
            g = jnp.dot(xb_s[...], wgu_s[:, cg], preferred_element_type=F32) + bgu_ref[:, cg]
            u = jnp.dot(xb_s[...], wgu_s[:, cu], preferred_element_type=F32) + bgu_ref[:, cu]
            g = jnp.minimum(g, SWIGLU_LIMIT)
            u = jnp.clip(u, -SWIGLU_LIMIT, SWIGLU_LIMIT)
            act_s[:, cg] = ((u + 1.0) * (g * _sigmoid(SWIGLU_ALPHA * g))).astype(BF16)
        y = jnp.dot(act_s[...], wdn_s[...], preferred_element_type=F32)
        y_ref[out_rows] = (y + bdn_ref[...]).reshape((half,) + y_ref.shape[1:])

    used = i < n_used
    even = i % 2 == 0

    def block(cur, nxt):
        (xa, sa), (xb, sb) = cur
        per = 2 * half // n_chunk
        prefetch = [[(r, nxt[r // half]) for r in range(c * per, (c + 1) * per)] for c in range(n_chunk)]
        half_block(xa, sa, slice(0, half), prefetch)
        half_block(xb, sb, slice(half, 2 * half), None)

    bufs_even = ((xbuf_a, sem_a), (xbuf_b, sem_b))
    bufs_odd = ((xbuf_c, sem_c), (xbuf_d, sem_d))

    @pl.when(jnp.logical_and(used, even))
    def _():
        block(bufs_even, bufs_odd)

    @pl.when(jnp.logical_and(used, jnp.logical_not(even)))
    def _():
        block(bufs_odd, bufs_even)

    @pl.when(jnp.logical_not(used))
    def _():
        @pl.when(jnp.logical_and(i == n_used, even))
        def _():
            wait_rows(xbuf_a, sem_a)
            wait_rows(xbuf_b, sem_b)

        @pl.when(jnp.logical_and(i == n_used, jnp.logical_not(even)))
        def _():
            wait_rows(xbuf_c, sem_c)
            wait_rows(xbuf_d, sem_d)

        y_ref[...] = jnp.zeros(y_ref.shape, y_ref.dtype)


def _experts(layer, hx, row_tok, block_e, n_used, w_gu, b_gu, w_dn, b_dn):
    t, sub, dl = hx.shape
    d = sub * dl
    depth, ne, _, dff2 = w_gu.shape
    dff = dff2 // 2
    nb = block_e.shape[0]
    bm = MOE_BM
    half = bm // 2
    tok3 = row_tok.reshape(nb, 1, bm)
    grid_spec = pltpu.PrefetchScalarGridSpec(
        num_scalar_prefetch=2,
        grid=(nb,),
        in_specs=[
            pl.BlockSpec((1, 1, bm), lambda i, be, nu: (i, 0, 0), memory_space=pltpu.SMEM),
            pl.BlockSpec((1, 1, bm), lambda i, be, nu: (jnp.minimum(i + 1, nb - 1), 0, 0),
                         memory_space=pltpu.SMEM),
            pl.BlockSpec(memory_space=pl.ANY),
            pl.BlockSpec((None, None, d, dff2), lambda i, be, nu: (layer, be[i], 0, 0)),
            pl.BlockSpec((None, None, 1, dff2), lambda i, be, nu: (layer, be[i], 0, 0)),
            pl.BlockSpec((None, None, dff, d), lambda i, be, nu: (layer, be[i], 0, 0)),
            pl.BlockSpec((None, None, 1, d), lambda i, be, nu: (layer, be[i], 0, 0)),
        ],
        out_specs=pl.BlockSpec((bm, sub, dl), lambda i, be, nu: (i, 0, 0)),
        scratch_shapes=[
            pltpu.VMEM((half, sub, dl), F32),
            pltpu.VMEM((half, sub, dl), F32),
            pltpu.VMEM((half, sub, dl), F32),
            pltpu.VMEM((half, sub, dl), F32),
            pltpu.SemaphoreType.DMA((1,)),
            pltpu.SemaphoreType.DMA((1,)),
            pltpu.SemaphoreType.DMA((1,)),
            pltpu.SemaphoreType.DMA((1,)),
            pltpu.VMEM((half, d), BF16),
            pltpu.VMEM((half, dff), BF16),
            pltpu.VMEM((d, dff2), BF16),
            pltpu.VMEM((dff, d), BF16),
        ],
    )
    return pl.pallas_call(
        _expert_kernel,
        grid_spec=grid_spec,
        out_shape=jax.ShapeDtypeStruct((nb * bm, sub, dl), F32),
        compiler_params=_cparams(("arbitrary",), no_bounds_checks=True),
        name="moe_experts",
    )(block_e, n_used, tok3, tok3, hx, w_gu, b_gu.reshape(depth, ne, 1, dff2), w_dn,
      b_dn.reshape(depth, ne, 1, d))


def _combine_kernel(dst_ref, dstn_ref, ys_hbm, gw_ref, x_ref, mod_ref, g_ref, o_ref, buf, sem):
    i = pl.program_id(0)
    nsteps = pl.num_programs(0)
    slot = i % 2
    n_copy = TOP_K * TILE

    def row_copy(dst_smem, e, sl):
        src = dst_smem[0, 0, e]
        kk = e // TILE
        r = e % TILE
        return pltpu.make_async_copy(ys_hbm.at[src], buf.at[sl, kk, r], sem.at[sl])

    @pl.when(i == 0)
    def _():
        def body(e, carry):
            row_copy(dst_ref, e, 0).start()
            return carry
        lax.fori_loop(0, n_copy, body, 0)

    @pl.when(i + 1 < nsteps)
    def _():
        for e in range(n_copy):
            row_copy(dstn_ref, e, 1 - slot).start(priority=e % 2)

    for kk in range(TOP_K):
        pltpu.make_async_copy(ys_hbm.at[pl.ds(0, TILE)], buf.at[slot, kk], sem.at[slot]).wait()

    gw = gw_ref[...]
    y = gw[:, 0:1] * buf[slot, 0].reshape(x_ref.shape)
    for kk in range(1, TOP_K):
        y = y + gw[:, kk:kk + 1] * buf[slot, kk].reshape(x_ref.shape)
    x = x_ref[...] + mod_ref[5:6, :] * y
    ms = jnp.mean(x * x, axis=-1, keepdims=True)
    o_ref[...] = x * lax.rsqrt(ms + RMS_EPS) * g_ref[...]


def _combine_final(ys, dest_tiles, gate_w, xt, modtab, final_g, bsz, ntb, mod_base):
    t, d = xt.shape
    nlat = ntb - 1
    n = bsz * nlat
    row = _mod_row(ntb, mod_base)

    def tile(i):
        return (i // nlat) * ntb + 1 + i % nlat

    return pl.pallas_call(
        _combine_kernel,
        grid=(n,),
        in_specs=[
            pl.BlockSpec((1, 1, TOP_K * TILE), lambda i: (tile(i), 0, 0), memory_space=pltpu.SMEM),
            pl.BlockSpec((1, 1, TOP_K * TILE), lambda i: (tile(jnp.minimum(i + 1, n - 1)), 0, 0),
                         memory_space=pltpu.SMEM),
            pl.BlockSpec(memory_space=pl.ANY),
            pl.BlockSpec((TILE, TOP_K), lambda i: (tile(i), 0)),
            pl.BlockSpec((TILE, d), lambda i: (tile(i), 0)),
            pl.BlockSpec((None, 6, d), lambda i: (row(tile(i)), 0, 0)),
            pl.BlockSpec((1, d), lambda i: (0, 0)),
        ],
        out_specs=pl.BlockSpec((TILE, d), lambda i: (i, 0)),
        out_shape=jax.ShapeDtypeStruct((n * TILE, d), F32),
        scratch_shapes=[
            pltpu.VMEM((2, TOP_K, TILE) + ys.shape[1:], F32),
            pltpu.SemaphoreType.DMA((2,)),
        ],
        compiler_params=_cparams(("arbitrary",), no_bounds_checks=True),
        name="moe_combine_final",
    )(dest_tiles, dest_tiles, ys, gate_w, xt, modtab, final_g)


IN_CHUNKS = 6
IN_DMA_CHUNKS = 4


def _comb_in_kernel(dst_ref, dstn_ref, ys_hbm, gw_ref, x_ref, modp_ref, g_ref, mod_ref, w_ref, xo_ref, o_ref,
                    buf_a, buf_b, sem_a, sem_b, fsem, hb_s):
    i = pl.program_id(0)
    nsteps = pl.num_programs(0)
    n_copy = TOP_K * TILE
    n = w_ref.shape[1]
    cw = n // IN_CHUNKS
    per = n_copy // IN_DMA_CHUNKS

    def row_copy(dst_smem, e, buf, sem):
        src = dst_smem[0, 0, e]
        return pltpu.make_async_copy(ys_hbm.at[src], buf.at[e // TILE, e % TILE], sem.at[0])

    def wait_rows(buf, sem):
        for kk in range(TOP_K):
            pltpu.make_async_copy(ys_hbm.at[pl.ds(0, TILE)], buf.at[kk], sem.at[0]).wait()

    @pl.when(i == 0)
    def _():
        def body(e, carry):
            row_copy(dst_ref, e, buf_a, sem_a).start()
            return carry
        lax.fori_loop(0, n_copy, body, 0)

    def tile(buf_cur, sem_cur, buf_nxt, sem_nxt):
        wait_rows(buf_cur, sem_cur)
        gw = gw_ref[...]
        y = gw[:, 0:1] * buf_cur[0].reshape(x_ref.shape)
        for kk in range(1, TOP_K):
            y = y + gw[:, kk:kk + 1] * buf_cur[kk].reshape(x_ref.shape)
        x = x_ref[...] + modp_ref[5:6, :] * y
        xo_ref[...] = x
        ms = jnp.mean(x * x, axis=-1, keepdims=True)
        h = x * lax.rsqrt(ms + RMS_EPS) * g_ref[...]
        h = h * (1.0 + mod_ref[1:2, :]) + mod_ref[0:1, :]
        hb_s[...] = h.astype(BF16)
        for c in range(IN_CHUNKS):
            if c < IN_DMA_CHUNKS:
                for e in range(c * per, (c + 1) * per):
                    row_copy(dstn_ref, e, buf_nxt, sem_nxt).start(priority=e % 2)
            if c == IN_DMA_CHUNKS:
                pl.semaphore_signal(fsem, 1)
                pl.semaphore_wait(fsem, 1)
            cs = slice(c * cw, (c + 1) * cw)
            o_ref[:, cs] = jnp.dot(hb_s[...], w_ref[:, cs], preferred_element_type=F32).astype(o_ref.dtype)

    even = i % 2 == 0

    @pl.when(even)
    def _():
        tile(buf_a, sem_a, buf_b, sem_b)

    @pl.when(jnp.logical_not(even))
    def _():
        tile(buf_b, sem_b, buf_a, sem_a)

    @pl.when(jnp.logical_and(i == nsteps - 1, even))
    def _():
        wait_rows(buf_b, sem_b)

    @pl.when(jnp.logical_and(i == nsteps - 1, jnp.logical_not(even)))
    def _():
        wait_rows(buf_a, sem_a)


def _comb_in_proj(ys, dest_tiles, gate_w, xt, g1, modtab, w_in_bf, ntb, mod_base_prev, mod_base):
    t, d = xt.shape
    n = w_in_bf.shape[1]
    nt = t // TILE
    assert n % (IN_CHUNKS * 256) == 0 and (TOP_K * TILE) % IN_DMA_CHUNKS == 0
    row_prev = _mod_row(ntb, mod_base_prev)
    row = _mod_row(ntb, mod_base)
    return pl.pallas_call(
        _comb_in_kernel,
        grid=(nt,),
        in_specs=[
            pl.BlockSpec((1, 1, TOP_K * TILE), lambda i: (i, 0, 0), memory_space=pltpu.SMEM),
            pl.BlockSpec((1, 1, TOP_K * TILE), lambda i: (jnp.minimum(i + 1, nt - 1), 0, 0),
                         memory_space=pltpu.SMEM),
            pl.BlockSpec(memory_space=pl.ANY),
            pl.BlockSpec((TILE, TOP_K), lambda i: (i, 0)),
            pl.BlockSpec((TILE, d), lambda i: (i, 0)),
            pl.BlockSpec((None, 6, d), lambda i: (row_prev(i), 0, 0)),
            pl.BlockSpec((1, d), lambda i: (0, 0)),
            pl.BlockSpec((None, 6, d), lambda i: (row(i), 0, 0)),
            pl.BlockSpec((d, n), lambda i: (0, 0), pipeline_mode=pl.Buffered(1)),
        ],
        out_specs=[
            pl.BlockSpec((TILE, d), lambda i: (i, 0)),
            pl.BlockSpec((TILE, n), lambda i: (i, 0)),
        ],
        out_shape=[
            jax.ShapeDtypeStruct((t, d), F32),
            jax.ShapeDtypeStruct((t, n), BF16),
        ],
        scratch_shapes=[
            pltpu.VMEM((TOP_K, TILE) + ys.shape[1:], F32),
            pltpu.VMEM((TOP_K, TILE) + ys.shape[1:], F32),
            pltpu.SemaphoreType.DMA((1,)),
            pltpu.SemaphoreType.DMA((1,)),
            pltpu.SemaphoreType.REGULAR,
            pltpu.VMEM((TILE, d), BF16),
        ],
        compiler_params=_cparams(("arbitrary",), no_bounds_checks=True),
        name="comb_in_proj",
    )(dest_tiles, dest_tiles, ys, gate_w, xt, modtab, g1, modtab, w_in_bf)


def _route_kernel(lg_ref, te_ref, gw_ref, pos_ref, cnt_ref, cnt_s):
    i = pl.program_id(0)

    @pl.when(i == 0)
    def _():
        cnt_s[...] = jnp.zeros(cnt_s.shape, F32)

    lg = lg_ref[...]
    ne = lg.shape[0]
    sub = lax.broadcasted_iota(jnp.int32, lg.shape, 0)
    work = lg
    vals, idxs, hots = [], [], []
    for _ in range(TOP_K):
        m = jnp.max(work, axis=0, keepdims=True)
        idx = jnp.min(jnp.where(work == m, sub, ne), axis=0, keepdims=True)
        hot = sub == idx
        vals.append(m)
        idxs.append(idx)
        hots.append(jnp.where(hot, 1.0, 0.0))
        work = jnp.where(hot, -jnp.inf, work)
    exps = [jnp.exp(v - vals[0]) for v in vals]
    ssum = exps[0]
    for e in exps[1:]:
        ssum = ssum + e
    hot_all = hots[0]
    for h in hots[1:]:
        hot_all = hot_all + h
    r2 = lax.broadcasted_iota(jnp.int32, (TILE, TILE), 0)
    c2 = lax.broadcasted_iota(jnp.int32, (TILE, TILE), 1)
    earlier = jnp.where(r2 < c2, 1.0, 0.0).astype(BF16)
    before = jnp.dot(hot_all.astype(BF16), earlier, preferred_element_type=F32) + cnt_s[...]
    for kk in range(TOP_K):
        te_ref[kk:kk + 1, :] = idxs[kk]
        gw_ref[kk:kk + 1, :] = exps[kk] / ssum
        pos_ref[kk:kk + 1, :] = jnp.sum(hots[kk] * before, axis=0, keepdims=True).astype(jnp.int32)
    cnt_s[...] = cnt_s[...] + jnp.sum(hot_all, axis=1, keepdims=True)
    cnt_ref[...] = cnt_s[...]


def _route(logits_t):
    ne, t = logits_t.shape
    return pl.pallas_call(
        _route_kernel,
        grid=(t // TILE,),
        in_specs=[pl.BlockSpec((ne, TILE), lambda i: (0, i))],
        out_specs=[
            pl.BlockSpec((TOP_K, TILE), lambda i: (0, i)),
            pl.BlockSpec((TOP_K, TILE), lambda i: (0, i)),
            pl.BlockSpec((TOP_K, TILE), lambda i: (0, i)),
            pl.BlockSpec((ne, 1), lambda i: (0, 0)),
        ],
        out_shape=[
            jax.ShapeDtypeStruct((TOP_K, t), jnp.int32),
            jax.ShapeDtypeStruct((TOP_K, t), F32),
            jax.ShapeDtypeStruct((TOP_K, t), jnp.int32),
            jax.ShapeDtypeStruct((ne, 1), F32),
        ],
        scratch_shapes=[pltpu.VMEM((ne, 1), F32)],
        compiler_params=_cparams(("arbitrary",)),
        name="route",
    )(logits_t)


def _routing(logits_t, n_tiles):
    t = logits_t.shape[1]
    top_e, gate, pos, cnt = _route(logits_t)
    n_assign = t * TOP_K
    experts = jnp.arange(N_EXPERTS, dtype=jnp.int32)
    key = (top_e * t + jnp.arange(t, dtype=jnp.int32)[None, :]).reshape(-1)
    order = jnp.argsort(key).astype(jnp.int32)
    counts = cnt.reshape(N_EXPERTS).astype(jnp.int32)
    grp_start = jnp.cumsum(counts) - counts
    padded = (counts + MOE_BM - 1) // MOE_BM * MOE_BM
    pad_end = jnp.cumsum(padded)
    pad_start = pad_end - padded
    dest = pos + jnp.sum(jnp.where(top_e[:, :, None] == experts[None, None, :], pad_start[None, None, :], 0), axis=2)
    n_blocks = n_assign // MOE_BM + N_EXPERTS
    blk = jnp.arange(n_blocks, dtype=jnp.int32)
    block_e = jnp.minimum(jnp.sum((pad_end[None, :] <= (blk * MOE_BM)[:, None]).astype(jnp.int32), axis=1),
                          N_EXPERTS - 1)
    src0 = grp_start[block_e] + blk * MOE_BM - pad_start[block_e]
    lim = grp_start[block_e] + counts[block_e]
    src = src0[:, None] + jnp.arange(MOE_BM, dtype=jnp.int32)[None, :]
    valid = src < lim[:, None]
    row_tok = jnp.where(valid, order[jnp.clip(src, 0, n_assign - 1)] % t, 0).astype(jnp.int32).reshape(-1)
    n_used = (pad_end[-1] // MOE_BM).astype(jnp.int32).reshape(1)
    dest_tiles = dest.reshape(TOP_K, n_tiles, TILE).transpose(1, 0, 2).reshape(n_tiles, 1, TOP_K * TILE)
    return gate.T, row_tok, block_e, n_used, dest_tiles


def _block_diag(w, heads):
    nb, c, _ = w.shape
    per = nb // heads
    w = w.reshape(heads, per, c, c)
    eye = jnp.eye(per, dtype=w.dtype)
    full = eye[None, :, None, :, None] * w[:, :, :, None, :]
    return full.reshape(heads, per * c, per * c)


def kernel(x, c, ctx, c_ctx, w_ada, b_ada, norm1_g, norm2_g, w_in, conv_dw_w, conv_dw_b, conv_ln_g, conv_ln_b, w_conv_out, m_sc_w, m_sc_b, m_wq, m_wk, m_w_if, m_b_if, m_norm_g, m_skip, w_m_out, pool_w, pool_scale, w_pool_out, w_out, router_w, router_b, w_gu, b_gu, w_dn, b_dn, final_g):
    bsz, seq, d = x.shape
    ctx_len = ctx.shape[1]
    depth = w_ada.shape[0]
    assert ctx_len == TILE and seq % TILE == 0 and d % (128 * HEADS) == 0
    nt = ctx_len + seq
    ntb = nt // TILE
    t = bsz * nt
    n_tiles = t // TILE
    d_conv = w_conv_out.shape[1]
    d_pool = w_pool_out.shape[1]
    d_m = w_m_out.shape[1]
    assert d_m == d and 2 * d_conv == d and 2 * d_pool == d
    c_conv = 3 * d
    c_pool = c_conv + 2 * d_conv
    c_mz = c_pool + d_pool
    c_mqk = c_mz + d_m
    c_mv = c_mqk + d_m
    d_in = c_mv + d_m
    col_conv, col_z, col_u, col_v = 3, 4, 5, 6
    col_pool = (7 * d) // d_pool

    xt = jnp.concatenate([ctx, x], axis=1).reshape(t, d)

    mp = -(-(bsz + 1) // 8) * 8
    c_all = jnp.zeros((mp, d), F32).at[:bsz].set(c).at[bsz].set(c_ctx)
    mod = _ada(c_all, w_ada, b_ada)
    mod_x = mod[:, :bsz].reshape(depth, bsz, 1, 6, d)
    mod_c = jnp.broadcast_to(mod[:, bsz].reshape(depth, 1, 1, 6, d), (depth, bsz, 1, 6, d))
    modtab = jnp.concatenate([mod_c, mod_x], axis=2).reshape(depth * bsz * 2, 6, d)

    gh = 2 * HEADS
    for l in range(depth):
        mod_base = l * bsz * 2
        w_in_bf = jnp.concatenate([w_in[l, :, 0:c_pool], w_in[l, :, c_mz:d_in], w_in[l, :, c_pool:c_mz]],
                                  axis=1).astype(BF16)
        if l == 0:
            px = _in_proj(xt, norm1_g[l].reshape(1, d), modtab, w_in_bf, ntb, mod_base)
        else:
            xt, px = _comb_in_proj(ys, dest_tiles, gate_w, xt, norm1_g[l].reshape(1, d), modtab, w_in_bf, ntb,
                                   mod_base - bsz * 2, mod_base)

        wq_bd = _block_diag(m_wq[l], HEADS).astype(BF16)
        wk_bd = _block_diag(m_wk[l], HEADS).astype(BF16)
        wif = jnp.concatenate([m_w_if[l, 0], m_w_if[l, 1]], axis=1)
        bif = jnp.concatenate([m_b_if[l, 0], m_b_if[l, 1]], axis=0)
        a, q, k, gates, gates_t = _mlstm_pre(
            px, col_u, col_v, m_sc_w[l], m_sc_b[l].reshape(1, d), wq_bd, wk_bd,
            wif.astype(BF16), wif.T.astype(BF16), bif.reshape(1, 2 * gh), bif.reshape(2 * gh, 1), bsz, ntb)
        hf, hr = _mlstm_scan(q, k, px, col_v, gates, gates_t, bsz, ntb)

        y_conv = _conv_module(px, col_conv, conv_dw_w[l], conv_dw_b[l].reshape(1, d_conv),
                              conv_ln_g[l].reshape(1, d_conv), conv_ln_b[l].reshape(1, d_conv),
                              w_conv_out[l].astype(BF16), bsz, ntb)
        y_pool = _pool_module(px, col_pool, pool_w[l].astype(BF16), pool_scale[l].reshape(1, d_pool),
                              w_pool_out[l].astype(BF16), bsz, nt, ctx_len)

        xt, hx, logits = _mix_out(
            xt, px, col_z, a, hf, hr, y_conv, y_pool, modtab, m_norm_g[l].reshape(1, d), m_skip[l].reshape(1, d),
            w_m_out[l].astype(BF16), w_out[l].astype(BF16), norm2_g[l].reshape(1, d), router_w[l].T.astype(BF16),
            router_b[l].reshape(N_EXPERTS, 1), ntb, mod_base)

        gate_w, row_tok, block_e, n_used, dest_tiles = _routing(logits, n_tiles)
        ys = _experts(l, hx, row_tok, block_e, n_used, w_gu, b_gu, w_dn, b_dn)

    out = _combine_final(ys, dest_tiles, gate_w, xt, modtab, final_g.reshape(1, d), bsz, ntb,
                         (depth - 1) * bsz * 2)
    return out.reshape(bsz, seq, d)
```

```python
import functools

import jax
import jax.numpy as jnp
from jax import lax
from jax.experimental import pallas as pl
from jax.experimental.pallas import tpu as pltpu

F32 = jnp.float32
BF16 = jnp.bfloat16
HIGHEST = lax.Precision.HIGHEST

TILE = 256
GRID_W = 64
CONV_WIDTH = 31
SUBLANES = 8
CONV_HALO = 16
SHORT_CONV = 4
HEADS = 4
POOL_WINDOWS = (2, 4, 8, 16)
N_EXPERTS = 32
TOP_K = 4
MOE_BM = 512
MOE_CHUNK = 256
SWIGLU_LIMIT = 7.0
SWIGLU_ALPHA = 1.702
RMS_EPS = 1e-6
LN_EPS = 1e-5
NEG = -1e30
VMEM_LIMIT = 56 * 1024 * 1024


def _cparams(sem, no_bounds_checks=False):
    return pltpu.CompilerParams(dimension_semantics=sem, vmem_limit_bytes=VMEM_LIMIT,
                                disable_bounds_checks=no_bounds_checks)


def _sigmoid(x):
    return jax.nn.sigmoid(x)


def _log_sigmoid(x):
    return jnp.minimum(x, 0.0) - jnp.log1p(jnp.exp(-jnp.abs(x)))


def _ada_kernel(c_ref, w_ref, b_ref, o_ref):
    s = c_ref[...]
    s = s * _sigmoid(s)
    o_ref[...] = jnp.dot(s, w_ref[...], precision=HIGHEST, preferred_element_type=F32) + b_ref[...]


def _ada(c_all, w_ada, b_ada):
    depth, d, n = w_ada.shape
    mp = c_all.shape[0]
    tn = 1536
    return pl.pallas_call(
        _ada_kernel,
        grid=(depth, n // tn),
        in_specs=[
            pl.BlockSpec((mp, d), lambda l, j: (0, 0)),
            pl.BlockSpec((None, d, tn), lambda l, j: (l, 0, j)),
            pl.BlockSpec((None, 1, tn), lambda l, j: (l, 0, j)),
        ],
        out_specs=pl.BlockSpec((None, mp, tn), lambda l, j: (l, 0, j)),
        out_shape=jax.ShapeDtypeStruct((depth, mp, n), F32),
        compiler_params=_cparams(("arbitrary", "arbitrary")),
        name="ada",
    )(c_all, w_ada, b_ada.reshape(depth, 1, n))


def _in_kernel(x_ref, g_ref, mod_ref, w_ref, o_ref):
    x = x_ref[...]
    ms = jnp.mean(x * x, axis=-1, keepdims=True)
    y = x * lax.rsqrt(ms + RMS_EPS) * g_ref[...]
    h = y * (1.0 + mod_ref[1:2, :]) + mod_ref[0:1, :]
    o_ref[...] = jnp.dot(h.astype(BF16), w_ref[...], preferred_element_type=F32).astype(BF16)


def _mod_row(ntb, base):
    def f(i):
        return base + 2 * (i // ntb) + jnp.minimum(i % ntb, 1)
    return f


def _in_proj(xt, g1, modtab, w_in_bf, ntb, mod_base):
    t, d = xt.shape
    n = w_in_bf.shape[1]
    row = _mod_row(ntb, mod_base)
    return pl.pallas_call(
        _in_kernel,
        grid=(t // TILE,),
        in_specs=[
            pl.BlockSpec((TILE, d), lambda i: (i, 0)),
            pl.BlockSpec((1, d), lambda i: (0, 0)),
            pl.BlockSpec((None, 6, d), lambda i: (row(i), 0, 0)),
            pl.BlockSpec((d, n), lambda i: (0, 0), pipeline_mode=pl.Buffered(1)),
        ],
        out_specs=pl.BlockSpec((TILE, n), lambda i: (i, 0)),
        out_shape=jax.ShapeDtypeStruct((t, n), BF16),
        compiler_params=_cparams(("arbitrary",)),
        name="in_proj",
    )(xt, g1, modtab, w_in_bf)


def _mpre_kernel(ntb, u_ref, up_ref, un_ref, v_ref, scw_ref, scb_ref, wq_ref, wk_ref, wif_ref, wift_ref,
                 bif_ref, bift_ref, a_ref, q_ref, k_ref, g_ref, gt_ref):
    i = pl.program_id(1)
    d = u_ref.shape[1]
    dh = d // HEADS
    u = u_ref[...].astype(F32)
    up = jnp.where(i >= 2, up_ref[...].astype(F32)[8:16, :], 0.0)
    un = jnp.where(jnp.logical_and(i >= 1, i <= ntb - 2), un_ref[...].astype(F32)[0:8, :], 0.0)
    ext = jnp.concatenate([up, u, un], axis=0)
    n_ext = TILE + 16
    conv = scb_ref[...]
    for j in range(SHORT_CONV):
        sh = (SHORT_CONV // 2 - j) % n_ext
        shifted = ext if sh == 0 else pltpu.roll(ext, sh, 0)
        conv = conv + scw_ref[j:j + 1, :] * shifted[8:8 + TILE, :]
    a = conv * _sigmoid(conv)
    ab = a.astype(BF16)
    a_ref[...] = ab
    qs, ks = [], []
    for hb in range(HEADS):
        blk = ab[:, hb * dh:(hb + 1) * dh]
        qs.append(jnp.dot(blk, wq_ref[hb], preferred_element_type=F32))
        ks.append(jnp.dot(blk, wk_ref[hb], preferred_element_type=F32))
    q = jnp.concatenate(qs, axis=1)
    k = jnp.concatenate(ks, axis=1)
    qb = q.astype(BF16)
    kb = k.astype(BF16)
    vb = v_ref[...]
    q_ref[...] = (q * (dh ** -0.5)).astype(BF16)
    k_ref[...] = kb
    pre = bif_ref[...]
    pret = bift_ref[...]
    nt_dims = (((1,), (1,)), ((), ()))
    for part, f in enumerate((qb, kb, vb)):
        pre = pre + jnp.dot(f, wif_ref[part * d:(part + 1) * d, :], preferred_element_type=F32)
        pret = pret + lax.dot_general(wift_ref[:, part * d:(part + 1) * d], f, nt_dims,
                                      preferred_element_type=F32)
    col = lax.broadcasted_iota(jnp.int32, pre.shape, 1)
    pre = jnp.where(col % (2 * HEADS) >= HEADS, _log_sigmoid(pre), pre)
    rowi = lax.broadcasted_iota(jnp.int32, pret.shape, 0)
    pret = jnp.where(rowi % (2 * HEADS) >= HEADS, _log_sigmoid(pret), pret)
    g_ref[0] = pre[:, 0:2 * HEADS]
    g_ref[1] = pre[:, 2 * HEADS:4 * HEADS]
    gt_ref[0] = pret[0:2 * HEADS, :]
    gt_ref[1] = pret[2 * HEADS:4 * HEADS, :]


def _mlstm_pre(px, col_u, col_v, sc_w, sc_b, wq_bd, wk_bd, wif, wift, bif, bift, bsz, ntb):
    t = px.shape[0]
    d = sc_w.shape[1]
    hrows = 16
    nh = t // hrows
    per = TILE // hrows

    def tile(b, i):
        return b * ntb + i

    gh = 2 * HEADS
    return pl.pallas_call(
        functools.partial(_mpre_kernel, ntb),
        grid=(bsz, ntb),
        in_specs=[
            pl.BlockSpec((TILE, d), lambda b, i: (tile(b, i), col_u)),
            pl.BlockSpec((hrows, d), lambda b, i: (jnp.maximum(tile(b, i) * per - 1, 0), col_u)),
            pl.BlockSpec((hrows, d), lambda b, i: (jnp.minimum((tile(b, i) + 1) * per, nh - 1), col_u)),
            pl.BlockSpec((TILE, d), lambda b, i: (tile(b, i), col_v)),
            pl.BlockSpec((SHORT_CONV, d), lambda b, i: (0, 0)),
            pl.BlockSpec((1, d), lambda b, i: (0, 0)),
            pl.BlockSpec((HEADS, d // HEADS, d // HEADS), lambda b, i: (0, 0, 0)),
            pl.BlockSpec((HEADS, d // HEADS, d // HEADS), lambda b, i: (0, 0, 0)),
            pl.BlockSpec((3 * d, 2 * gh), lambda b, i: (0, 0)),
            pl.BlockSpec((2 * gh, 3 * d), lambda b, i: (0, 0)),
            pl.BlockSpec((1, 2 * gh), lambda b, i: (0, 0)),
            pl.BlockSpec((2 * gh, 1), lambda b, i: (0, 0)),
        ],
        out_specs=[
            pl.BlockSpec((TILE, d), lambda b, i: (tile(b, i), 0)),
            pl.BlockSpec((TILE, d), lambda b, i: (tile(b, i), 0)),
            pl.BlockSpec((TILE, d), lambda b, i: (tile(b, i), 0)),
            pl.BlockSpec((2, TILE, gh), lambda b, i: (0, tile(b, i), 0)),
            pl.BlockSpec((2, gh, TILE), lambda b, i: (0, 0, tile(b, i))),
        ],
        out_shape=[
            jax.ShapeDtypeStruct((t, d), BF16),
            jax.ShapeDtypeStruct((t, d), BF16),
            jax.ShapeDtypeStruct((t, d), BF16),
            jax.ShapeDtypeStruct((2, t, gh), F32),
            jax.ShapeDtypeStruct((2, gh, t), F32),
        ],
        compiler_params=_cparams(("arbitrary", "arbitrary")),
        name="mlstm_pre",
    )(px, px, px, px, sc_w, sc_b, wq_bd, wk_bd, wif, wift, bif, bift)


AUG = 128


def _scan_direction(fwd, q_ref, k_ref, v_ref, g_ref, gt_ref, h_ref, cta_s, m_s):
    dh = q_ref.shape[1] // HEADS
    x = g_ref[...]
    xt = gt_ref[...]
    rid = lax.broadcasted_iota(jnp.int32, x.shape, 0)
    cid = lax.broadcasted_iota(jnp.int32, xt.shape, 1)
    p = x
    pt = xt
    s = 1
    while s < TILE:
        p = p + jnp.where(rid >= s, pltpu.roll(p, s, 0), 0.0)
        pt = pt + jnp.where(cid >= s, pltpu.roll(pt, s, 1), 0.0)
        s *= 2
    tot = p[TILE - 1:TILE, :]
    tott = pt[:, TILE - 1:TILE]
    bc = p if fwd else tot - p + x
    br = pt if fwd else tott - pt + xt
    cmax = x[:, 0:HEADS] - bc[:, HEADS:2 * HEADS]
    rida = lax.broadcasted_iota(jnp.int32, cmax.shape, 0)
    s = 1
    while s < TILE:
        if fwd:
            cmax = jnp.maximum(cmax, jnp.where(rida >= s, pltpu.roll(cmax, s, 0), NEG))
        else:
            cmax = jnp.maximum(cmax, jnp.where(rida < TILE - s, pltpu.roll(cmax, TILE - s, 0), NEG))
        s *= 2
    r2 = lax.broadcasted_iota(jnp.int32, (TILE, TILE), 0)
    c2 = lax.broadcasted_iota(jnp.int32, (TILE, TILE), 1)
    tri = r2 >= c2 if fwd else r2 <= c2
    lane = lax.broadcasted_iota(jnp.int32, (TILE, AUG), 1)
    one_col = jnp.where(lane == 0, 1.0, 0.0)
    nt_dims = (((1,), (1,)), ((), ()))
    tn_dims = (((0,), (0,)), ((), ()))
    b4 = bc[:, HEADS:2 * HEADS]
    bend4 = tot[:, HEADS:2 * HEADS]
    mprev4 = m_s[0:1, 0:HEADS]
    colv4 = -jnp.maximum(mprev4, cmax)
    wint4 = jnp.exp(mprev4 + colv4)
    emt4 = jnp.exp(colv4 - b4)
    g4 = bend4 - b4 + x[:, 0:HEADS]
    mnew4 = jnp.maximum(bend4 + mprev4, jnp.max(g4, axis=0, keepdims=True))
    wj4 = jnp.exp(g4 - mnew4)
    decay4 = jnp.exp(bend4 + mprev4 - mnew4)
    m_s[0:1, 0:HEADS] = mnew4
    for hh in range(HEADS):
        sl = slice(hh * dh, (hh + 1) * dh)
        q = q_ref[:, sl]
        k = k_ref[:, sl]
        v = v_ref[:, sl]
        row_vec = xt[hh:hh + 1, :] - br[HEADS + hh:HEADS + hh + 1, :]
        dmat = jnp.exp(jnp.where(tri, row_vec + colv4[:, hh:hh + 1], NEG))
        sc = lax.dot_general(q, k, nt_dims, preferred_element_type=F32) * dmat
        cta = cta_s[hh]
        v_aug = jnp.concatenate([v, one_col.astype(BF16)], axis=1)
        nd = (jnp.dot(sc.astype(BF16), v_aug, preferred_element_type=F32)
              + wint4[:, hh:hh + 1] * jnp.dot(q, cta.astype(BF16), preferred_element_type=F32))
        den = nd[:, dh:dh + 1]
        h = nd[:, 0:dh] / jnp.maximum(jnp.abs(den), emt4[:, hh:hh + 1])
        h_ref[:, sl] = h.astype(h_ref.dtype)
        wj = wj4[:, hh:hh + 1]
        vw = jnp.concatenate([(v.astype(F32) * wj).astype(BF16), (one_col * wj).astype(BF16)], axis=1)
        cta_s[hh] = (decay4[:, hh:hh + 1] * cta
                     + lax.dot_general(k, vw, tn_dims, preferred_element_type=F32))


def _scan_kernel(qf_ref, kf_ref, vf_ref, gf_ref, gtf_ref, qr_ref, kr_ref, vr_ref, gr_ref, gtr_ref,
                 hf_ref, hr_ref, ctaf_s, mf_s, ctar_s, mr_s):
    @pl.when(pl.program_id(1) == 0)
    def _():
        for ref in (ctaf_s, mf_s, ctar_s, mr_s):
            ref[...] = jnp.zeros(ref.shape, F32)

    _scan_direction(True, qf_ref, kf_ref, vf_ref, gf_ref, gtf_ref, hf_ref, ctaf_s, mf_s)
    _scan_direction(False, qr_ref, kr_ref, vr_ref, gr_ref, gtr_ref, hr_ref, ctar_s, mr_s)


def _mlstm_scan(q, k, px, col_v, gates, gates_t, bsz, ntb):
    t, d = q.shape
    dh = d // HEADS
    gh = 2 * HEADS

    def cf(b, i):
        return b * ntb + i

    def cr(b, i):
        return b * ntb + jnp.where(i == 0, 0, ntb - i)

    def specs(chunk, dr):
        return [
            pl.BlockSpec((TILE, d), lambda b, i: (chunk(b, i), 0)),
            pl.BlockSpec((TILE, d), lambda b, i: (chunk(b, i), 0)),
            pl.BlockSpec((TILE, d), lambda b, i: (chunk(b, i), col_v)),
            pl.BlockSpec((None, TILE, gh), lambda b, i: (dr, chunk(b, i), 0)),
            pl.BlockSpec((None, gh, TILE), lambda b, i: (dr, 0, chunk(b, i))),
        ]

    state = [pltpu.VMEM((HEADS, dh, dh + AUG), F32), pltpu.VMEM((SUBLANES, 128), F32)]
    hf, hr = pl.pallas_call(
        _scan_kernel,
        grid=(bsz, ntb),
        in_specs=specs(cf, 0) + specs(cr, 1),
        out_specs=[
            pl.BlockSpec((TILE, d), lambda b, i: (cf(b, i), 0)),
            pl.BlockSpec((TILE, d), lambda b, i: (cr(b, i), 0)),
        ],
        out_shape=[jax.ShapeDtypeStruct((t, d), BF16), jax.ShapeDtypeStruct((t, d), BF16)],
        scratch_shapes=state + state,
        compiler_params=_cparams(("arbitrary", "arbitrary")),
        name="mlstm_scan",
    )(q, k, px, gates, gates_t, q, k, px, gates, gates_t)
    return hf, hr


def _conv_kernel(u_ref, dw_ref, db_ref, lg_ref, lb_ref, wo_ref, o_ref, pad_s):
    i = pl.program_id(1)
    dc = dw_ref.shape[1]
    u = u_ref[...].astype(F32)
    a = u[:, :dc] * _sigmoid(u[:, dc:])
    zeros = jnp.zeros((CONV_HALO, dc), F32)

    def depthwise(seg_len, n_seg):
        stride = seg_len + 2 * CONV_HALO
        used = n_seg * stride
        for s in range(n_seg):
            base = s * stride
            pad_s[0, base:base + CONV_HALO, :] = zeros
            pad_s[0, base + CONV_HALO:base + CONV_HALO + seg_len, :] = a[s * seg_len:(s + 1) * seg_len, :]
            pad_s[0, base + CONV_HALO + seg_len:base + stride, :] = zeros
        for rho in range(1, SUBLANES):
            pad_s[rho, 0:used - SUBLANES, :] = pad_s[0, rho:used - SUBLANES + rho, :]
        outs = []
        for s in range(n_seg):
            base = s * stride
            acc = jnp.zeros((seg_len, dc), F32)
            for j in range(CONV_WIDTH):
                off = base + CONV_HALO - CONV_WIDTH // 2 + j
                rho = off % SUBLANES
                acc = acc + dw_ref[j:j + 1, :] * pad_s[rho, off - rho:off - rho + seg_len, :]
            outs.append(acc)
        return outs[0] if n_seg == 1 else jnp.concatenate(outs, axis=0)

    def finish(acc):
        acc = acc + db_ref[...]
        mu = jnp.mean(acc, axis=-1, keepdims=True)
        var = jnp.mean(jnp.square(acc - mu), axis=-1, keepdims=True)
        y = (acc - mu) * lax.rsqrt(var + LN_EPS) * lg_ref[...] + lb_ref[...]
        y = y * _sigmoid(y)
        o_ref[...] = jnp.dot(y.astype(BF16), wo_ref[...], preferred_element_type=F32).astype(o_ref.dtype)

    @pl.when(i == 0)
    def _():
        finish(depthwise(TILE, 1))

    @pl.when(i > 0)
    def _():
        finish(depthwise(GRID_W, TILE // GRID_W))


def _conv_module(px, col, dw_w, dw_b, ln_g, ln_b, w_o_bf, bsz, ntb):
    t = px.shape[0]
    dc, d = w_o_bf.shape
    return pl.pallas_call(
        _conv_kernel,
        grid=(bsz, ntb),
        in_specs=[
            pl.BlockSpec((TILE, 2 * dc), lambda b, i: (b * ntb + i, col)),
            pl.BlockSpec((CONV_WIDTH, dc), lambda b, i: (0, 0)),
            pl.BlockSpec((1, dc), lambda b, i: (0, 0)),
            pl.BlockSpec((1, dc), lambda b, i: (0, 0)),
            pl.BlockSpec((1, dc), lambda b, i: (0, 0)),
            pl.BlockSpec((dc, d), lambda b, i: (0, 0)),
        ],
        out_specs=pl.BlockSpec((TILE, d), lambda b, i: (b * ntb + i, 0)),
        out_shape=jax.ShapeDtypeStruct((t, d), BF16),
        scratch_shapes=[pltpu.VMEM((SUBLANES, (TILE // GRID_W) * (GRID_W + 2 * CONV_HALO), dc), F32)],
        compiler_params=_cparams(("arbitrary", "arbitrary")),
        name="conv_module",
    )(px, dw_w, dw_b, ln_g, ln_b, w_o_bf)


def _pool_kernel(ctx_len, u_ref, pw_ref, ps_ref, wo_ref, o_ref, d_s):
    nt, dp = u_ref.shape
    pg = dp // len(POOL_WINDOWS)
    rows = (nt - ctx_len) // GRID_W
    uf = u_ref[...].astype(F32)
    tq = lax.broadcasted_iota(jnp.int32, (ctx_len, ctx_len), 0)
    tk = lax.broadcasted_iota(jnp.int32, (ctx_len, ctx_len), 1)
    for g, w in enumerate(POOL_WINDOWS):
        cs = slice(g * pg, (g + 1) * pg)
        lo = jnp.maximum(tq - w // 2, 0)
        hi = jnp.minimum(tq + w // 2 - 1, ctx_len - 1)
        band = jnp.where(tk >= lo, jnp.where(tk <= hi, 1.0, 0.0), 0.0).astype(BF16)
        ssum = jnp.dot(band, u_ref[0:ctx_len, cs], preferred_element_type=F32)
        cnt = (hi - lo + 1)[:, 0:1].astype(F32)
        dd = ssum / cnt - uf[0:ctx_len, cs]
        y = jnp.dot(dd.astype(BF16), pw_ref[g], preferred_element_type=F32)
        d_s[0:ctx_len, cs] = (y * ps_ref[:, cs]).astype(BF16)
    ridx = lax.broadcasted_iota(jnp.int32, (rows, GRID_W, pg), 0)
    for g, w in enumerate(POOL_WINDOWS):
        cs = slice(g * pg, (g + 1) * pg)
        xg = uf[ctx_len:, cs].reshape(rows, GRID_W, pg)
        acc = xg
        for dlt in range(-(w // 2), w // 2):
            if dlt == 0:
                continue
            z = jnp.zeros((abs(dlt), GRID_W, pg), F32)
            if dlt > 0:
                acc = acc + jnp.concatenate([xg[dlt:], z], axis=0)
            else:
                acc = acc + jnp.concatenate([z, xg[:dlt]], axis=0)
        cnt = (jnp.minimum(ridx + w // 2 - 1, rows - 1) - jnp.maximum(ridx - w // 2, 0) + 1).astype(F32)
        dd = (acc / cnt - xg).reshape(rows * GRID_W, pg)
        y = jnp.dot(dd.astype(BF16), pw_ref[g], preferred_element_type=F32)
        d_s[ctx_len:, cs] = (y * ps_ref[:, cs]).astype(BF16)
    o_ref[...] = jnp.dot(d_s[...], wo_ref[...], preferred_element_type=F32).astype(o_ref.dtype)


def _pool_module(px, col, pool_w_bf, pool_scale, w_o_bf, bsz, nt, ctx_len):
    t = px.shape[0]
    dp, d = w_o_bf.shape
    ng = len(POOL_WINDOWS)
    return pl.pallas_call(
        functools.partial(_pool_kernel, ctx_len),
        grid=(bsz,),
        in_specs=[
            pl.BlockSpec((nt, dp), lambda b: (b, col)),
            pl.BlockSpec((ng, dp // ng, dp // ng), lambda b: (0, 0, 0)),
            pl.BlockSpec((1, dp), lambda b: (0, 0)),
            pl.BlockSpec((dp, d), lambda b: (0, 0)),
        ],
        out_specs=pl.BlockSpec((nt, d), lambda b: (b, 0)),
        out_shape=jax.ShapeDtypeStruct((t, d), BF16),
        scratch_shapes=[pltpu.VMEM((nt, dp), BF16)],
        compiler_params=_cparams(("arbitrary",)),
        name="pool_module",
    )(px, pool_w_bf, pool_scale, w_o_bf)


def _mix_out_kernel(x_ref, gate_ref, z_ref, a_ref, hf_ref, hr_ref, yc_ref, yp_ref, mod_ref, ng_ref, skip_ref, wm_ref,
                    wo_ref, g2_ref, rw_ref, rb_ref, xo_ref, hx_ref, te_ref, gw_ref, pos_ref, cnt_ref, cnt_s):
    d = x_ref.shape[1]
    dh = d // HEADS
    hsum = hf_ref[...].astype(F32) + hr_ref[...].astype(F32)
    parts = []
    for hh in range(HEADS):
        hb = hsum[:, hh * dh:(hh + 1) * dh]
        mu = jnp.mean(hb, axis=-1, keepdims=True)
        var = jnp.mean(jnp.square(hb - mu), axis=-1, keepdims=True)
        parts.append((hb - mu) * lax.rsqrt(var + LN_EPS))
    hn = jnp.concatenate(parts, axis=1)
    z = z_ref[...].astype(F32)
    tm = (hn * ng_ref[...] + skip_ref[...] * a_ref[...].astype(F32)) * (z * _sigmoid(z))
    y_m = jnp.dot(tm.astype(BF16), wm_ref[...], preferred_element_type=F32)
    merged = (_sigmoid(gate_ref[:, 0:d].astype(F32)) * yc_ref[...].astype(F32)
              + _sigmoid(gate_ref[:, d:2 * d].astype(F32)) * y_m
              + _sigmoid(gate_ref[:, 2 * d:3 * d].astype(F32)) * yp_ref[...].astype(F32))
    y = jnp.dot(merged.astype(BF16), wo_ref[...], preferred_element_type=F32)
    x = x_ref[...] + mod_ref[2:3, :] * y
    xo_ref[...] = x
    ms = jnp.mean(x * x, axis=-1, keepdims=True)
    hx = x * lax.rsqrt(ms + RMS_EPS) * g2_ref[...]
    hx = hx * (1.0 + mod_ref[4:5, :]) + mod_ref[3:4, :]
    hx_ref[...] = hx.reshape(hx_ref.shape)
    lg = lax.dot_general(rw_ref[...], hx.astype(BF16), (((1,), (1,)), ((), ())),
                         preferred_element_type=F32) + rb_ref[...]
    _route_tile(lg, te_ref, gw_ref, pos_ref, cnt_ref, cnt_s)


def _mix_out(xt, px, col_z, a, hf, hr, y_conv, y_pool, modtab, norm_g, skip, w_m_bf, w_out_bf, g2, router_w, router_b,
             ntb, mod_base):
    t, d = xt.shape
    ne = router_w.shape[0]
    row = _mod_row(ntb, mod_base)
    vec = lambda: pl.BlockSpec((1, d), lambda i: (0, 0))
    return pl.pallas_call(
        _mix_out_kernel,
        grid=(t // TILE,),
        in_specs=[
            pl.BlockSpec((TILE, d), lambda i: (i, 0)),
            pl.BlockSpec((TILE, 3 * d), lambda i: (i, 0)),
            pl.BlockSpec((TILE, d), lambda i: (i, col_z)),
            pl.BlockSpec((TILE, d), lambda i: (i, 0)),
            pl.BlockSpec((TILE, d), lambda i: (i, 0)),
            pl.BlockSpec((TILE, d), lambda i: (i, 0)),
            pl.BlockSpec((TILE, d), lambda i: (i, 0)),
            pl.BlockSpec((TILE, d), lambda i: (i, 0)),
            pl.BlockSpec((None, 6, d), lambda i: (row(i), 0, 0)),
            vec(), vec(),
            pl.BlockSpec((d, d), lambda i: (0, 0)),
            pl.BlockSpec((d, d), lambda i: (0, 0)),
            vec(),
            pl.BlockSpec((ne, d), lambda i: (0, 0)),
            pl.BlockSpec((ne, 1), lambda i: (0, 0)),
        ],
        out_specs=[
            pl.BlockSpec((TILE, d), lambda i: (i, 0)),
            pl.BlockSpec((TILE, SUBLANES, d // SUBLANES), lambda i: (i, 0, 0)),
            pl.BlockSpec((TOP_K, TILE), lambda i: (0, i)),
            pl.BlockSpec((TOP_K, TILE), lambda i: (0, i)),
            pl.BlockSpec((TOP_K, TILE), lambda i: (0, i)),
            pl.BlockSpec((ne, 1), lambda i: (0, 0)),
        ],
        out_shape=[
            jax.ShapeDtypeStruct((t, d), F32),
            jax.ShapeDtypeStruct((t, SUBLANES, d // SUBLANES), F32),
            jax.ShapeDtypeStruct((TOP_K, t), jnp.int32),
            jax.ShapeDtypeStruct((TOP_K, t), F32),
            jax.ShapeDtypeStruct((TOP_K, t), jnp.int32),
            jax.ShapeDtypeStruct((ne, 1), F32),
        ],
        scratch_shapes=[pltpu.VMEM((ne, 1), F32)],
        compiler_params=_cparams(("arbitrary",)),
        name="mix_out",
    )(xt, px, px, a, hf, hr, y_conv, y_pool, modtab, norm_g, skip, w_m_bf, w_out_bf, g2, router_w, router_b)


def _expert_kernel(be_ref, nu_ref, tok_ref, tokn_ref, hx_hbm, wgu_ref, bgu_ref, wdn_ref, bdn_ref, y_ref,
                   xbuf_a, xbuf_b, xbuf_c, xbuf_d, sem_a, sem_b, sem_c, sem_d, xb_s, act_s, wgu_s, wdn_s):
    i = pl.program_id(0)
    n_used = nu_ref[0]
    half = xbuf_a.shape[0]
    dff = wdn_s.shape[0]
    n_chunk = dff // MOE_CHUNK

    def row_copy(tok_smem, r, xbuf, sem):
        t = tok_smem[0, 0, r]
        return pltpu.make_async_copy(hx_hbm.at[t], xbuf.at[r % half], sem.at[0])

    def wait_rows(xbuf, sem):
        pltpu.make_async_copy(hx_hbm.at[pl.ds(0, half)], xbuf, sem.at[0]).wait()

    @pl.when(i == 0)
    def _():
        def body(r, carry):
            row_copy(tok_ref, r, xbuf_a, sem_a).start()
            row_copy(tok_ref, half + r, xbuf_b, sem_b).start()
            return carry
        lax.fori_loop(0, half, body, 0)

    e_now = be_ref[i]
    e_prev = be_ref[jnp.maximum(i - 1, 0)]

    @pl.when(jnp.logical_or(i == 0, e_now != e_prev))
    def _():
        wgu_s[...] = wgu_ref[...].astype(BF16)
        wdn_s[...] = wdn_ref[...].astype(BF16)

    def half_block(x_cur, sem_cur, out_rows, prefetch):
        wait_rows(x_cur, sem_cur)
        xb_s[...] = x_cur[...].reshape(xb_s.shape).astype(BF16)
        for c in range(n_chunk):
            for r, (xbuf, sem) in (prefetch[c] if prefetch else ()):
                row_copy(tokn_ref, r, xbuf, sem).start(priority=r % 2)
            cg = slice(c * MOE_CHUNK, (c + 1) * MOE_CHUNK)
            cu = slice(dff + c * MOE_CHUNK, dff + (c + 1) * MOE_CHUNK)
            g = jnp.dot(xb_s[...], wgu_s[:, cg], preferred_element_type=F32) + bgu_ref[:, cg]
            u = jnp.dot(xb_s[...], wgu_s[:, cu], preferred_element_type=F32) + bgu_ref[:, cu]
            g = jnp.minimum(g, SWIGLU_LIMIT)
            u = jnp.clip(u, -SWIGLU_LIMIT, SWIGLU_LIMIT)
            act_s[:, cg] = ((u + 1.0) * (g * _sigmoid(SWIGLU_ALPHA * g))).astype(BF16)
        y = jnp.dot(act_s[...], wdn_s[...], preferred_element_type=F32)
        y_ref[out_rows] = (y + bdn_ref[...]).reshape((half,) + y_ref.shape[1:])

    used = i < n_used
    even = i % 2 == 0

    def block(cur, nxt):
        (xa, sa), (xb, sb) = cur
        per = 2 * half // n_chunk
        prefetch = [[(r, nxt[r // half]) for r in range(c * per, (c + 1) * per)] for c in range(n_chunk)]
        half_block(xa, sa, slice(0, half), prefetch)
        half_block(xb, sb, slice(half, 2 * half), None)

    bufs_even = ((xbuf_a, sem_a), (xbuf_b, sem_b))
    bufs_odd = ((xbuf_c, sem_c), (xbuf_d, sem_d))

    @pl.when(jnp.logical_and(used, even))
    def _():
        block(bufs_even, bufs_odd)

    @pl.when(jnp.logical_and(used, jnp.logical_not(even)))
    def _():
        block(bufs_odd, bufs_even)

    @pl.when(jnp.logical_not(used))
    def _():
        @pl.when(jnp.logical_and(i == n_used, even))
        def _():
            wait_rows(xbuf_a, sem_a)
            wait_rows(xbuf_b, sem_b)

        @pl.when(jnp.logical_and(i == n_used, jnp.logical_not(even)))
        def _():
            wait_rows(xbuf_c, sem_c)
            wait_rows(xbuf_d, sem_d)

        y_ref[...] = jnp.zeros(y_ref.shape, y_ref.dtype)


def _experts(layer, hx, row_tok, block_e, n_used, w_gu, b_gu, w_dn, b_dn):
    t, sub, dl = hx.shape
    d = sub * dl
    depth, ne, _, dff2 = w_gu.shape
    dff = dff2 // 2
    nb = block_e.shape[0]
    bm = MOE_BM
    half = bm // 2
    tok3 = row_tok.reshape(nb, 1, bm)
    grid_spec = pltpu.PrefetchScalarGridSpec(
        num_scalar_prefetch=2,
        grid=(nb,),
        in_specs=[
            pl.BlockSpec((1, 1, bm), lambda i, be, nu: (i, 0, 0), memory_space=pltpu.SMEM),
            pl.BlockSpec((1, 1, bm), lambda i, be, nu: (jnp.minimum(i + 1, nb - 1), 0, 0),
                         memory_space=pltpu.SMEM),
            pl.BlockSpec(memory_space=pl.ANY),
            pl.BlockSpec((None, None, d, dff2), lambda i, be, nu: (layer, be[i], 0, 0)),
            pl.BlockSpec((None, None, 1, dff2), lambda i, be, nu: (layer, be[i], 0, 0)),
            pl.BlockSpec((None, None, dff, d), lambda i, be, nu: (layer, be[i], 0, 0)),
            pl.BlockSpec((None, None, 1, d), lambda i, be, nu: (layer, be[i], 0, 0)),
        ],
        out_specs=pl.BlockSpec((bm, sub, dl), lambda i, be, nu: (i, 0, 0)),
        scratch_shapes=[
            pltpu.VMEM((half, sub, dl), F32),
            pltpu.VMEM((half, sub, dl), F32),
            pltpu.VMEM((half, sub, dl), F32),
            pltpu.VMEM((half, sub, dl), F32),
            pltpu.SemaphoreType.DMA((1,)),
            pltpu.SemaphoreType.DMA((1,)),
            pltpu.SemaphoreType.DMA((1,)),
            pltpu.SemaphoreType.DMA((1,)),
            pltpu.VMEM((half, d), BF16),
            pltpu.VMEM((half, dff), BF16),
            pltpu.VMEM((d, dff2), BF16),
            pltpu.VMEM((dff, d), BF16),
        ],
    )
    return pl.pallas_call(
        _expert_kernel,
        grid_spec=grid_spec,
        out_shape=jax.ShapeDtypeStruct((nb * bm, sub, dl), F32),
        compiler_params=_cparams(("arbitrary",), no_bounds_checks=True),
        name="moe_experts",
    )(block_e, n_used, tok3, tok3, hx, w_gu, b_gu.reshape(depth, ne, 1, dff2), w_dn,
      b_dn.reshape(depth, ne, 1, d))


def _combine_kernel(dst_ref, dstn_ref, ys_hbm, gw_ref, x_ref, mod_ref, g_ref, o_ref, buf, sem):
    i = pl.program_id(0)
    nsteps = pl.num_programs(0)
    slot = i % 2
    n_copy = TOP_K * TILE

    def row_copy(dst_smem, e, sl):
        src = dst_smem[0, 0, e]
        kk = e // TILE
        r = e % TILE
        return pltpu.make_async_copy(ys_hbm.at[src], buf.at[sl, kk, r], sem.at[sl])

    @pl.when(i == 0)
    def _():
        def body(e, carry):
            row_copy(dst_ref, e, 0).start()
            return carry
        lax.fori_loop(0, n_copy, body, 0)

    @pl.when(i + 1 < nsteps)
    def _():
        for e in range(n_copy):
            row_copy(dstn_ref, e, 1 - slot).start(priority=e % 2)

    for kk in range(TOP_K):
        pltpu.make_async_copy(ys_hbm.at[pl.ds(0, TILE)], buf.at[slot, kk], sem.at[slot]).wait()

    gw = gw_ref[...]
    y = gw[:, 0:1] * buf[slot, 0].reshape(x_ref.shape)
    for kk in range(1, TOP_K):
        y = y + gw[:, kk:kk + 1] * buf[slot, kk].reshape(x_ref.shape)
    x = x_ref[...] + mod_ref[5:6, :] * y
    ms = jnp.mean(x * x, axis=-1, keepdims=True)
    o_ref[...] = x * lax.rsqrt(ms + RMS_EPS) * g_ref[...]


def _combine_final(ys, dest_tiles, gate_w, xt, modtab, final_g, bsz, ntb, mod_base):
    t, d = xt.shape
    nlat = ntb - 1
    n = bsz * nlat
    row = _mod_row(ntb, mod_base)

    def tile(i):
        return (i // nlat) * ntb + 1 + i % nlat

    return pl.pallas_call(
        _combine_kernel,
        grid=(n,),
        in_specs=[
            pl.BlockSpec((1, 1, TOP_K * TILE), lambda i: (tile(i), 0, 0), memory_space=pltpu.SMEM),
            pl.BlockSpec((1, 1, TOP_K * TILE), lambda i: (tile(jnp.minimum(i + 1, n - 1)), 0, 0),
                         memory_space=pltpu.SMEM),
            pl.BlockSpec(memory_space=pl.ANY),
            pl.BlockSpec((TILE, TOP_K), lambda i: (tile(i), 0)),
            pl.BlockSpec((TILE, d), lambda i: (tile(i), 0)),
            pl.BlockSpec((None, 6, d), lambda i: (row(tile(i)), 0, 0)),
            pl.BlockSpec((1, d), lambda i: (0, 0)),
        ],
        out_specs=pl.BlockSpec((TILE, d), lambda i: (i, 0)),
        out_shape=jax.ShapeDtypeStruct((n * TILE, d), F32),
        scratch_shapes=[
            pltpu.VMEM((2, TOP_K, TILE) + ys.shape[1:], F32),
            pltpu.SemaphoreType.DMA((2,)),
        ],
        compiler_params=_cparams(("arbitrary",), no_bounds_checks=True),
        name="moe_combine_final",
    )(dest_tiles, dest_tiles, ys, gate_w, xt, modtab, final_g)


IN_CHUNKS = 6
IN_DMA_CHUNKS = 4


def _comb_in_kernel(dst_ref, dstn_ref, ys_hbm, gw_ref, x_ref, modp_ref, g_ref, mod_ref, w_ref, xo_ref, o_ref,
                    buf_a, buf_b, sem_a, sem_b, fsem, hb_s):
    i = pl.program_id(0)
    nsteps = pl.num_programs(0)
    n_copy = TOP_K * TILE
    n = w_ref.shape[1]
    cw = n // IN_CHUNKS
    per = n_copy // IN_DMA_CHUNKS

    def row_copy(dst_smem, e, buf, sem):
        src = dst_smem[0, 0, e]
        return pltpu.make_async_copy(ys_hbm.at[src], buf.at[e // TILE, e % TILE], sem.at[0])

    def wait_rows(buf, sem):
        for kk in range(TOP_K):
            pltpu.make_async_copy(ys_hbm.at[pl.ds(0, TILE)], buf.at[kk], sem.at[0]).wait()

    @pl.when(i == 0)
    def _():
        def body(e, carry):
            row_copy(dst_ref, e, buf_a, sem_a).start()
            return carry
        lax.fori_loop(0, n_copy, body, 0)

    def tile(buf_cur, sem_cur, buf_nxt, sem_nxt):
        wait_rows(buf_cur, sem_cur)
        gw = gw_ref[...]
        y = gw[:, 0:1] * buf_cur[0].reshape(x_ref.shape)
        for kk in range(1, TOP_K):
            y = y + gw[:, kk:kk + 1] * buf_cur[kk].reshape(x_ref.shape)
        x = x_ref[...] + modp_ref[5:6, :] * y
        xo_ref[...] = x
        ms = jnp.mean(x * x, axis=-1, keepdims=True)
        h = x * lax.rsqrt(ms + RMS_EPS) * g_ref[...]
        h = h * (1.0 + mod_ref[1:2, :]) + mod_ref[0:1, :]
        hb_s[...] = h.astype(BF16)
        for c in range(IN_CHUNKS):
            if c < IN_DMA_CHUNKS:
                for e in range(c * per, (c + 1) * per):
                    row_copy(dstn_ref, e, buf_nxt, sem_nxt).start(priority=e % 2)
            if c == IN_DMA_CHUNKS:
                pl.semaphore_signal(fsem, 1)
                pl.semaphore_wait(fsem, 1)
            cs = slice(c * cw, (c + 1) * cw)
            o_ref[:, cs] = jnp.dot(hb_s[...], w_ref[:, cs], preferred_element_type=F32).astype(o_ref.dtype)

    even = i % 2 == 0

    @pl.when(even)
    def _():
        tile(buf_a, sem_a, buf_b, sem_b)

    @pl.when(jnp.logical_not(even))
    def _():
        tile(buf_b, sem_b, buf_a, sem_a)

    @pl.when(jnp.logical_and(i == nsteps - 1, even))
    def _():
        wait_rows(buf_b, sem_b)

    @pl.when(jnp.logical_and(i == nsteps - 1, jnp.logical_not(even)))
    def _():
        wait_rows(buf_a, sem_a)


def _comb_in_proj(ys, dest_tiles, gate_w, xt, g1, modtab, w_in_bf, ntb, mod_base_prev, mod_base):
    t, d = xt.shape
    n = w_in_bf.shape[1]
    nt = t // TILE
    assert n % (IN_CHUNKS * 256) == 0 and (TOP_K * TILE) % IN_DMA_CHUNKS == 0
    row_prev = _mod_row(ntb, mod_base_prev)
    row = _mod_row(ntb, mod_base)
    return pl.pallas_call(
        _comb_in_kernel,
        grid=(nt,),
        in_specs=[
            pl.BlockSpec((1, 1, TOP_K * TILE), lambda i: (i, 0, 0), memory_space=pltpu.SMEM),
            pl.BlockSpec((1, 1, TOP_K * TILE), lambda i: (jnp.minimum(i + 1, nt - 1), 0, 0),
                         memory_space=pltpu.SMEM),
            pl.BlockSpec(memory_space=pl.ANY),
            pl.BlockSpec((TILE, TOP_K), lambda i: (i, 0)),
            pl.BlockSpec((TILE, d), lambda i: (i, 0)),
            pl.BlockSpec((None, 6, d), lambda i: (row_prev(i), 0, 0)),
            pl.BlockSpec((1, d), lambda i: (0, 0)),
            pl.BlockSpec((None, 6, d), lambda i: (row(i), 0, 0)),
            pl.BlockSpec((d, n), lambda i: (0, 0), pipeline_mode=pl.Buffered(1)),
        ],
        out_specs=[
            pl.BlockSpec((TILE, d), lambda i: (i, 0)),
            pl.BlockSpec((TILE, n), lambda i: (i, 0)),
        ],
        out_shape=[
            jax.ShapeDtypeStruct((t, d), F32),
            jax.ShapeDtypeStruct((t, n), BF16),
        ],
        scratch_shapes=[
            pltpu.VMEM((TOP_K, TILE) + ys.shape[1:], F32),
            pltpu.VMEM((TOP_K, TILE) + ys.shape[1:], F32),
            pltpu.SemaphoreType.DMA((1,)),
            pltpu.SemaphoreType.DMA((1,)),
            pltpu.SemaphoreType.REGULAR,
            pltpu.VMEM((TILE, d), BF16),
        ],
        compiler_params=_cparams(("arbitrary",), no_bounds_checks=True),
        name="comb_in_proj",
    )(dest_tiles, dest_tiles, ys, gate_w, xt, modtab, g1, modtab, w_in_bf)


def _route_tile(lg, te_ref, gw_ref, pos_ref, cnt_ref, cnt_s):
    @pl.when(pl.program_id(0) == 0)
    def _():
        cnt_s[...] = jnp.zeros(cnt_s.shape, F32)

    ne = lg.shape[0]
    sub = lax.broadcasted_iota(jnp.int32, lg.shape, 0)
    work = lg
    vals, idxs, hots = [], [], []
    for _ in range(TOP_K):
        m = jnp.max(work, axis=0, keepdims=True)
        idx = jnp.min(jnp.where(work == m, sub, ne), axis=0, keepdims=True)
        hot = sub == idx
        vals.append(m)
        idxs.append(idx)
        hots.append(jnp.where(hot, 1.0, 0.0))
        work = jnp.where(hot, -jnp.inf, work)
    exps = [jnp.exp(v - vals[0]) for v in vals]
    ssum = exps[0]
    for e in exps[1:]:
        ssum = ssum + e
    hot_all = hots[0]
    for h in hots[1:]:
        hot_all = hot_all + h
    r2 = lax.broadcasted_iota(jnp.int32, (TILE, TILE), 0)
    c2 = lax.broadcasted_iota(jnp.int32, (TILE, TILE), 1)
    earlier = jnp.where(r2 < c2, 1.0, 0.0).astype(BF16)
    before = jnp.dot(hot_all.astype(BF16), earlier, preferred_element_type=F32) + cnt_s[...]
    for kk in range(TOP_K):
        te_ref[kk:kk + 1, :] = idxs[kk]
        gw_ref[kk:kk + 1, :] = exps[kk] / ssum
        pos_ref[kk:kk + 1, :] = jnp.sum(hots[kk] * before, axis=0, keepdims=True).astype(jnp.int32)
    cnt_s[...] = cnt_s[...] + jnp.sum(hot_all, axis=1, keepdims=True)
    cnt_ref[...] = cnt_s[...]


def _routing(top_e, gate, pos, cnt, n_tiles):
    t = top_e.shape[1]
    n_assign = t * TOP_K
    experts = jnp.arange(N_EXPERTS, dtype=jnp.int32)
    key = (top_e * t + jnp.arange(t, dtype=jnp.int32)[None, :]).reshape(-1)
    order = jnp.argsort(key).astype(jnp.int32)
    counts = cnt.reshape(N_EXPERTS).astype(jnp.int32)
    grp_start = jnp.cumsum(counts) - counts
    padded = (counts + MOE_BM - 1) // MOE_BM * MOE_BM
    pad_end = jnp.cumsum(padded)
    pad_start = pad_end - padded
    dest = pos + jnp.sum(jnp.where(top_e[:, :, None] == experts[None, None, :], pad_start[None, None, :], 0), axis=2)
    n_blocks = n_assign // MOE_BM + N_EXPERTS
    blk = jnp.arange(n_blocks, dtype=jnp.int32)
    block_e = jnp.minimum(jnp.sum((pad_end[None, :] <= (blk * MOE_BM)[:, None]).astype(jnp.int32), axis=1),
                          N_EXPERTS - 1)
    src0 = grp_start[block_e] + blk * MOE_BM - pad_start[block_e]
    lim = grp_start[block_e] + counts[block_e]
    src = src0[:, None] + jnp.arange(MOE_BM, dtype=jnp.int32)[None, :]
    valid = src < lim[:, None]
    row_tok = jnp.where(valid, order[jnp.clip(src, 0, n_assign - 1)] % t, 0).astype(jnp.int32).reshape(-1)
    n_used = (pad_end[-1] // MOE_BM).astype(jnp.int32).reshape(1)
    dest_tiles = dest.reshape(TOP_K, n_tiles, TILE).transpose(1, 0, 2).reshape(n_tiles, 1, TOP_K * TILE)
    return gate.T, row_tok, block_e, n_used, dest_tiles


def _block_diag(w, heads):
    nb, c, _ = w.shape
    per = nb // heads
    w = w.reshape(heads, per, c, c)
    eye = jnp.eye(per, dtype=w.dtype)
    full = eye[None, :, None, :, None] * w[:, :, :, None, :]
    return full.reshape(heads, per * c, per * c)


def kernel(x, c, ctx, c_ctx, w_ada, b_ada, norm1_g, norm2_g, w_in, conv_dw_w, conv_dw_b, conv_ln_g, conv_ln_b, w_conv_out, m_sc_w, m_sc_b, m_wq, m_wk, m_w_if, m_b_if, m_norm_g, m_skip, w_m_out, pool_w, pool_scale, w_pool_out, w_out, router_w, router_b, w_gu, b_gu, w_dn, b_dn, final_g):
    bsz, seq, d = x.shape
    ctx_len = ctx.shape[1]
    depth = w_ada.shape[0]
    assert ctx_len == TILE and seq % TILE == 0 and d % (128 * HEADS) == 0
    nt = ctx_len + seq
    ntb = nt // TILE
    t = bsz * nt
    n_tiles = t // TILE
    d_conv = w_conv_out.shape[1]
    d_pool = w_pool_out.shape[1]
    d_m = w_m_out.shape[1]
    assert d_m == d and 2 * d_conv == d and 2 * d_pool == d
    c_conv = 3 * d
    c_pool = c_conv + 2 * d_conv
    c_mz = c_pool + d_pool
    c_mqk = c_mz + d_m
    c_mv = c_mqk + d_m
    d_in = c_mv + d_m
    col_conv, col_z, col_u, col_v = 3, 4, 5, 6
    col_pool = (7 * d) // d_pool

    xt = jnp.concatenate([ctx, x], axis=1).reshape(t, d)

    mp = -(-(bsz + 1) // 8) * 8
    c_all = jnp.zeros((mp, d), F32).at[:bsz].set(c).at[bsz].set(c_ctx)
    mod = _ada(c_all, w_ada, b_ada)
    mod_x = mod[:, :bsz].reshape(depth, bsz, 1, 6, d)
    mod_c = jnp.broadcast_to(mod[:, bsz].reshape(depth, 1, 1, 6, d), (depth, bsz, 1, 6, d))
    modtab = jnp.concatenate([mod_c, mod_x], axis=2).reshape(depth * bsz * 2, 6, d)

    gh = 2 * HEADS
    for l in range(depth):
        mod_base = l * bsz * 2
        w_in_bf = jnp.concatenate([w_in[l, :, 0:c_pool], w_in[l, :, c_mz:d_in], w_in[l, :, c_pool:c_mz]],
                                  axis=1).astype(BF16)
        if l == 0:
            px = _in_proj(xt, norm1_g[l].reshape(1, d), modtab, w_in_bf, ntb, mod_base)
        else:
            xt, px = _comb_in_proj(ys, dest_tiles, gate_w, xt, norm1_g[l].reshape(1, d), modtab, w_in_bf, ntb,
                                   mod_base - bsz * 2, mod_base)

        wq_bd = _block_diag(m_wq[l], HEADS).astype(BF16)
        wk_bd = _block_diag(m_wk[l], HEADS).astype(BF16)
        wif = jnp.concatenate([m_w_if[l, 0], m_w_if[l, 1]], axis=1)
        bif = jnp.concatenate([m_b_if[l, 0], m_b_if[l, 1]], axis=0)
        a, q, k, gates, gates_t = _mlstm_pre(
            px, col_u, col_v, m_sc_w[l], m_sc_b[l].reshape(1, d), wq_bd, wk_bd,
            wif.astype(BF16), wif.T.astype(BF16), bif.reshape(1, 2 * gh), bif.reshape(2 * gh, 1), bsz, ntb)
        hf, hr = _mlstm_scan(q, k, px, col_v, gates, gates_t, bsz, ntb)

        y_conv = _conv_module(px, col_conv, conv_dw_w[l], conv_dw_b[l].reshape(1, d_conv),
                              conv_ln_g[l].reshape(1, d_conv), conv_ln_b[l].reshape(1, d_conv),
                              w_conv_out[l].astype(BF16), bsz, ntb)
        y_pool = _pool_module(px, col_pool, pool_w[l].astype(BF16), pool_scale[l].reshape(1, d_pool),
                              w_pool_out[l].astype(BF16), bsz, nt, ctx_len)

        xt, hx, top_e, gate, pos, cnt = _mix_out(
            xt, px, col_z, a, hf, hr, y_conv, y_pool, modtab, m_norm_g[l].reshape(1, d), m_skip[l].reshape(1, d),
            w_m_out[l].astype(BF16), w_out[l].astype(BF16), norm2_g[l].reshape(1, d), router_w[l].T.astype(BF16),
            router_b[l].reshape(N_EXPERTS, 1), ntb, mod_base)

        gate_w, row_tok, block_e, n_used, dest_tiles = _routing(top_e, gate, pos, cnt, n_tiles)
        ys = _experts(l, hx, row_tok, block_e, n_used, w_gu, b_gu, w_dn, b_dn)

    out = _combine_final(ys, dest_tiles, gate_w, xt, modtab, final_g.reshape(1, d), bsz, ntb,
                         (depth - 1) * bsz * 2)
    return out.reshape(bsz, seq, d)
```
